```python
import math
import jax, jax.numpy as jnp
from jax import lax
import numpy as np

D_MODEL = 1024
BATCH = 16
SEQ = 2048
DEPTH = 2

D_MIX = D_MODEL
LRU_WIDTH = D_MIX // 4
LRU_BLOCKS = 4
LRU_BLOCK = LRU_WIDTH // LRU_BLOCKS
CONV_WIDTH = 4
LRU_C = 8.0
HG_HEADS = 4
HG_DK = 64
HG_DV = 64
HG_KDIM = HG_HEADS * HG_DK
HG_WIDTH = HG_HEADS * HG_DV
HG_CHUNK = 64
MLA_HEADS = 8
MLA_NOPE = 64
MLA_ROPE = 32
MLA_V = 64
MLA_QK = MLA_NOPE + MLA_ROPE
MLA_Q_RANK = 256
MLA_KV_RANK = 128
MLA_WIDTH = MLA_HEADS * MLA_V
ATTN_BLOCK = 128
ROPE_THETA = 10000.0
D_FF = 2752
N_EXPERTS = 8
TOP_K = 2
D_EXPERT = 3584
N_DENSE = (DEPTH + 1) // 2
N_MOE = DEPTH // 2
EPS = 1e-6
IN_SIZES = (LRU_WIDTH, LRU_WIDTH, HG_KDIM, HG_KDIM, HG_KDIM, HG_WIDTH, HG_WIDTH, MLA_Q_RANK, MLA_KV_RANK, MLA_ROPE)
D_IN = LRU_WIDTH * 2 + HG_KDIM * 3 + HG_WIDTH * 2 + MLA_Q_RANK + MLA_KV_RANK + MLA_ROPE

kernel_name = 'hybrid_rglru_hgrn2_mla_moe_encoder'


def rms_norm(x, g):
    xf = x.astype(jnp.float32)
    y = xf * lax.rsqrt(jnp.mean(xf * xf, axis=-1, keepdims=True) + EPS)
    return (y * g.astype(jnp.float32)).astype(x.dtype)


def split_columns(proj):
    outs, start = [], 0
    for size in IN_SIZES:
        outs.append(proj[..., start:start + size])
        start += size
    return outs


def centred_dwconv(x, w, b):
    S = x.shape[1]
    pad_l = CONV_WIDTH // 2
    xp = jnp.pad(x, ((0, 0), (pad_l, CONV_WIDTH - 1 - pad_l), (0, 0)))
    y = b
    for k in range(CONV_WIDTH):
        y = y + xp[:, k:k + S] * w[k]
    return y


def linear_scan(a, b, reverse):
    def combine(l, r):
        return r[0] * l[0], r[0] * l[1] + r[1]
    return lax.associative_scan(combine, (a, b), axis=1, reverse=reverse)[1]


def rg_lru(x, w_a, b_a, w_x, b_x, lam, reverse):
    B, S, W = x.shape
    f32 = jnp.float32
    xb = x.reshape(B, S, LRU_BLOCKS, LRU_BLOCK)
    r = jax.nn.sigmoid(jnp.einsum('bsnd,nde->bsne', xb, w_a.astype(f32)).reshape(B, S, W) + b_a.astype(f32))
    i = jax.nn.sigmoid(jnp.einsum('bsnd,nde->bsne', xb, w_x.astype(f32)).reshape(B, S, W) + b_x.astype(f32))
    log_a = -LRU_C * r * jax.nn.softplus(-lam.astype(f32))
    a = jnp.exp(log_a)
    mult = jnp.sqrt(-jnp.expm1(2.0 * log_a))
    return linear_scan(a, mult * (i * x), reverse)


def rglru_mixer(xa, gate, conv_w, conv_b, w_a, b_a, w_x, b_x, lam):
    xc = centred_dwconv(xa, conv_w, conv_b).astype(jnp.float32)
    h = (rg_lru(xc, w_a[0], b_a[0], w_x[0], b_x[0], lam[0], False)
         + rg_lru(xc, w_a[1], b_a[1], w_x[1], b_x[1], lam[1], True))
    return (h * jax.nn.gelu(gate.astype(jnp.float32))).astype(xa.dtype)


def gla_chunked(q, k, v, logf):
    B, H, S, dk = q.shape
    dv = v.shape[-1]
    n = S // HG_CHUNK

    def to_chunks(t):
        return jnp.moveaxis(t.reshape(B, H, n, HG_CHUNK, t.shape[-1]), 2, 0)

    lower = jnp.tril(jnp.ones((HG_CHUNK, HG_CHUNK), dtype=bool))[:, :, None]

    def step(state, inp):
        qc, kc, vc, gc = inp
        bcum = jnp.cumsum(gc, axis=2)
        b_last = bcum[:, :, -1:, :]
        o_inter = jnp.einsum('bhtd,bhde->bhte', qc * jnp.exp(bcum), state)
        diff = bcum[:, :, :, None, :] - bcum[:, :, None, :, :]
        decay = jnp.exp(jnp.where(lower, diff, -jnp.inf))
        scores = jnp.einsum('bhtd,bhsd,bhtsd->bhts', qc, kc, decay)
        o = o_inter + jnp.einsum('bhts,bhse->bhte', scores, vc)
        new_state = (state * jnp.exp(b_last[:, :, 0, :, None])
                     + jnp.einsum('bhsd,bhse->bhde', kc * jnp.exp(b_last - bcum), vc))
        return new_state, o

    init = jnp.zeros((B, H, dk, dv), jnp.float32)
    _, o = lax.scan(step, init, (to_chunks(q), to_chunks(k), to_chunks(v), to_chunks(logf)))
    return jnp.moveaxis(o, 0, 2).reshape(B, H, S, dv)


def hgrn2_mixer(q_in, ff_in, fb_in, i_in, g_in, lb, norm_g):
    B, S, _ = q_in.shape
    f32 = jnp.float32

    def heads(t):
        return t.astype(f32).reshape(B, S, HG_HEADS, -1).transpose(0, 2, 1, 3)

    def gates(z, lb_dir):
        f = lb_dir.astype(f32) + (1.0 - lb_dir.astype(f32)) * jax.nn.sigmoid(z.astype(f32))
        return heads(jnp.log(f)), heads(1.0 - f)

    q = heads(jax.nn.silu(q_in.astype(f32)) * HG_DK ** -0.5)
    v = heads(i_in)
    logf_f, k_f = gates(ff_in, lb[0])
    logf_b, k_b = gates(fb_in, lb[1])
    rev = lambda t: jnp.flip(t, axis=2)
    o_fwd = gla_chunked(q, k_f, v, logf_f)
    o_bwd = rev(gla_chunked(rev(q), rev(k_b), rev(v), rev(logf_b)))
    o = (o_fwd + o_bwd).transpose(0, 2, 1, 3)
    o = rms_norm(o, norm_g).reshape(B, S, HG_WIDTH)
    return (o * jax.nn.silu(g_in.astype(f32))).astype(q_in.dtype)


def apply_rope_tail(t, cos, sin):
    t_nope, t_rope = t[..., :MLA_NOPE], t[..., MLA_NOPE:]
    half = MLA_ROPE // 2
    x1, x2 = t_rope[..., :half].astype(jnp.float32), t_rope[..., half:].astype(jnp.float32)
    rot = jnp.concatenate([x1 * cos - x2 * sin, x2 * cos + x1 * sin], axis=-1).astype(t.dtype)
    return jnp.concatenate([t_nope, rot], axis=-1)


def dense_attention_blocked(q, k, v):
    B, S, H, dq = q.shape
    n = S // ATTN_BLOCK
    scale = dq ** -0.5
    qb = jnp.moveaxis(q.reshape(B, n, ATTN_BLOCK, H, dq), 1, 0)

    def one_block(q_blk):
        s = jnp.einsum('bqhd,bkhd->bhqk', q_blk, k).astype(jnp.float32) * scale
        p = jax.nn.softmax(s, axis=-1).astype(v.dtype)
        return jnp.einsum('bhqk,bkhd->bqhd', p, v)

    o = lax.map(one_block, qb)
    return jnp.moveaxis(o, 0, 1).reshape(B, S, H, v.shape[-1])


def mla_mixer(c_q, c_kv, k_rope, positions, q_norm_g, w_uq, kv_norm_g, w_ukv, qk_g_q, qk_g_k):
    B, S, _ = c_q.shape
    q = (rms_norm(c_q, q_norm_g) @ w_uq).reshape(B, S, MLA_HEADS, MLA_QK)
    kv = (rms_norm(c_kv, kv_norm_g) @ w_ukv).reshape(B, S, MLA_HEADS, MLA_NOPE + MLA_V)
    k_nope, v = kv[..., :MLA_NOPE], kv[..., MLA_NOPE:]
    k = jnp.concatenate([k_nope, jnp.broadcast_to(k_rope[:, :, None, :], (B, S, MLA_HEADS, MLA_ROPE))], axis=-1)
    q = rms_norm(q, qk_g_q)
    k = rms_norm(k, qk_g_k)
    inv_freq = ROPE_THETA ** (-jnp.arange(0, MLA_ROPE // 2, dtype=jnp.float32) * (2.0 / MLA_ROPE))
    ang = positions.astype(jnp.float32)[..., None] * inv_freq
    cos, sin = jnp.cos(ang)[:, :, None, :], jnp.sin(ang)[:, :, None, :]
    q = apply_rope_tail(q, cos, sin)
    k = apply_rope_tail(k, cos, sin)
    o = dense_attention_blocked(q, k, v)
    return o.reshape(B, S, MLA_WIDTH)


def swiglu_dense(h, w_gate_up, w_down):
    gu = h @ w_gate_up
    g, u = gu[..., :D_FF], gu[..., D_FF:]
    return (jax.nn.silu(g) * u) @ w_down


def moe_swiglu(h, router, w1, w3, w2):
    B, S, D = h.shape
    ht = h.reshape(B * S, D)
    logits = (ht @ router).astype(jnp.float32)
    top_v, top_i = lax.top_k(logits, TOP_K)
    top_w = jax.nn.softmax(top_v, axis=-1)
    combine = jnp.sum(jax.nn.one_hot(top_i, N_EXPERTS, dtype=jnp.float32) * top_w[..., None], axis=1).astype(h.dtype)
    y = jnp.zeros_like(ht)
    for e in range(N_EXPERTS):
        act = jax.nn.silu(ht @ w1[e]) * (ht @ w3[e])
        y = y + (act * combine[:, e:e + 1]) @ w2[e]
    return y.reshape(B, S, D)


def setup_inputs(seed: int = 0) -> dict:
    key = jax.random.key(seed)
    ks = jax.random.split(key, 32)
    f32 = jnp.float32
    nrm = lambda k, shape, scale: jax.random.normal(k, shape, f32) * scale
    gain = lambda k, shape: 1.0 + 0.02 * jax.random.normal(k, shape, f32)
    u = jax.random.uniform(ks[10], (DEPTH, 2, LRU_WIDTH), f32, minval=0.9, maxval=0.999)
    a0 = u ** (1.0 / LRU_C)
    lam = jnp.log(a0) - jnp.log1p(-a0)
    positions = jnp.arange(SEQ, dtype=jnp.int32)[None, :] + jax.random.randint(ks[1], (BATCH, 1), 0, 4096, dtype=jnp.int32)
    return {
        'x': jax.random.normal(ks[0], (BATCH, SEQ, D_MODEL), f32),
        'positions': positions,
        'norm_mix': gain(ks[2], (DEPTH, D_MODEL)),
        'w_in': nrm(ks[3], (DEPTH, D_MODEL, D_IN), D_MODEL ** -0.5),
        'conv_w': nrm(ks[4], (DEPTH, CONV_WIDTH, LRU_WIDTH), CONV_WIDTH ** -0.5),
        'conv_b': nrm(ks[5], (DEPTH, LRU_WIDTH), 0.01),
        'lru_wa': nrm(ks[6], (DEPTH, 2, LRU_BLOCKS, LRU_BLOCK, LRU_BLOCK), LRU_BLOCK ** -0.5),
        'lru_ba': nrm(ks[7], (DEPTH, 2, LRU_WIDTH), 0.01),
        'lru_wx': nrm(ks[8], (DEPTH, 2, LRU_BLOCKS, LRU_BLOCK, LRU_BLOCK), LRU_BLOCK ** -0.5),
        'lru_bx': nrm(ks[9], (DEPTH, 2, LRU_WIDTH), 0.01),
        'lru_lam': lam,
        'out_g_a': gain(ks[11], (DEPTH, LRU_WIDTH)),
        'hg_lb_logits': nrm(ks[12], (DEPTH, 2, HG_KDIM), 0.1),
        'hg_norm_g': gain(ks[13], (DEPTH, HG_DV)),
        'mla_q_norm': gain(ks[14], (DEPTH, MLA_Q_RANK)),
        'mla_w_uq': nrm(ks[15], (DEPTH, MLA_Q_RANK, MLA_HEADS * MLA_QK), MLA_Q_RANK ** -0.5),
        'mla_kv_norm': gain(ks[16], (DEPTH, MLA_KV_RANK)),
        'mla_w_ukv': nrm(ks[17], (DEPTH, MLA_KV_RANK, MLA_HEADS * (MLA_NOPE + MLA_V)), MLA_KV_RANK ** -0.5),
        'qk_norm_q': gain(ks[18], (DEPTH, MLA_QK)),
        'qk_norm_k': gain(ks[19], (DEPTH, MLA_QK)),
        'out_g_c': gain(ks[20], (DEPTH, MLA_WIDTH)),
        'w_out': nrm(ks[21], (DEPTH, D_MIX, D_MODEL), D_MIX ** -0.5),
        'norm_ffn': gain(ks[22], (DEPTH, D_MODEL)),
        'ffn_w_gate_up': nrm(ks[23], (N_DENSE, D_MODEL, 2 * D_FF), D_MODEL ** -0.5),
        'ffn_w_down': nrm(ks[24], (N_DENSE, D_FF, D_MODEL), D_FF ** -0.5),
        'moe_router': nrm(ks[25], (N_MOE, D_MODEL, N_EXPERTS), D_MODEL ** -0.5),
        'moe_w1': nrm(ks[26], (N_MOE, N_EXPERTS, D_MODEL, D_EXPERT), D_MODEL ** -0.5),
        'moe_w3': nrm(ks[27], (N_MOE, N_EXPERTS, D_MODEL, D_EXPERT), D_MODEL ** -0.5),
        'moe_w2': nrm(ks[28], (N_MOE, N_EXPERTS, D_EXPERT, D_MODEL), D_EXPERT ** -0.5),
    }


def reference(x, positions, norm_mix, w_in, conv_w, conv_b, lru_wa, lru_ba, lru_wx, lru_bx, lru_lam,
              out_g_a, hg_lb_logits, hg_norm_g, mla_q_norm, mla_w_uq, mla_kv_norm, mla_w_ukv,
              qk_norm_q, qk_norm_k, out_g_c, w_out, norm_ffn, ffn_w_gate_up, ffn_w_down,
              moe_router, moe_w1, moe_w3, moe_w2):
    p = jax.nn.softmax(hg_lb_logits.astype(jnp.float32), axis=0)
    lower_bounds = jnp.cumsum(p, axis=0) - p[0:1]
    for l in range(DEPTH):
        h = rms_norm(x, norm_mix[l])
        (xa, ga, hq, hff, hfb, hi, hg, cq, ckv, kr) = split_columns(h @ w_in[l])
        y_a = rglru_mixer(xa, ga, conv_w[l], conv_b[l], lru_wa[l], lru_ba[l], lru_wx[l], lru_bx[l], lru_lam[l])
        y_b = hgrn2_mixer(hq, hff, hfb, hi, hg, lower_bounds[l], hg_norm_g[l])
        y_c = mla_mixer(cq, ckv, kr, positions, mla_q_norm[l], mla_w_uq[l], mla_kv_norm[l], mla_w_ukv[l],
                        qk_norm_q[l], qk_norm_k[l])
        mixed = jnp.concatenate([rms_norm(y_a, out_g_a[l]), y_b, rms_norm(y_c, out_g_c[l])], axis=-1)
        x = x + mixed @ w_out[l]
        h = rms_norm(x, norm_ffn[l])
        if l % 2 == 0:
            x = x + swiglu_dense(h, ffn_w_gate_up[l // 2], ffn_w_down[l // 2])
        else:
            x = x + moe_swiglu(h, moe_router[l // 2], moe_w1[l // 2], moe_w3[l // 2], moe_w2[l // 2])
    return x
```

```python
import functools
import math

import jax
import jax.numpy as jnp
from jax import lax
from jax.experimental import pallas as pl
from jax.experimental.pallas import tpu as pltpu

F32 = jnp.float32
BF16 = jnp.bfloat16

D_MODEL = 1024
LRU_WIDTH = 256
LRU_BLOCKS = 4
CONV_WIDTH = 4
LRU_C = 8.0
HG_HEADS = 4
HG_DK = 64
HG_WIDTH = 256
MLA_HEADS = 8
MLA_NOPE = 64
MLA_ROPE = 32
MLA_V = 64
MLA_QK = MLA_NOPE + MLA_ROPE
MLA_Q_RANK = 256
MLA_KV_RANK = 128
MLA_WIDTH = MLA_HEADS * MLA_V
ROPE_THETA = 10000.0
D_FF = 2752
N_EXPERTS = 8
D_EXPERT = 3584
EPS = 1e-6

LANE = 128
SUBLANE = 8
HEAD_SLOT = 128
HG_CHUNK = 64
D_FF_PAD = 2816
VMEM_LIMIT = 56 * 1024 * 1024

PA_W, PB_W, PC_W = 512, 1280, 512


def _cparams(sem):
  return pltpu.CompilerParams(dimension_semantics=sem, vmem_limit_bytes=VMEM_LIMIT)


def _sigmoid(x):
  return 1.0 / (1.0 + jnp.exp(-x))


def _silu(x):
  return x * _sigmoid(x)


def _rms(x, gain, n=None):
  n = x.shape[-1] if n is None else n
  ms = jnp.sum(x * x, axis=-1, keepdims=True) * (1.0 / n)
  return x * lax.rsqrt(ms + EPS) * gain


def _norm_inproj_kernel(x_ref, g_ref, w_ref, oa_ref, ob_ref, oc_ref):
  h = _rms(x_ref[...], g_ref[...]).astype(BF16)
  p = jnp.dot(h, w_ref[...], preferred_element_type=F32)
  oa_ref[...] = p[:, :PA_W]
  ob_ref[...] = p[:, PA_W:PA_W + PB_W]
  oc_ref[...] = p[:, PA_W + PB_W:]


def _norm_inproj(x2, gain, w, tm):
  T = x2.shape[0]
  n = w.shape[1]
  return pl.pallas_call(
      _norm_inproj_kernel,
      grid=(T // tm,),
      in_specs=[
          pl.BlockSpec((tm, D_MODEL), lambda i: (i, 0)),
          pl.BlockSpec((1, D_MODEL), lambda i: (0, 0)),
          pl.BlockSpec((D_MODEL, n), lambda i: (0, 0)),
      ],
      out_specs=[
          pl.BlockSpec((tm, PA_W), lambda i: (i, 0)),
          pl.BlockSpec((tm, PB_W), lambda i: (i, 0)),
          pl.BlockSpec((tm, PC_W), lambda i: (i, 0)),
      ],
      out_shape=[
          jax.ShapeDtypeStruct((T, PA_W), F32),
          jax.ShapeDtypeStruct((T, PB_W), F32),
          jax.ShapeDtypeStruct((T, PC_W), F32),
      ],
      compiler_params=_cparams(("parallel",)),
      name="norm_inproj",
  )(x2, gain, w)


def _rglru_kernel(pa_ref, cw_ref, cb_ref, wg_ref, bg_ref, lam_ref, og_ref, o_ref,
                  af_ref, bf_ref, ab_ref, bb_ref, hf_ref, hb_ref):
  S = pa_ref.shape[0]
  W = LRU_WIDTH
  xa = pa_ref[:, 0:W]
  row = lax.broadcasted_iota(jnp.int32, (S, W), 0)

  xc = cb_ref[...] + cw_ref[2:3, :] * xa
  for k in (0, 1, 3):
    d = k - CONV_WIDTH // 2
    shifted = pltpu.roll(xa, (-d) % S, 0)
    valid = (row >= -d) if d < 0 else (row < S - d)
    xc = xc + cw_ref[k:k + 1, :] * jnp.where(valid, shifted, 0.0)

  z = jnp.dot(xc.astype(BF16), wg_ref[...], preferred_element_type=F32) + bg_ref[...]
  lam = lam_ref[...]
  sp = jnp.maximum(-lam, 0.0) + jnp.log1p(jnp.exp(-jnp.abs(lam)))
  for d, (a_ref, b_ref) in enumerate(((af_ref, bf_ref), (ab_ref, bb_ref))):
    r = _sigmoid(z[:, d * 2 * W:d * 2 * W + W])
    i = _sigmoid(z[:, d * 2 * W + W:(d + 1) * 2 * W])
    log_a = (-LRU_C) * r * sp[d:d + 1, :]
    a = jnp.exp(log_a)
    a_ref[...] = a
    y2 = 2.0 * log_a
    u = a * a
    near = jnp.where(u == 1.0, y2, (u - 1.0) * y2 / jnp.log(u))
    em1 = jnp.where(y2 > -0.5, near, u - 1.0)
    b_ref[...] = jnp.sqrt(-em1) * (i * xc)

  n = S // SUBLANE
  r8 = lax.broadcasted_iota(jnp.int32, (SUBLANE, W), 0)

  def body(c, carry):
    cf, cb = carry
    sl = pl.ds(pl.multiple_of(c * SUBLANE, SUBLANE), SUBLANE)
    a = af_ref[sl, :]
    b = bf_ref[sl, :]
    for s in (1, 2, 4):
      m = r8 >= s
      b = jnp.where(m, a * pltpu.roll(b, s, 0) + b, b)
      a = jnp.where(m, a * pltpu.roll(a, s, 0), a)
    h = b + a * cf
    hf_ref[sl, :] = h
    cf = h[SUBLANE - 1:SUBLANE, :]

    sl = pl.ds(pl.multiple_of((n - 1 - c) * SUBLANE, SUBLANE), SUBLANE)
    a = ab_ref[sl, :]
    b = bb_ref[sl, :]
    for s in (1, 2, 4):
      m = r8 < SUBLANE - s
      b = jnp.where(m, a * pltpu.roll(b, SUBLANE - s, 0) + b, b)
      a = jnp.where(m, a * pltpu.roll(a, SUBLANE - s, 0), a)
    h = b + a * cb
    hb_ref[sl, :] = h
    cb = h[0:1, :]
    return cf, cb

  zero = jnp.zeros((1, W), F32)
  lax.fori_loop(0, n, body, (zero, zero), unroll=4)

  g = pa_ref[:, W:2 * W]
  gelu = 0.5 * g * (1.0 + jnp.tanh(math.sqrt(2.0 / math.pi) * (g + 0.044715 * (g * g * g))))
  y = (hf_ref[...] + hb_ref[...]) * gelu
  o_ref[...] = _rms(y, og_ref[...]).astype(o_ref.dtype)


def _rglru(pa, conv_w, conv_b, wg, bg, lam, out_g, B, S):
  T = B * S
  W = LRU_WIDTH
  const = lambda shape: pl.BlockSpec(shape, lambda b: (0,) * len(shape))
  return pl.pallas_call(
      _rglru_kernel,
      grid=(B,),
      in_specs=[
          pl.BlockSpec((S, PA_W), lambda b: (b, 0)),
          const((CONV_WIDTH, W)), const((1, W)), const((W, 4 * W)), const((1, 4 * W)),
          const((2, W)), const((1, W)),
      ],
      out_specs=pl.BlockSpec((S, W), lambda b: (b, 0)),
      out_shape=jax.ShapeDtypeStruct((T, W), BF16),
      scratch_shapes=[pltpu.VMEM((S, W), F32) for _ in range(6)],
      compiler_params=_cparams(("parallel",)),
      name="rglru",
  )(pa, conv_w, conv_b, wg, bg, lam, out_g)


_HG_LEVELS = (1, 2, 4, 8, 16, 32)


def _bcast_row_in_blocks(x, block, row):
  L, C = x.shape
  x3 = x.reshape(L // block, block, C)
  return jnp.broadcast_to(x3[:, row:row + 1, :], x3.shape).reshape(L, C)


def _inc_prefix_products(f, row):
  L = f.shape[0]
  out = {1: f}
  p2 = f * jnp.where((row & 1) == 1, pltpu.roll(f, 1, 0), 1.0)
  out[2] = p2
  r4 = row & 3
  p4 = p2 * jnp.where(r4 == 2, pltpu.roll(p2, 1, 0), jnp.where(r4 == 3, pltpu.roll(p2, 2, 0), 1.0))
  out[4] = p4
  p, h = p4, 4
  while h < L:
    p = p * jnp.where((row & h) != 0, _bcast_row_in_blocks(p, 2 * h, h - 1), 1.0)
    h *= 2
    out[h] = p
  return out


def _inc_suffix_products(f, row):
  L = f.shape[0]
  up = lambda x, s: pltpu.roll(x, L - s, 0)
  out = {1: f}
  p2 = f * jnp.where((row & 1) == 0, up(f, 1), 1.0)
  out[2] = p2
  r4 = row & 3
  p4 = p2 * jnp.where(r4 == 1, up(p2, 1), jnp.where(r4 == 0, up(p2, 2), 1.0))
  out[4] = p4
  p, h = p4, 4
  while h < L:
    p = p * jnp.where((row & h) == 0, _bcast_row_in_blocks(p, 2 * h, h), 1.0)
    h *= 2
    out[h] = p
  return out


def _hg_chunk(q, f, v, st_ref, masks_ref, hmask, row, reverse):
  L = q.shape[0]
  k = 1.0 - f
  pre = _inc_prefix_products(f, row)
  suf = _inc_suffix_products(f, row)
  if not reverse:
    q_dec = pre
    k_dec = {h: (jnp.where((row & (h - 1)) == h - 1, 1.0, pltpu.roll(suf[h], L - 1, 0)) if h > 1 else None)
             for h in suf}
  else:
    q_dec = suf
    k_dec = {h: (jnp.where((row & (h - 1)) == 0, 1.0, pltpu.roll(pre[h], 1, 0)) if h > 1 else None)
             for h in pre}

  def head_blockdiag(x):
    return jnp.concatenate([x] * HG_HEADS, axis=0) * hmask

  nt = (((1,), (1,)), ((), ()))
  kb = k.astype(BF16)
  p = masks_ref[0] * lax.dot_general(q.astype(BF16), head_blockdiag(kb), nt, preferred_element_type=F32)
  for li, h in enumerate(_HG_LEVELS):
    qh = (q * q_dec[h]).astype(BF16)
    kh = kb if h == 1 else (k * k_dec[h]).astype(BF16)
    p = p + masks_ref[li + 1] * lax.dot_general(qh, head_blockdiag(kh), nt, preferred_element_type=F32)

  vb = v.astype(BF16)
  o = jnp.dot(p.astype(BF16), head_blockdiag(vb), preferred_element_type=F32)
  st = st_ref[...]
  o = o + lax.dot_general((q * q_dec[L]).astype(BF16), st.astype(BF16), nt, preferred_element_type=F32)

  ke = (k * k_dec[L]).astype(BF16)
  end = q_dec[L][0:1, :] if reverse else q_dec[L][L - 1:L, :]
  tn = (((0,), (0,)), ((), ()))
  upd = lax.dot_general(vb, ke, tn, preferred_element_type=F32)
  st_ref[...] = st * end + upd * hmask.astype(F32)
  return o


def _hgrn2_kernel(pb_ref, lb_ref, ng_ref, mf_ref, mb_ref, hm_ref, ones_ref, o_ref,
                  of_ref, ob_ref, sf_ref, sb_ref):
  S = pb_ref.shape[0]
  W = HG_WIDTH
  L = HG_CHUNK
  n = S // L
  row = lax.broadcasted_iota(jnp.int32, (L, W), 0)
  hmask = hm_ref[...]
  sf_ref[...] = jnp.zeros_like(sf_ref)
  sb_ref[...] = jnp.zeros_like(sb_ref)

  def load(c, fcol, lb):
    sl = pl.ds(pl.multiple_of(c * L, L), L)
    q = _silu(pb_ref[sl, 0:W]) * (HG_DK ** -0.5)
    f = lb + (1.0 - lb) * _sigmoid(pb_ref[sl, fcol:fcol + W])
    v = pb_ref[sl, 3 * W:4 * W]
    return sl, q, f, v

  def body(c, _):
    sl, q, f, v = load(c, W, lb_ref[0:1, :])
    of_ref[sl, :] = _hg_chunk(q, f, v, sf_ref, mf_ref, hmask, row, False)
    sl, q, f, v = load(n - 1 - c, 2 * W, lb_ref[1:2, :])
    ob_ref[sl, :] = _hg_chunk(q, f, v, sb_ref, mb_ref, hmask, row, True)
    return 0

  lax.fori_loop(0, n, body, 0)

  o = of_ref[...] + ob_ref[...]
  ms = jnp.dot((o * o).astype(BF16), ones_ref[...], preferred_element_type=F32) * (1.0 / HG_DK)
  y = o * lax.rsqrt(ms + EPS) * ng_ref[...]
  o_ref[...] = (y * _silu(pb_ref[:, 4 * W:5 * W])).astype(o_ref.dtype)


def _hg_constants():
  L = HG_CHUNK
  t = jnp.arange(L)[:, None]
  s = jnp.arange(L)[None, :]
  fwd, bwd = [t == s], [t == s]
  for h in _HG_LEVELS:
    same = (t // (2 * h)) == (s // (2 * h))
    fwd.append(same & ((t & h) != 0) & ((s & h) == 0))
    bwd.append(same & ((t & h) == 0) & ((s & h) != 0))
  tile = lambda m: jnp.tile(jnp.stack(m).astype(F32), (1, 1, HG_HEADS))
  hid = jnp.arange(HG_WIDTH) // HG_DK
  hmask = (hid[:, None] == hid[None, :])
  return tile(fwd), tile(bwd), hmask.astype(BF16), hmask.astype(BF16)


def _hgrn2(pb, lb, norm_g, B, S):
  T = B * S
  W = HG_WIDTH
  mf, mb, hmask, ones = _hg_constants()
  const = lambda shape: pl.BlockSpec(shape, lambda b: (0,) * len(shape))
  return pl.pallas_call(
      _hgrn2_kernel,
      grid=(B,),
      in_specs=[
          pl.BlockSpec((S, PB_W), lambda b: (b, 0)),
          const((2, W)), const((1, W)), const(mf.shape), const(mb.shape), const((W, W)), const((W, W)),
      ],
      out_specs=pl.BlockSpec((S, W), lambda b: (b, 0)),
      out_shape=jax.ShapeDtypeStruct((T, W), BF16),
      scratch_shapes=[pltpu.VMEM((S, W), F32), pltpu.VMEM((S, W), F32),
                      pltpu.VMEM((W, W), F32), pltpu.VMEM((W, W), F32)],
      compiler_params=_cparams(("parallel",)),
      name="hgrn2",
  )(pb, lb, norm_g, mf, mb, hmask, ones)


def _rope_tables_kernel(pos_ref, invf_ref, cos_ref, sin_ref):
  lane = lax.broadcasted_iota(jnp.int32, cos_ref.shape, 1)
  ang = pos_ref[...].astype(F32) * invf_ref[...]
  c = jnp.cos(ang)
  s = jnp.sin(ang)
  half = MLA_ROPE // 2
  cos_ref[...] = jnp.where(lane < MLA_NOPE, 1.0, jnp.where(lane < MLA_QK, c, 0.0))
  sin_ref[...] = jnp.where(lane < MLA_NOPE, 0.0,
                           jnp.where(lane < MLA_NOPE + half, -s, jnp.where(lane < MLA_QK, s, 0.0)))


def _rope_tables(pos2, tm):
  T = pos2.shape[0]
  half = MLA_ROPE // 2
  inv_freq = ROPE_THETA ** (-jnp.arange(0, half, dtype=F32) * (2.0 / MLA_ROPE))
  invf = jnp.zeros((1, HEAD_SLOT), F32)
  invf = invf.at[0, MLA_NOPE:MLA_NOPE + half].set(inv_freq).at[0, MLA_NOPE + half:MLA_QK].set(inv_freq)
  return pl.pallas_call(
      _rope_tables_kernel,
      grid=(T // tm,),
      in_specs=[pl.BlockSpec((tm, 1), lambda i: (i, 0)), pl.BlockSpec((1, HEAD_SLOT), lambda i: (0, 0))],
      out_specs=[pl.BlockSpec((tm, HEAD_SLOT), lambda i: (i, 0))] * 2,
      out_shape=[jax.ShapeDtypeStruct((T, HEAD_SLOT), F32)] * 2,
      compiler_params=_cparams(("parallel",)),
      name="rope_tables",
  )(pos2, invf)


def _rope(t, cos_t, sin_t, lane):
  half = MLA_ROPE // 2
  partner = jnp.where(lane < MLA_NOPE + half, pltpu.roll(t, HEAD_SLOT - half, 1), pltpu.roll(t, half, 1))
  return t * cos_t + partner * sin_t


def _mla_prep_kernel(pc_ref, cos_ref, sin_ref, qn_ref, kvn_ref, wq_ref, wk_ref, wv_ref, gq_ref, gk_ref,
                     q_ref, k_ref, v_ref):
  ts = pc_ref.shape[0]
  cos_t = cos_ref[...]
  sin_t = sin_ref[...]
  lane = lax.broadcasted_iota(jnp.int32, (ts, HEAD_SLOT), 1)
  cq = _rms(pc_ref[:, 0:MLA_Q_RANK], qn_ref[...]).astype(BF16)
  ckv = _rms(pc_ref[:, MLA_Q_RANK:MLA_Q_RANK + MLA_KV_RANK], kvn_ref[...]).astype(BF16)
  kr_slot = pc_ref[:, MLA_Q_RANK + MLA_KV_RANK:]
  q_all = jnp.dot(cq, wq_ref[...], preferred_element_type=F32)
  k_all = jnp.dot(ckv, wk_ref[...], preferred_element_type=F32)
  v_ref[...] = jnp.dot(ckv, wv_ref[...], preferred_element_type=F32).astype(v_ref.dtype)
  scale = MLA_QK ** -0.5
  for h in range(MLA_HEADS):
    sl = slice(h * HEAD_SLOT, (h + 1) * HEAD_SLOT)
    qh = _rope(_rms(q_all[:, sl], gq_ref[...], MLA_QK), cos_t, sin_t, lane)
    q_ref[h] = (qh * scale).astype(q_ref.dtype)
    kh = _rope(_rms(k_all[:, sl] + kr_slot, gk_ref[...], MLA_QK), cos_t, sin_t, lane)
    k_ref[h] = kh.astype(k_ref.dtype)


def _mla_prep(pc, cos_t, sin_t, q_norm, kv_norm, wq, wk, wv, gq, gk, B, S, ts):
  H = MLA_HEADS
  nb = S // ts
  const = lambda shape: pl.BlockSpec(shape, lambda b, s: (0,) * len(shape))
  tok = lambda w: pl.BlockSpec((ts, w), lambda b, s: (b * nb + s, 0))
  return pl.pallas_call(
      _mla_prep_kernel,
      grid=(B, nb),
      in_specs=[
          tok(PC_W), tok(HEAD_SLOT), tok(HEAD_SLOT),
          const((1, MLA_Q_RANK)), const((1, MLA_KV_RANK)),
          const(wq.shape), const(wk.shape), const(wv.shape),
          const((1, HEAD_SLOT)), const((1, HEAD_SLOT)),
      ],
      out_specs=[
          pl.BlockSpec((None, H, ts, HEAD_SLOT), lambda b, s: (b, 0, s, 0)),
          pl.BlockSpec((None, H, ts, HEAD_SLOT), lambda b, s: (b, 0, s, 0)),
          pl.BlockSpec((ts, MLA_WIDTH), lambda b, s: (b * nb + s, 0)),
      ],
      out_shape=[
          jax.ShapeDtypeStruct((B, H, S, HEAD_SLOT), BF16),
          jax.ShapeDtypeStruct((B, H, S, HEAD_SLOT), BF16),
          jax.ShapeDtypeStruct((B * S, MLA_WIDTH), BF16),
      ],
      compiler_params=_cparams(("parallel", "parallel")),
      name="mla_prep",
  )(pc, cos_t, sin_t, q_norm, kv_norm, wq, wk, wv, gq, gk)


def _attention_kernel(q_ref, k_ref, v_ref, o_ref):
  tq = q_ref.shape[1]
  v = v_ref[...]
  nt = (((1,), (1,)), ((), ()))
  outs = []
  for h in range(2):
    s = lax.dot_general(q_ref[h], k_ref[h], nt, preferred_element_type=F32)
    m = jnp.max(s, axis=-1, keepdims=True)
    p = jnp.exp(s - m)
    l = jnp.sum(p, axis=-1, keepdims=True)
    o = jnp.dot(p.astype(BF16), v, preferred_element_type=F32)
    outs.append(o * (1.0 / l))
  lane = lax.broadcasted_iota(jnp.int32, (tq, 2 * MLA_V), 1)
  o_ref[...] = jnp.where(lane < MLA_V, outs[0], outs[1]).astype(o_ref.dtype)


def _attention(q, k, v, B, S, tq):
  H = MLA_HEADS
  nq = S // tq
  return pl.pallas_call(
      _attention_kernel,
      grid=(B, H // 2, nq),
      in_specs=[
          pl.BlockSpec((None, 2, tq, HEAD_SLOT), lambda b, hp, i: (b, hp, i, 0)),
          pl.BlockSpec((None, 2, S, HEAD_SLOT), lambda b, hp, i: (b, hp, 0, 0)),
          pl.BlockSpec((S, 2 * MLA_V), lambda b, hp, i: (b, hp)),
      ],
      out_specs=pl.BlockSpec((tq, 2 * MLA_V), lambda b, hp, i: (b * nq + i, hp)),
      out_shape=jax.ShapeDtypeStruct((B * S, MLA_WIDTH), F32),
      compiler_params=_cparams(("parallel", "parallel", "parallel")),
      name="mla_attention",
  )(q, k, v)


def _outproj_kernel(x_ref, ya_ref, yb_ref, yc_ref, gc_ref, w_ref, o_ref):
  yc = _rms(yc_ref[...], gc_ref[...]).astype(BF16)
  a0, a1 = LRU_WIDTH, LRU_WIDTH + HG_WIDTH
  acc = jnp.dot(ya_ref[...], w_ref[0:a0, :], preferred_element_type=F32)
  acc = acc + jnp.dot(yb_ref[...], w_ref[a0:a1, :], preferred_element_type=F32)
  acc = acc + jnp.dot(yc, w_ref[a1:, :], preferred_element_type=F32)
  o_ref[...] = x_ref[...] + acc


def _outproj(x2, ya, yb, yc, gc, w, tm):
  T = x2.shape[0]
  tok = lambda wd: pl.BlockSpec((tm, wd), lambda i: (i, 0))
  return pl.pallas_call(
      _outproj_kernel,
      grid=(T // tm,),
      in_specs=[tok(D_MODEL), tok(LRU_WIDTH), tok(HG_WIDTH), tok(MLA_WIDTH),
                pl.BlockSpec((1, MLA_WIDTH), lambda i: (0, 0)),
                pl.BlockSpec((D_MODEL, D_MODEL), lambda i: (0, 0))],
      out_specs=tok(D_MODEL),
      out_shape=jax.ShapeDtypeStruct((T, D_MODEL), F32),
      compiler_params=_cparams(("parallel",)),
      name="outproj",
  )(x2, ya, yb, yc, gc, w)


def _ffn_kernel(x_ref, g_ref, wg_ref, wu_ref, wd_ref, o_ref, h_ref, acc_ref):
  f = pl.program_id(1)

  @pl.when(f == 0)
  def _():
    h_ref[...] = _rms(x_ref[...], g_ref[...]).astype(BF16)
    acc_ref[...] = jnp.zeros_like(acc_ref)

  h = h_ref[...]
  g = jnp.dot(h, wg_ref[...], preferred_element_type=F32)
  u = jnp.dot(h, wu_ref[...], preferred_element_type=F32)
  a = (_silu(g) * u).astype(BF16)
  acc_ref[...] += jnp.dot(a, wd_ref[...], preferred_element_type=F32)

  @pl.when(f == pl.num_programs(1) - 1)
  def _():
    o_ref[...] = x_ref[...] + acc_ref[...]


def _ffn(x2, gain, wg, wu, wd, tm, tf):
  T = x2.shape[0]
  F = wg.shape[1]
  return pl.pallas_call(
      _ffn_kernel,
      grid=(T // tm, F // tf),
      in_specs=[
          pl.BlockSpec((tm, D_MODEL), lambda i, f: (i, 0)),
          pl.BlockSpec((1, D_MODEL), lambda i, f: (0, 0)),
          pl.BlockSpec((D_MODEL, tf), lambda i, f: (0, f)),
          pl.BlockSpec((D_MODEL, tf), lambda i, f: (0, f)),
          pl.BlockSpec((tf, D_MODEL), lambda i, f: (f, 0)),
      ],
      out_specs=pl.BlockSpec((tm, D_MODEL), lambda i, f: (i, 0)),
      out_shape=jax.ShapeDtypeStruct((T, D_MODEL), F32),
      scratch_shapes=[pltpu.VMEM((tm, D_MODEL), BF16), pltpu.VMEM((tm, D_MODEL), F32)],
      compiler_params=_cparams(("parallel", "arbitrary")),
      name="ffn_dense",
  )(x2, gain, wg, wu, wd)


def _router_kernel(x_ref, g_ref, wr_ref, c_ref):
  h = _rms(x_ref[...], g_ref[...])
  logits = jnp.dot(h, wr_ref[...], preferred_element_type=F32, precision=lax.Precision.HIGHEST)
  lane = lax.broadcasted_iota(jnp.int32, logits.shape, 1)
  neg = jnp.float32(-jnp.inf)
  lg = jnp.where(lane < N_EXPERTS, logits, neg)
  m1 = jnp.max(lg, axis=-1, keepdims=True)
  i1 = jnp.min(jnp.where(lg == m1, lane, LANE), axis=-1, keepdims=True)
  lg2 = jnp.where(lane == i1, neg, lg)
  m2 = jnp.max(lg2, axis=-1, keepdims=True)
  i2 = jnp.min(jnp.where(lg2 == m2, lane, LANE), axis=-1, keepdims=True)
  e2 = jnp.exp(m2 - m1)
  w1 = 1.0 / (1.0 + e2)
  w2 = e2 / (1.0 + e2)
  c_ref[...] = jnp.where(lane == i1, w1, 0.0) + jnp.where(lane == i2, w2, 0.0)


def _router(x2, gain, wr, tm):
  T = x2.shape[0]
  return pl.pallas_call(
      _router_kernel,
      grid=(T // tm,),
      in_specs=[pl.BlockSpec((tm, D_MODEL), lambda i: (i, 0)),
                pl.BlockSpec((1, D_MODEL), lambda i: (0, 0)),
                pl.BlockSpec((D_MODEL, LANE), lambda i: (0, 0))],
      out_specs=pl.BlockSpec((tm, LANE), lambda i: (i, 0)),
      out_shape=jax.ShapeDtypeStruct((T, LANE), F32),
      compiler_params=_cparams(("parallel",)),
      name="moe_router",
  )(x2, gain, wr)


def _moe_kernel(x_ref, g_ref, c_ref, w1_ref, w3_ref, w2_ref, o_ref, h_ref, acc_ref):
  e = pl.program_id(1)
  f = pl.program_id(2)

  @pl.when((e == 0) & (f == 0))
  def _():
    h_ref[...] = _rms(x_ref[...], g_ref[...]).astype(BF16)
    acc_ref[...] = jnp.zeros_like(acc_ref)

  h = h_ref[...]
  lane = lax.broadcasted_iota(jnp.int32, c_ref.shape, 1)
  comb = jnp.sum(jnp.where(lane == e, c_ref[...], 0.0), axis=-1, keepdims=True)
  g = jnp.dot(h, w1_ref[...], preferred_element_type=F32)
  u = jnp.dot(h, w3_ref[...], preferred_element_type=F32)
  a = (_silu(g) * u * comb).astype(BF16)
  acc_ref[...] += jnp.dot(a, w2_ref[...], preferred_element_type=F32)

  @pl.when((e == pl.num_programs(1) - 1) & (f == pl.num_programs(2) - 1))
  def _():
    o_ref[...] = x_ref[...] + acc_ref[...]


def _moe(x2, gain, comb, w1, w3, w2, tm, tf):
  T = x2.shape[0]
  E, _, F = w1.shape
  return pl.pallas_call(
      _moe_kernel,
      grid=(T // tm, E, F // tf),
      in_specs=[
          pl.BlockSpec((tm, D_MODEL), lambda i, e, f: (i, 0)),
          pl.BlockSpec((1, D_MODEL), lambda i, e, f: (0, 0)),
          pl.BlockSpec((tm, LANE), lambda i, e, f: (i, 0)),
          pl.BlockSpec((None, D_MODEL, tf), lambda i, e, f: (e, 0, f)),
          pl.BlockSpec((None, D_MODEL, tf), lambda i, e, f: (e, 0, f)),
          pl.BlockSpec((None, tf, D_MODEL), lambda i, e, f: (e, f, 0)),
      ],
      out_specs=pl.BlockSpec((tm, D_MODEL), lambda i, e, f: (i, 0)),
      out_shape=jax.ShapeDtypeStruct((T, D_MODEL), F32),
      scratch_shapes=[pltpu.VMEM((tm, D_MODEL), BF16), pltpu.VMEM((tm, D_MODEL), F32)],
      compiler_params=_cparams(("parallel", "arbitrary", "arbitrary")),
      name="moe_dense",
  )(x2, gain, comb, w1, w3, w2)


def _block_diag(w):
  n, d, e = w.shape
  eye = jnp.eye(n, dtype=w.dtype)
  return (eye[:, None, :, None] * w[:, :, None, :]).reshape(n * d, n * e)


def _head_slots(w, width, offset=0):
  r = w.shape[0]
  w3 = w.reshape(r, MLA_HEADS, width)
  out = jnp.zeros((r, MLA_HEADS, HEAD_SLOT), w.dtype).at[:, :, offset:offset + width].set(w3)
  return out.reshape(r, MLA_HEADS * HEAD_SLOT)


def _slot_gain(g):
  return jnp.zeros((1, HEAD_SLOT), F32).at[0, :MLA_QK].set(g.astype(F32))


def kernel(x, positions, norm_mix, w_in, conv_w, conv_b, lru_wa, lru_ba, lru_wx, lru_bx, lru_lam,
           out_g_a, hg_lb_logits, hg_norm_g, mla_q_norm, mla_w_uq, mla_kv_norm, mla_w_ukv,
           qk_norm_q, qk_norm_k, out_g_c, w_out, norm_ffn, ffn_w_gate_up, ffn_w_down,
           moe_router, moe_w1, moe_w3, moe_w2):
  B, S, D = x.shape
  T = B * S
  depth = w_in.shape[0]
  tm = 512
  row = lambda v: v.astype(F32).reshape(1, -1)

  p = jax.nn.softmax(hg_lb_logits.astype(F32), axis=0)
  lower_bounds = jnp.cumsum(p, axis=0) - p[0:1]

  x2 = x.reshape(T, D)
  cos_t, sin_t = _rope_tables(positions.reshape(T, 1), tm)

  for l in range(depth):
    n_main = w_in.shape[2] - MLA_ROPE
    w_in_l = jnp.concatenate(
        [w_in[l][:, :n_main], jnp.zeros((D, MLA_NOPE), F32), w_in[l][:, n_main:],
         jnp.zeros((D, HEAD_SLOT - MLA_QK), F32)], axis=1).astype(BF16)
    wg_lru = jnp.concatenate(
        [_block_diag(lru_wa[l, 0]), _block_diag(lru_wx[l, 0]),
         _block_diag(lru_wa[l, 1]), _block_diag(lru_wx[l, 1])], axis=1).astype(BF16)
    bg_lru = jnp.concatenate([lru_ba[l, 0], lru_bx[l, 0], lru_ba[l, 1], lru_bx[l, 1]]).reshape(1, -1)
    wq = _head_slots(mla_w_uq[l], MLA_QK).astype(BF16)
    wkv = mla_w_ukv[l].reshape(MLA_KV_RANK, MLA_HEADS, MLA_NOPE + MLA_V)
    wk = _head_slots(wkv[:, :, :MLA_NOPE].reshape(MLA_KV_RANK, -1), MLA_NOPE).astype(BF16)
    wv = wkv[:, :, MLA_NOPE:].reshape(MLA_KV_RANK, MLA_WIDTH).astype(BF16)

    pa, pb, pc = _norm_inproj(x2, row(norm_mix[l]), w_in_l, tm)
    ya = _rglru(pa, conv_w[l], row(conv_b[l]), wg_lru, bg_lru, lru_lam[l], row(out_g_a[l]), B, S)
    yb = _hgrn2(pb, lower_bounds[l], row(jnp.tile(hg_norm_g[l], HG_HEADS)), B, S)
    q, k, v = _mla_prep(pc, cos_t, sin_t, row(mla_q_norm[l]), row(mla_kv_norm[l]), wq, wk, wv,
                        _slot_gain(qk_norm_q[l]), _slot_gain(qk_norm_k[l]), B, S, min(tm, S))
    yc = _attention(q, k, v, B, S, min(512, S))
    x2 = _outproj(x2, ya, yb, yc, row(out_g_c[l]), w_out[l].astype(BF16), tm)

    if l % 2 == 0:
      wgu = ffn_w_gate_up[l // 2]
      pad = D_FF_PAD - D_FF
      wg = jnp.pad(wgu[:, :D_FF], ((0, 0), (0, pad))).astype(BF16)
      wu = jnp.pad(wgu[:, D_FF:], ((0, 0), (0, pad))).astype(BF16)
      wd = jnp.pad(ffn_w_down[l // 2], ((0, pad), (0, 0))).astype(BF16)
      x2 = _ffn(x2, row(norm_ffn[l]), wg, wu, wd, tm, D_FF_PAD // 2)
    else:
      wr = jnp.pad(moe_router[l // 2], ((0, 0), (0, LANE - N_EXPERTS)))
      comb = _router(x2, row(norm_ffn[l]), wr, tm)
      x2 = _moe(x2, row(norm_ffn[l]), comb, moe_w1[l // 2].astype(BF16), moe_w3[l // 2].astype(BF16),
                moe_w2[l // 2].astype(BF16), tm, D_EXPERT // 4)
  return x2.reshape(B, S, D)
```

```python
import functools
import math

import jax
import jax.numpy as jnp
from jax import lax
from jax.experimental import pallas as pl
from jax.experimental.pallas import tpu as pltpu

F32 = jnp.float32
BF16 = jnp.bfloat16

D_MODEL = 1024
LRU_WIDTH = 256
LRU_BLOCKS = 4
CONV_WIDTH = 4
LRU_C = 8.0
HG_HEADS = 4
HG_DK = 64
HG_WIDTH = 256
MLA_HEADS = 8
MLA_NOPE = 64
MLA_ROPE = 32
MLA_V = 64
MLA_QK = MLA_NOPE + MLA_ROPE
MLA_Q_RANK = 256
MLA_KV_RANK = 128
MLA_WIDTH = MLA_HEADS * MLA_V
ROPE_THETA = 10000.0
D_FF = 2752
N_EXPERTS = 8
D_EXPERT = 3584
EPS = 1e-6

LANE = 128
SUBLANE = 8
HEAD_SLOT = 128
HG_CHUNK = 64
D_FF_PAD = 2816
VMEM_LIMIT = 56 * 1024 * 1024

PA_W, PB_W, PC_W = 512, 1280, 512


def _cparams(sem):
  return pltpu.CompilerParams(dimension_semantics=sem, vmem_limit_bytes=VMEM_LIMIT)


def _sigmoid(x):
  return 1.0 / (1.0 + jnp.exp(-x))


def _silu(x):
  return x * _sigmoid(x)


def _rms(x, gain, n=None):
  n = x.shape[-1] if n is None else n
  ms = jnp.sum(x * x, axis=-1, keepdims=True) * (1.0 / n)
  return x * lax.rsqrt(ms + EPS) * gain


def _norm_inproj_kernel(x_ref, g_ref, w_ref, oa_ref, ob_ref, oc_ref):
  h = _rms(x_ref[...], g_ref[...]).astype(BF16)
  p = jnp.dot(h, w_ref[...], preferred_element_type=F32)
  oa_ref[...] = p[:, :PA_W]
  ob_ref[...] = p[:, PA_W:PA_W + PB_W]
  oc_ref[...] = p[:, PA_W + PB_W:]


def _norm_inproj(x2, gain, w, tm):
  T = x2.shape[0]
  n = w.shape[1]
  return pl.pallas_call(
      _norm_inproj_kernel,
      grid=(T // tm,),
      in_specs=[
          pl.BlockSpec((tm, D_MODEL), lambda i: (i, 0)),
          pl.BlockSpec((1, D_MODEL), lambda i: (0, 0)),
          pl.BlockSpec((D_MODEL, n), lambda i: (0, 0)),
      ],
      out_specs=[
          pl.BlockSpec((tm, PA_W), lambda i: (i, 0)),
          pl.BlockSpec((tm, PB_W), lambda i: (i, 0)),
          pl.BlockSpec((tm, PC_W), lambda i: (i, 0)),
      ],
      out_shape=[
          jax.ShapeDtypeStruct((T, PA_W), F32),
          jax.ShapeDtypeStruct((T, PB_W), F32),
          jax.ShapeDtypeStruct((T, PC_W), F32),
      ],
      compiler_params=_cparams(("parallel",)),
      name="norm_inproj",
  )(x2, gain, w)


def _rglru_kernel(pa_ref, cw_ref, cb_ref, wg_ref, bg_ref, lam_ref, og_ref, o_ref,
                  af_ref, bf_ref, ab_ref, bb_ref, hf_ref, hb_ref):
  S = pa_ref.shape[0]
  W = LRU_WIDTH
  xa = pa_ref[:, 0:W]
  row = lax.broadcasted_iota(jnp.int32, (S, W), 0)

  xc = cb_ref[...] + cw_ref[2:3, :] * xa
  for k in (0, 1, 3):
    d = k - CONV_WIDTH // 2
    shifted = pltpu.roll(xa, (-d) % S, 0)
    valid = (row >= -d) if d < 0 else (row < S - d)
    xc = xc + cw_ref[k:k + 1, :] * jnp.where(valid, shifted, 0.0)

  z = jnp.dot(xc.astype(BF16), wg_ref[...], preferred_element_type=F32) + bg_ref[...]
  lam = lam_ref[...]
  sp = jnp.maximum(-lam, 0.0) + jnp.log1p(jnp.exp(-jnp.abs(lam)))
  for d, (a_ref, b_ref) in enumerate(((af_ref, bf_ref), (ab_ref, bb_ref))):
    r = _sigmoid(z[:, d * 2 * W:d * 2 * W + W])
    i = _sigmoid(z[:, d * 2 * W + W:(d + 1) * 2 * W])
    log_a = (-LRU_C) * r * sp[d:d + 1, :]
    a = jnp.exp(log_a)
    a_ref[...] = a
    y2 = 2.0 * log_a
    u = a * a
    near = jnp.where(u == 1.0, y2, (u - 1.0) * y2 / jnp.log(u))
    em1 = jnp.where(y2 > -0.5, near, u - 1.0)
    b_ref[...] = jnp.sqrt(-em1) * (i * xc)

  n = S // SUBLANE
  r8 = lax.broadcasted_iota(jnp.int32, (SUBLANE, W), 0)

  def body(c, carry):
    cf, cb = carry
    sl = pl.ds(pl.multiple_of(c * SUBLANE, SUBLANE), SUBLANE)
    a = af_ref[sl, :]
    b = bf_ref[sl, :]
    for s in (1, 2, 4):
      m = r8 >= s
      b = jnp.where(m, a * pltpu.roll(b, s, 0) + b, b)
      a = jnp.where(m, a * pltpu.roll(a, s, 0), a)
    h = b + a * cf
    hf_ref[sl, :] = h
    cf = h[SUBLANE - 1:SUBLANE, :]

    sl = pl.ds(pl.multiple_of((n - 1 - c) * SUBLANE, SUBLANE), SUBLANE)
    a = ab_ref[sl, :]
    b = bb_ref[sl, :]
    for s in (1, 2, 4):
      m = r8 < SUBLANE - s
      b = jnp.where(m, a * pltpu.roll(b, SUBLANE - s, 0) + b, b)
      a = jnp.where(m, a * pltpu.roll(a, SUBLANE - s, 0), a)
    h = b + a * cb
    hb_ref[sl, :] = h
    cb = h[0:1, :]
    return cf, cb

  zero = jnp.zeros((1, W), F32)
  lax.fori_loop(0, n, body, (zero, zero), unroll=4)

  g = pa_ref[:, W:2 * W]
  gelu = 0.5 * g * (1.0 + jnp.tanh(math.sqrt(2.0 / math.pi) * (g + 0.044715 * (g * g * g))))
  y = (hf_ref[...] + hb_ref[...]) * gelu
  o_ref[...] = _rms(y, og_ref[...]).astype(o_ref.dtype)


def _rglru(pa, conv_w, conv_b, wg, bg, lam, out_g, B, S):
  T = B * S
  W = LRU_WIDTH
  const = lambda shape: pl.BlockSpec(shape, lambda b: (0,) * len(shape))
  return pl.pallas_call(
      _rglru_kernel,
      grid=(B,),
      in_specs=[
          pl.BlockSpec((S, PA_W), lambda b: (b, 0)),
          const((CONV_WIDTH, W)), const((1, W)), const((W, 4 * W)), const((1, 4 * W)),
          const((2, W)), const((1, W)),
      ],
      out_specs=pl.BlockSpec((S, W), lambda b: (b, 0)),
      out_shape=jax.ShapeDtypeStruct((T, W), BF16),
      scratch_shapes=[pltpu.VMEM((S, W), F32) for _ in range(6)],
      compiler_params=_cparams(("parallel",)),
      name="rglru",
  )(pa, conv_w, conv_b, wg, bg, lam, out_g)


_HG_LEVELS = (1, 2, 4, 8, 16, 32)


def _bcast_row_in_blocks(x, block, row):
  L, C = x.shape
  x3 = x.reshape(L // block, block, C)
  return jnp.broadcast_to(x3[:, row:row + 1, :], x3.shape).reshape(L, C)


def _inc_prefix_products(f, row):
  L = f.shape[0]
  out = {1: f}
  p2 = f * jnp.where((row & 1) == 1, pltpu.roll(f, 1, 0), 1.0)
  out[2] = p2
  r4 = row & 3
  p4 = p2 * jnp.where(r4 == 2, pltpu.roll(p2, 1, 0), jnp.where(r4 == 3, pltpu.roll(p2, 2, 0), 1.0))
  out[4] = p4
  p, h = p4, 4
  while h < L:
    p = p * jnp.where((row & h) != 0, _bcast_row_in_blocks(p, 2 * h, h - 1), 1.0)
    h *= 2
    out[h] = p
  return out


def _inc_suffix_products(f, row):
  L = f.shape[0]
  up = lambda x, s: pltpu.roll(x, L - s, 0)
  out = {1: f}
  p2 = f * jnp.where((row & 1) == 0, up(f, 1), 1.0)
  out[2] = p2
  r4 = row & 3
  p4 = p2 * jnp.where(r4 == 1, up(p2, 1), jnp.where(r4 == 0, up(p2, 2), 1.0))
  out[4] = p4
  p, h = p4, 4
  while h < L:
    p = p * jnp.where((row & h) == 0, _bcast_row_in_blocks(p, 2 * h, h), 1.0)
    h *= 2
    out[h] = p
  return out


def _hg_chunk(q, f, v, st_ref, masks_ref, hmask, row, reverse):
  L = q.shape[0]
  k = 1.0 - f
  pre = _inc_prefix_products(f, row)
  suf = _inc_suffix_products(f, row)
  if not reverse:
    q_dec = pre
    k_dec = {h: (jnp.where((row & (h - 1)) == h - 1, 1.0, pltpu.roll(suf[h], L - 1, 0)) if h > 1 else None)
             for h in suf}
  else:
    q_dec = suf
    k_dec = {h: (jnp.where((row & (h - 1)) == 0, 1.0, pltpu.roll(pre[h], 1, 0)) if h > 1 else None)
             for h in pre}

  def head_blockdiag(x):
    return jnp.concatenate([x] * HG_HEADS, axis=0) * hmask

  nt = (((1,), (1,)), ((), ()))
  kb = k.astype(BF16)
  p = masks_ref[0] * lax.dot_general(q.astype(BF16), head_blockdiag(kb), nt, preferred_element_type=F32)
  for li, h in enumerate(_HG_LEVELS):
    qh = (q * q_dec[h]).astype(BF16)
    kh = kb if h == 1 else (k * k_dec[h]).astype(BF16)
    p = p + masks_ref[li + 1] * lax.dot_general(qh, head_blockdiag(kh), nt, preferred_element_type=F32)

  vb = v.astype(BF16)
  o = jnp.dot(p.astype(BF16), head_blockdiag(vb), preferred_element_type=F32)
  st = st_ref[...]
  o = o + lax.dot_general((q * q_dec[L]).astype(BF16), st.astype(BF16), nt, preferred_element_type=F32)

  ke = (k * k_dec[L]).astype(BF16)
  end = q_dec[L][0:1, :] if reverse else q_dec[L][L - 1:L, :]
  tn = (((0,), (0,)), ((), ()))
  upd = lax.dot_general(vb, ke, tn, preferred_element_type=F32)
  st_ref[...] = st * end + upd * hmask.astype(F32)
  return o


def _hgrn2_kernel(pb_ref, lb_ref, ng_ref, mf_ref, mb_ref, hm_ref, ones_ref, o_ref,
                  of_ref, ob_ref, sf_ref, sb_ref):
  S = pb_ref.shape[0]
  W = HG_WIDTH
  L = HG_CHUNK
  n = S // L
  row = lax.broadcasted_iota(jnp.int32, (L, W), 0)
  hmask = hm_ref[...]
  sf_ref[...] = jnp.zeros_like(sf_ref)
  sb_ref[...] = jnp.zeros_like(sb_ref)

  def load(c, fcol, lb):
    sl = pl.ds(pl.multiple_of(c * L, L), L)
    q = _silu(pb_ref[sl, 0:W]) * (HG_DK ** -0.5)
    f = lb + (1.0 - lb) * _sigmoid(pb_ref[sl, fcol:fcol + W])
    v = pb_ref[sl, 3 * W:4 * W]
    return sl, q, f, v

  def body(c, _):
    sl, q, f, v = load(c, W, lb_ref[0:1, :])
    of_ref[sl, :] = _hg_chunk(q, f, v, sf_ref, mf_ref, hmask, row, False)
    sl, q, f, v = load(n - 1 - c, 2 * W, lb_ref[1:2, :])
    ob_ref[sl, :] = _hg_chunk(q, f, v, sb_ref, mb_ref, hmask, row, True)
    return 0

  lax.fori_loop(0, n, body, 0)

  o = of_ref[...] + ob_ref[...]
  ms = jnp.dot((o * o).astype(BF16), ones_ref[...], preferred_element_type=F32) * (1.0 / HG_DK)
  y = o * lax.rsqrt(ms + EPS) * ng_ref[...]
  o_ref[...] = (y * _silu(pb_ref[:, 4 * W:5 * W])).astype(o_ref.dtype)


def _hg_constants():
  L = HG_CHUNK
  t = jnp.arange(L)[:, None]
  s = jnp.arange(L)[None, :]
  fwd, bwd = [t == s], [t == s]
  for h in _HG_LEVELS:
    same = (t // (2 * h)) == (s // (2 * h))
    fwd.append(same & ((t & h) != 0) & ((s & h) == 0))
    bwd.append(same & ((t & h) == 0) & ((s & h) != 0))
  tile = lambda m: jnp.tile(jnp.stack(m).astype(F32), (1, 1, HG_HEADS))
  hid = jnp.arange(HG_WIDTH) // HG_DK
  hmask = (hid[:, None] == hid[None, :])
  return tile(fwd), tile(bwd), hmask.astype(BF16), hmask.astype(BF16)


def _hgrn2(pb, lb, norm_g, B, S):
  T = B * S
  W = HG_WIDTH
  mf, mb, hmask, ones = _hg_constants()
  const = lambda shape: pl.BlockSpec(shape, lambda b: (0,) * len(shape))
  return pl.pallas_call(
      _hgrn2_kernel,
      grid=(B,),
      in_specs=[
          pl.BlockSpec((S, PB_W), lambda b: (b, 0)),
          const((2, W)), const((1, W)), const(mf.shape), const(mb.shape), const((W, W)), const((W, W)),
      ],
      out_specs=pl.BlockSpec((S, W), lambda b: (b, 0)),
      out_shape=jax.ShapeDtypeStruct((T, W), BF16),
      scratch_shapes=[pltpu.VMEM((S, W), F32), pltpu.VMEM((S, W), F32),
                      pltpu.VMEM((W, W), F32), pltpu.VMEM((W, W), F32)],
      compiler_params=_cparams(("parallel",)),
      name="hgrn2",
  )(pb, lb, norm_g, mf, mb, hmask, ones)


def _rope_tables_kernel(pos_ref, invf_ref, cos_ref, sin_ref):
  lane = lax.broadcasted_iota(jnp.int32, cos_ref.shape, 1)
  ang = pos_ref[...].astype(F32) * invf_ref[...]
  c = jnp.cos(ang)
  s = jnp.sin(ang)
  half = MLA_ROPE // 2
  cos_ref[...] = jnp.where(lane < MLA_NOPE, 1.0, jnp.where(lane < MLA_QK, c, 0.0))
  sin_ref[...] = jnp.where(lane < MLA_NOPE, 0.0,
                           jnp.where(lane < MLA_NOPE + half, -s, jnp.where(lane < MLA_QK, s, 0.0)))


def _rope_tables(pos2, tm):
  T = pos2.shape[0]
  half = MLA_ROPE // 2
  inv_freq = ROPE_THETA ** (-jnp.arange(0, half, dtype=F32) * (2.0 / MLA_ROPE))
  invf = jnp.zeros((1, HEAD_SLOT), F32)
  invf = invf.at[0, MLA_NOPE:MLA_NOPE + half].set(inv_freq).at[0, MLA_NOPE + half:MLA_QK].set(inv_freq)
  return pl.pallas_call(
      _rope_tables_kernel,
      grid=(T // tm,),
      in_specs=[pl.BlockSpec((tm, 1), lambda i: (i, 0)), pl.BlockSpec((1, HEAD_SLOT), lambda i: (0, 0))],
      out_specs=[pl.BlockSpec((tm, HEAD_SLOT), lambda i: (i, 0))] * 2,
      out_shape=[jax.ShapeDtypeStruct((T, HEAD_SLOT), F32)] * 2,
      compiler_params=_cparams(("parallel",)),
      name="rope_tables",
  )(pos2, invf)


def _rope(t, cos_t, sin_t, lane):
  half = MLA_ROPE // 2
  partner = jnp.where(lane < MLA_NOPE + half, pltpu.roll(t, HEAD_SLOT - half, 1), pltpu.roll(t, half, 1))
  return t * cos_t + partner * sin_t


def _mla_prep_kernel(pc_ref, cos_ref, sin_ref, qn_ref, kvn_ref, wq_ref, wk_ref, wv_ref, gq_ref, gk_ref,
                     q_ref, k_ref, v_ref):
  ts = pc_ref.shape[0]
  cos_t = cos_ref[...]
  sin_t = sin_ref[...]
  lane = lax.broadcasted_iota(jnp.int32, (ts, HEAD_SLOT), 1)
  cq = _rms(pc_ref[:, 0:MLA_Q_RANK], qn_ref[...]).astype(BF16)
  ckv = _rms(pc_ref[:, MLA_Q_RANK:MLA_Q_RANK + MLA_KV_RANK], kvn_ref[...]).astype(BF16)
  kr_slot = pc_ref[:, MLA_Q_RANK + MLA_KV_RANK:]
  q_all = jnp.dot(cq, wq_ref[...], preferred_element_type=F32)
  k_all = jnp.dot(ckv, wk_ref[...], preferred_element_type=F32)
  v_ref[...] = jnp.dot(ckv, wv_ref[...], preferred_element_type=F32).astype(v_ref.dtype)
  scale = MLA_QK ** -0.5
  for h in range(MLA_HEADS):
    sl = slice(h * HEAD_SLOT, (h + 1) * HEAD_SLOT)
    qh = _rope(_rms(q_all[:, sl], gq_ref[...], MLA_QK), cos_t, sin_t, lane)
    q_ref[h] = (qh * scale).astype(q_ref.dtype)
    kh = _rope(_rms(k_all[:, sl] + kr_slot, gk_ref[...], MLA_QK), cos_t, sin_t, lane)
    k_ref[h] = kh.astype(k_ref.dtype)


def _mla_prep(pc, cos_t, sin_t, q_norm, kv_norm, wq, wk, wv, gq, gk, B, S, ts):
  H = MLA_HEADS
  nb = S // ts
  const = lambda shape: pl.BlockSpec(shape, lambda b, s: (0,) * len(shape))
  tok = lambda w: pl.BlockSpec((ts, w), lambda b, s: (b * nb + s, 0))
  return pl.pallas_call(
      _mla_prep_kernel,
      grid=(B, nb),
      in_specs=[
          tok(PC_W), tok(HEAD_SLOT), tok(HEAD_SLOT),
          const((1, MLA_Q_RANK)), const((1, MLA_KV_RANK)),
          const(wq.shape), const(wk.shape), const(wv.shape),
          const((1, HEAD_SLOT)), const((1, HEAD_SLOT)),
      ],
      out_specs=[
          pl.BlockSpec((None, H, ts, HEAD_SLOT), lambda b, s: (b, 0, s, 0)),
          pl.BlockSpec((None, H, ts, HEAD_SLOT), lambda b, s: (b, 0, s, 0)),
          pl.BlockSpec((ts, MLA_WIDTH), lambda b, s: (b * nb + s, 0)),
      ],
      out_shape=[
          jax.ShapeDtypeStruct((B, H, S, HEAD_SLOT), BF16),
          jax.ShapeDtypeStruct((B, H, S, HEAD_SLOT), BF16),
          jax.ShapeDtypeStruct((B * S, MLA_WIDTH), BF16),
      ],
      compiler_params=_cparams(("parallel", "parallel")),
      name="mla_prep",
  )(pc, cos_t, sin_t, q_norm, kv_norm, wq, wk, wv, gq, gk)


def _attention_kernel(q_ref, k_ref, v_ref, o_ref):
  tq = q_ref.shape[1]
  v = v_ref[...]
  nt = (((1,), (1,)), ((), ()))
  outs = []
  for h in range(2):
    s = lax.dot_general(q_ref[h], k_ref[h], nt, preferred_element_type=F32)
    m = jnp.max(s, axis=-1, keepdims=True)
    p = jnp.exp(s - m)
    l = jnp.sum(p, axis=-1, keepdims=True)
    o = jnp.dot(p.astype(BF16), v, preferred_element_type=F32)
    outs.append(o * (1.0 / l))
  lane = lax.broadcasted_iota(jnp.int32, (tq, 2 * MLA_V), 1)
  o_ref[...] = jnp.where(lane < MLA_V, outs[0], outs[1]).astype(o_ref.dtype)


def _attention(q, k, v, B, S, tq):
  H = MLA_HEADS
  nq = S // tq
  return pl.pallas_call(
      _attention_kernel,
      grid=(B, H // 2, nq),
      in_specs=[
          pl.BlockSpec((None, 2, tq, HEAD_SLOT), lambda b, hp, i: (b, hp, i, 0)),
          pl.BlockSpec((None, 2, S, HEAD_SLOT), lambda b, hp, i: (b, hp, 0, 0)),
          pl.BlockSpec((S, 2 * MLA_V), lambda b, hp, i: (b, hp)),
      ],
      out_specs=pl.BlockSpec((tq, 2 * MLA_V), lambda b, hp, i: (b * nq + i, hp)),
      out_shape=jax.ShapeDtypeStruct((B * S, MLA_WIDTH), F32),
      compiler_params=_cparams(("parallel", "parallel", "parallel")),
      name="mla_attention",
  )(q, k, v)


def _outproj_kernel(x_ref, ya_ref, yb_ref, yc_ref, gc_ref, w_ref, o_ref):
  yc = _rms(yc_ref[...], gc_ref[...]).astype(BF16)
  a0, a1 = LRU_WIDTH, LRU_WIDTH + HG_WIDTH
  acc = jnp.dot(ya_ref[...], w_ref[0:a0, :], preferred_element_type=F32)
  acc = acc + jnp.dot(yb_ref[...], w_ref[a0:a1, :], preferred_element_type=F32)
  acc = acc + jnp.dot(yc, w_ref[a1:, :], preferred_element_type=F32)
  o_ref[...] = x_ref[...] + acc


def _outproj(x2, ya, yb, yc, gc, w, tm):
  T = x2.shape[0]
  tok = lambda wd: pl.BlockSpec((tm, wd), lambda i: (i, 0))
  return pl.pallas_call(
      _outproj_kernel,
      grid=(T // tm,),
      in_specs=[tok(D_MODEL), tok(LRU_WIDTH), tok(HG_WIDTH), tok(MLA_WIDTH),
                pl.BlockSpec((1, MLA_WIDTH), lambda i: (0, 0)),
                pl.BlockSpec((D_MODEL, D_MODEL), lambda i: (0, 0))],
      out_specs=tok(D_MODEL),
      out_shape=jax.ShapeDtypeStruct((T, D_MODEL), F32),
      compiler_params=_cparams(("parallel",)),
      name="outproj",
  )(x2, ya, yb, yc, gc, w)


def _ffn_kernel(x_ref, g_ref, wg_ref, wu_ref, wd_ref, o_ref, h_ref, acc_ref):
  f = pl.program_id(1)

  @pl.when(f == 0)
  def _():
    h_ref[...] = _rms(x_ref[...], g_ref[...]).astype(BF16)
    acc_ref[...] = jnp.zeros_like(acc_ref)

  h = h_ref[...]
  g = jnp.dot(h, wg_ref[...], preferred_element_type=F32)
  u = jnp.dot(h, wu_ref[...], preferred_element_type=F32)
  a = (_silu(g) * u).astype(BF16)
  acc_ref[...] += jnp.dot(a, wd_ref[...], preferred_element_type=F32)

  @pl.when(f == pl.num_programs(1) - 1)
  def _():
    o_ref[...] = x_ref[...] + acc_ref[...]


def _ffn(x2, gain, wg, wu, wd, tm, tf):
  T = x2.shape[0]
  F = wg.shape[1]
  return pl.pallas_call(
      _ffn_kernel,
      grid=(T // tm, F // tf),
      in_specs=[
          pl.BlockSpec((tm, D_MODEL), lambda i, f: (i, 0)),
          pl.BlockSpec((1, D_MODEL), lambda i, f: (0, 0)),
          pl.BlockSpec((D_MODEL, tf), lambda i, f: (0, f)),
          pl.BlockSpec((D_MODEL, tf), lambda i, f: (0, f)),
          pl.BlockSpec((tf, D_MODEL), lambda i, f: (f, 0)),
      ],
      out_specs=pl.BlockSpec((tm, D_MODEL), lambda i, f: (i, 0)),
      out_shape=jax.ShapeDtypeStruct((T, D_MODEL), F32),
      scratch_shapes=[pltpu.VMEM((tm, D_MODEL), BF16), pltpu.VMEM((tm, D_MODEL), F32)],
      compiler_params=_cparams(("parallel", "arbitrary")),
      name="ffn_dense",
  )(x2, gain, wg, wu, wd)


def _router_kernel(x_ref, g_ref, wr_ref, sel_ref):
  h = _rms(x_ref[...], g_ref[...])
  logits = jnp.dot(h, wr_ref[...], preferred_element_type=F32, precision=lax.Precision.HIGHEST)
  lane = lax.broadcasted_iota(jnp.int32, logits.shape, 1)
  neg = jnp.float32(-jnp.inf)
  lg = jnp.where(lane < N_EXPERTS, logits, neg)
  m1 = jnp.max(lg, axis=-1, keepdims=True)
  i1 = jnp.min(jnp.where(lg == m1, lane, LANE), axis=-1, keepdims=True)
  lg2 = jnp.where(lane == i1, neg, lg)
  m2 = jnp.max(lg2, axis=-1, keepdims=True)
  i2 = jnp.min(jnp.where(lg2 == m2, lane, LANE), axis=-1, keepdims=True)
  e2 = jnp.exp(m2 - m1)
  w1 = 1.0 / (1.0 + e2)
  w2 = e2 / (1.0 + e2)
  sel_ref[...] = jnp.where(lane == 0, i1.astype(F32), jnp.where(lane == 1, i2.astype(F32),
                           jnp.where(lane == 2, w1, jnp.where(lane == 3, w2, 0.0))))


def _router(x2, gain, wr, tm):
  T = x2.shape[0]
  return pl.pallas_call(
      _router_kernel,
      grid=(T // tm,),
      in_specs=[pl.BlockSpec((tm, D_MODEL), lambda i: (i, 0)),
                pl.BlockSpec((1, D_MODEL), lambda i: (0, 0)),
                pl.BlockSpec((D_MODEL, LANE), lambda i: (0, 0))],
      out_specs=pl.BlockSpec((tm, LANE), lambda i: (i, 0)),
      out_shape=jax.ShapeDtypeStruct((T, LANE), F32),
      compiler_params=_cparams(("parallel",)),
      name="moe_router",
  )(x2, gain, wr)


def _route_plan(sel, tm):
  T = sel.shape[0]
  e_flat = sel[:, 0:2].astype(jnp.int32).reshape(1, 2 * T)
  onehot = (e_flat == jnp.arange(N_EXPERTS, dtype=jnp.int32)[:, None]).astype(jnp.int32)
  csum = jnp.cumsum(onehot, axis=1)
  counts = csum[:, -1]
  tiles_e = (counts + tm - 1) // tm
  tile_end = jnp.cumsum(tiles_e)
  offs = (tile_end - tiles_e) * tm
  pos = jnp.sum(onehot * (csum - 1 + offs[:, None]), axis=0)
  n_tiles = (2 * T) // tm + N_EXPERTS
  tile_expert = jnp.sum(jnp.arange(n_tiles, dtype=jnp.int32)[:, None] >= tile_end[None, :], axis=1)
  tile_expert = jnp.minimum(tile_expert, N_EXPERTS - 1).astype(jnp.int32)
  return pos.astype(jnp.int32), tile_expert, tile_end[-1:].astype(jnp.int32), n_tiles


def _row_copy(src, dst, sem, rows=1, src_row=0, dst_row=0):
  return pltpu.make_async_copy(src.at[pl.ds(src_row, rows), :], dst.at[pl.ds(dst_row, rows), :], sem)


def _dispatch_kernel(pos_ref, x_hbm, xs_in, xs_hbm, sem):
  del xs_in
  tt = pos_ref.shape[-1] // 2
  base = pl.program_id(0) * tt

  def body(r, _):
    for c in range(2):
      _row_copy(x_hbm, xs_hbm, sem, 1, base + r, pos_ref[0, 0, 2 * r + c]).start()
    return 0

  lax.fori_loop(0, tt, body, 0, unroll=8)
  _row_copy(x_hbm, xs_hbm, sem, 2 * tt).wait()


def _dispatch(x2, pos, n_rows, tt):
  T = x2.shape[0]
  pos3 = pos.reshape(T // tt, 1, 2 * tt)
  xs0 = jnp.zeros((n_rows, D_MODEL), F32)
  return pl.pallas_call(
      _dispatch_kernel,
      grid=(T // tt,),
      in_specs=[pl.BlockSpec((1, 1, 2 * tt), lambda i: (i, 0, 0), memory_space=pltpu.SMEM),
                pl.BlockSpec(memory_space=pl.ANY), pl.BlockSpec(memory_space=pl.ANY)],
      out_specs=pl.BlockSpec(memory_space=pl.ANY),
      out_shape=jax.ShapeDtypeStruct((n_rows, D_MODEL), F32),
      scratch_shapes=[pltpu.SemaphoreType.DMA(())],
      input_output_aliases={2: 0},
      compiler_params=_cparams(("arbitrary",)),
      name="moe_dispatch",
  )(pos3, x2, xs0)


def _moe_group_kernel(te_ref, nu_ref, xs_ref, g_ref, w1_ref, w3_ref, w2_ref, ys_ref, h_ref, acc_ref):
  del te_ref
  i = pl.program_id(0)
  f = pl.program_id(1)
  last = pl.num_programs(1) - 1
  used = i < nu_ref[0]

  @pl.when(used & (f == 0))
  def _():
    h_ref[...] = _rms(xs_ref[...], g_ref[...]).astype(BF16)
    acc_ref[...] = jnp.zeros_like(acc_ref)

  @pl.when(used)
  def _():
    h = h_ref[...]
    g = jnp.dot(h, w1_ref[...], preferred_element_type=F32)
    u = jnp.dot(h, w3_ref[...], preferred_element_type=F32)
    a = (_silu(g) * u).astype(BF16)
    acc_ref[...] += jnp.dot(a, w2_ref[...], preferred_element_type=F32)

  @pl.when(used & (f == last))
  def _():
    ys_ref[...] = acc_ref[...]

  @pl.when(jnp.logical_not(used) & (f == last))
  def _():
    ys_ref[...] = jnp.zeros_like(ys_ref)


def _moe_group(xs, gain, tile_expert, n_used, w1, w3, w2, tm, tf):
  P = xs.shape[0]
  F = w1.shape[2]
  nf = F // tf
  fidx = lambda i, f, te, nu: jnp.where(i < nu[0], f, nf - 1)
  grid_spec = pltpu.PrefetchScalarGridSpec(
      num_scalar_prefetch=2,
      grid=(P // tm, nf),
      in_specs=[
          pl.BlockSpec((tm, D_MODEL), lambda i, f, te, nu: (i, 0)),
          pl.BlockSpec((1, D_MODEL), lambda i, f, te, nu: (0, 0)),
          pl.BlockSpec((None, D_MODEL, tf), lambda i, f, te, nu: (te[i], 0, fidx(i, f, te, nu))),
          pl.BlockSpec((None, D_MODEL, tf), lambda i, f, te, nu: (te[i], 0, fidx(i, f, te, nu))),
          pl.BlockSpec((None, tf, D_MODEL), lambda i, f, te, nu: (te[i], fidx(i, f, te, nu), 0)),
      ],
      out_specs=pl.BlockSpec((tm, D_MODEL), lambda i, f, te, nu: (i, 0)),
      scratch_shapes=[pltpu.VMEM((tm, D_MODEL), BF16), pltpu.VMEM((tm, D_MODEL), F32)],
  )
  return pl.pallas_call(
      _moe_group_kernel,
      grid_spec=grid_spec,
      out_shape=jax.ShapeDtypeStruct((P, D_MODEL), F32),
      compiler_params=_cparams(("arbitrary", "arbitrary")),
      name="moe_group",
  )(tile_expert, n_used, xs, gain, w1, w3, w2)


def _combine_kernel(pos_ref, x_ref, sel_ref, ys_hbm, o_ref, ybuf, sem):
  tt = x_ref.shape[0]

  def body(r, _):
    for c in range(2):
      _row_copy(ys_hbm, ybuf.at[c], sem, 1, pos_ref[0, 0, 2 * r + c], r).start()
    return 0

  lax.fori_loop(0, tt, body, 0, unroll=8)
  for c in range(2):
    _row_copy(ys_hbm, ybuf.at[c], sem, tt).wait()
  sel = sel_ref[...]
  o_ref[...] = x_ref[...] + sel[:, 2:3] * ybuf[0] + sel[:, 3:4] * ybuf[1]


def _combine(x2, sel, pos, ys, tt):
  T = x2.shape[0]
  pos3 = pos.reshape(T // tt, 1, 2 * tt)
  return pl.pallas_call(
      _combine_kernel,
      grid=(T // tt,),
      in_specs=[pl.BlockSpec((1, 1, 2 * tt), lambda i: (i, 0, 0), memory_space=pltpu.SMEM),
                pl.BlockSpec((tt, D_MODEL), lambda i: (i, 0)),
                pl.BlockSpec((tt, LANE), lambda i: (i, 0)),
                pl.BlockSpec(memory_space=pl.ANY)],
      out_specs=pl.BlockSpec((tt, D_MODEL), lambda i: (i, 0)),
      out_shape=jax.ShapeDtypeStruct((T, D_MODEL), F32),
      scratch_shapes=[pltpu.VMEM((2, tt, D_MODEL), F32), pltpu.SemaphoreType.DMA(())],
      compiler_params=_cparams(("arbitrary",)),
      name="moe_combine",
  )(pos3, x2, sel, ys)


def _block_diag(w):
  n, d, e = w.shape
  eye = jnp.eye(n, dtype=w.dtype)
  return (eye[:, None, :, None] * w[:, :, None, :]).reshape(n * d, n * e)


def _head_slots(w, width, offset=0):
  r = w.shape[0]
  w3 = w.reshape(r, MLA_HEADS, width)
  out = jnp.zeros((r, MLA_HEADS, HEAD_SLOT), w.dtype).at[:, :, offset:offset + width].set(w3)
  return out.reshape(r, MLA_HEADS * HEAD_SLOT)


def _slot_gain(g):
  return jnp.zeros((1, HEAD_SLOT), F32).at[0, :MLA_QK].set(g.astype(F32))


def kernel(x, positions, norm_mix, w_in, conv_w, conv_b, lru_wa, lru_ba, lru_wx, lru_bx, lru_lam,
           out_g_a, hg_lb_logits, hg_norm_g, mla_q_norm, mla_w_uq, mla_kv_norm, mla_w_ukv,
           qk_norm_q, qk_norm_k, out_g_c, w_out, norm_ffn, ffn_w_gate_up, ffn_w_down,
           moe_router, moe_w1, moe_w3, moe_w2):
  B, S, D = x.shape
  T = B * S
  depth = w_in.shape[0]
  tm = 512
  row = lambda v: v.astype(F32).reshape(1, -1)

  p = jax.nn.softmax(hg_lb_logits.astype(F32), axis=0)
  lower_bounds = jnp.cumsum(p, axis=0) - p[0:1]

  x2 = x.reshape(T, D)
  cos_t, sin_t = _rope_tables(positions.reshape(T, 1), tm)

  for l in range(depth):
    n_main = w_in.shape[2] - MLA_ROPE
    w_in_l = jnp.concatenate(
        [w_in[l][:, :n_main], jnp.zeros((D, MLA_NOPE), F32), w_in[l][:, n_main:],
         jnp.zeros((D, HEAD_SLOT - MLA_QK), F32)], axis=1).astype(BF16)
    wg_lru = jnp.concatenate(
        [_block_diag(lru_wa[l, 0]), _block_diag(lru_wx[l, 0]),
         _block_diag(lru_wa[l, 1]), _block_diag(lru_wx[l, 1])], axis=1).astype(BF16)
    bg_lru = jnp.concatenate([lru_ba[l, 0], lru_bx[l, 0], lru_ba[l, 1], lru_bx[l, 1]]).reshape(1, -1)
    wq = _head_slots(mla_w_uq[l], MLA_QK).astype(BF16)
    wkv = mla_w_ukv[l].reshape(MLA_KV_RANK, MLA_HEADS, MLA_NOPE + MLA_V)
    wk = _head_slots(wkv[:, :, :MLA_NOPE].reshape(MLA_KV_RANK, -1), MLA_NOPE).astype(BF16)
    wv = wkv[:, :, MLA_NOPE:].reshape(MLA_KV_RANK, MLA_WIDTH).astype(BF16)

    pa, pb, pc = _norm_inproj(x2, row(norm_mix[l]), w_in_l, tm)
    ya = _rglru(pa, conv_w[l], row(conv_b[l]), wg_lru, bg_lru, lru_lam[l], row(out_g_a[l]), B, S)
    yb = _hgrn2(pb, lower_bounds[l], row(jnp.tile(hg_norm_g[l], HG_HEADS)), B, S)
    q, k, v = _mla_prep(pc, cos_t, sin_t, row(mla_q_norm[l]), row(mla_kv_norm[l]), wq, wk, wv,
                        _slot_gain(qk_norm_q[l]), _slot_gain(qk_norm_k[l]), B, S, min(tm, S))
    yc = _attention(q, k, v, B, S, min(512, S))
    x2 = _outproj(x2, ya, yb, yc, row(out_g_c[l]), w_out[l].astype(BF16), tm)

    if l % 2 == 0:
      wgu = ffn_w_gate_up[l // 2]
      pad = D_FF_PAD - D_FF
      wg = jnp.pad(wgu[:, :D_FF], ((0, 0), (0, pad))).astype(BF16)
      wu = jnp.pad(wgu[:, D_FF:], ((0, 0), (0, pad))).astype(BF16)
      wd = jnp.pad(ffn_w_down[l // 2], ((0, pad), (0, 0))).astype(BF16)
      x2 = _ffn(x2, row(norm_ffn[l]), wg, wu, wd, tm, D_FF_PAD // 2)
    else:
      wr = jnp.pad(moe_router[l // 2], ((0, 0), (0, LANE - N_EXPERTS)))
      gain = row(norm_ffn[l])
      tme = 1024 if T >= 8192 else 128
      sel = _router(x2, gain, wr, tm)
      pos, tile_expert, n_used, n_tiles = _route_plan(sel, tme)
      xs = _dispatch(x2, pos, n_tiles * tme, tm)
      ys = _moe_group(xs, gain, tile_expert, n_used, moe_w1[l // 2].astype(BF16), moe_w3[l // 2].astype(BF16),
                      moe_w2[l // 2].astype(BF16), tme, D_EXPERT // 7)
      x2 = _combine(x2, sel, pos, ys, tm)
  return x2.reshape(B, S, D)
```

```python
import functools
import math

import jax
import jax.numpy as jnp
from jax import lax
from jax.experimental import pallas as pl
from jax.experimental.pallas import tpu as pltpu

F32 = jnp.float32
BF16 = jnp.bfloat16

D_MODEL = 1024
LRU_WIDTH = 256
LRU_BLOCKS = 4
CONV_WIDTH = 4
LRU_C = 8.0
HG_HEADS = 4
HG_DK = 64
HG_WIDTH = 256
MLA_HEADS = 8
MLA_NOPE = 64
MLA_ROPE = 32
MLA_V = 64
MLA_QK = MLA_NOPE + MLA_ROPE
MLA_Q_RANK = 256
MLA_KV_RANK = 128
MLA_WIDTH = MLA_HEADS * MLA_V
ROPE_THETA = 10000.0
D_FF = 2752
N_EXPERTS = 8
D_EXPERT = 3584
EPS = 1e-6

LANE = 128
SUBLANE = 8
HEAD_SLOT = 128
HG_CHUNK = 64
D_FF_PAD = 2816
MOE_ZERO_ROWS = 256
VMEM_LIMIT = 56 * 1024 * 1024

PA_W, PB_W, PC_W = 512, 1280, 512


def _cparams(sem):
  return pltpu.CompilerParams(dimension_semantics=sem, vmem_limit_bytes=VMEM_LIMIT)


def _sigmoid(x):
  return 1.0 / (1.0 + jnp.exp(-x))


def _silu(x):
  return x * _sigmoid(x)


def _rms(x, gain, n=None):
  n = x.shape[-1] if n is None else n
  ms = jnp.sum(x * x, axis=-1, keepdims=True) * (1.0 / n)
  return x * lax.rsqrt(ms + EPS) * gain


def _norm_inproj_kernel(x_ref, g_ref, w_ref, oa_ref, ob_ref, oc_ref):
  h = _rms(x_ref[...], g_ref[...]).astype(BF16)
  p = jnp.dot(h, w_ref[...], preferred_element_type=F32)
  oa_ref[...] = p[:, :PA_W]
  ob_ref[...] = p[:, PA_W:PA_W + PB_W]
  oc_ref[...] = p[:, PA_W + PB_W:]


def _norm_inproj(x2, gain, w, tm):
  T = x2.shape[0]
  n = w.shape[1]
  return pl.pallas_call(
      _norm_inproj_kernel,
      grid=(T // tm,),
      in_specs=[
          pl.BlockSpec((tm, D_MODEL), lambda i: (i, 0)),
          pl.BlockSpec((1, D_MODEL), lambda i: (0, 0)),
          pl.BlockSpec((D_MODEL, n), lambda i: (0, 0)),
      ],
      out_specs=[
          pl.BlockSpec((tm, PA_W), lambda i: (i, 0)),
          pl.BlockSpec((tm, PB_W), lambda i: (i, 0)),
          pl.BlockSpec((tm, PC_W), lambda i: (i, 0)),
      ],
      out_shape=[
          jax.ShapeDtypeStruct((T, PA_W), F32),
          jax.ShapeDtypeStruct((T, PB_W), F32),
          jax.ShapeDtypeStruct((T, PC_W), F32),
      ],
      compiler_params=_cparams(("parallel",)),
      name="norm_inproj",
  )(x2, gain, w)


def _rglru_kernel(pa_ref, cw_ref, cb_ref, wg_ref, bg_ref, lam_ref, og_ref, o_ref,
                  af_ref, bf_ref, ab_ref, bb_ref, hf_ref, hb_ref):
  S = pa_ref.shape[0]
  W = LRU_WIDTH
  xa = pa_ref[:, 0:W]
  row = lax.broadcasted_iota(jnp.int32, (S, W), 0)

  xc = cb_ref[...] + cw_ref[2:3, :] * xa
  for k in (0, 1, 3):
    d = k - CONV_WIDTH // 2
    shifted = pltpu.roll(xa, (-d) % S, 0)
    valid = (row >= -d) if d < 0 else (row < S - d)
    xc = xc + cw_ref[k:k + 1, :] * jnp.where(valid, shifted, 0.0)

  z = jnp.dot(xc.astype(BF16), wg_ref[...], preferred_element_type=F32) + bg_ref[...]
  lam = lam_ref[...]
  sp = jnp.maximum(-lam, 0.0) + jnp.log1p(jnp.exp(-jnp.abs(lam)))
  for d, (a_ref, b_ref) in enumerate(((af_ref, bf_ref), (ab_ref, bb_ref))):
    r = _sigmoid(z[:, d * 2 * W:d * 2 * W + W])
    i = _sigmoid(z[:, d * 2 * W + W:(d + 1) * 2 * W])
    log_a = (-LRU_C) * r * sp[d:d + 1, :]
    a = jnp.exp(log_a)
    a_ref[...] = a
    y2 = 2.0 * log_a
    u = a * a
    near = jnp.where(u == 1.0, y2, (u - 1.0) * y2 / jnp.log(u))
    em1 = jnp.where(y2 > -0.5, near, u - 1.0)
    b_ref[...] = jnp.sqrt(-em1) * (i * xc)

  n = S // SUBLANE
  r8 = lax.broadcasted_iota(jnp.int32, (SUBLANE, W), 0)

  def body(c, carry):
    cf, cb = carry
    sl = pl.ds(pl.multiple_of(c * SUBLANE, SUBLANE), SUBLANE)
    a = af_ref[sl, :]
    b = bf_ref[sl, :]
    for s in (1, 2, 4):
      m = r8 >= s
      b = jnp.where(m, a * pltpu.roll(b, s, 0) + b, b)
      a = jnp.where(m, a * pltpu.roll(a, s, 0), a)
    h = b + a * cf
    hf_ref[sl, :] = h
    cf = h[SUBLANE - 1:SUBLANE, :]

    sl = pl.ds(pl.multiple_of((n - 1 - c) * SUBLANE, SUBLANE), SUBLANE)
    a = ab_ref[sl, :]
    b = bb_ref[sl, :]
    for s in (1, 2, 4):
      m = r8 < SUBLANE - s
      b = jnp.where(m, a * pltpu.roll(b, SUBLANE - s, 0) + b, b)
      a = jnp.where(m, a * pltpu.roll(a, SUBLANE - s, 0), a)
    h = b + a * cb
    hb_ref[sl, :] = h
    cb = h[0:1, :]
    return cf, cb

  zero = jnp.zeros((1, W), F32)
  lax.fori_loop(0, n, body, (zero, zero), unroll=4)

  g = pa_ref[:, W:2 * W]
  gelu = 0.5 * g * (1.0 + jnp.tanh(math.sqrt(2.0 / math.pi) * (g + 0.044715 * (g * g * g))))
  y = (hf_ref[...] + hb_ref[...]) * gelu
  o_ref[...] = _rms(y, og_ref[...]).astype(o_ref.dtype)


def _rglru(pa, conv_w, conv_b, wg, bg, lam, out_g, B, S):
  T = B * S
  W = LRU_WIDTH
  const = lambda shape: pl.BlockSpec(shape, lambda b: (0,) * len(shape))
  return pl.pallas_call(
      _rglru_kernel,
      grid=(B,),
      in_specs=[
          pl.BlockSpec((S, PA_W), lambda b: (b, 0)),
          const((CONV_WIDTH, W)), const((1, W)), const((W, 4 * W)), const((1, 4 * W)),
          const((2, W)), const((1, W)),
      ],
      out_specs=pl.BlockSpec((S, W), lambda b: (b, 0)),
      out_shape=jax.ShapeDtypeStruct((T, W), BF16),
      scratch_shapes=[pltpu.VMEM((S, W), F32) for _ in range(6)],
      compiler_params=_cparams(("parallel",)),
      name="rglru",
  )(pa, conv_w, conv_b, wg, bg, lam, out_g)


_HG_LEVELS = (1, 2, 4, 8, 16, 32)


def _bcast_row_in_blocks(x, block, row):
  L, C = x.shape
  x3 = x.reshape(L // block, block, C)
  return jnp.broadcast_to(x3[:, row:row + 1, :], x3.shape).reshape(L, C)


def _inc_prefix_products(f, row):
  L = f.shape[0]
  out = {1: f}
  p2 = f * jnp.where((row & 1) == 1, pltpu.roll(f, 1, 0), 1.0)
  out[2] = p2
  r4 = row & 3
  p4 = p2 * jnp.where(r4 == 2, pltpu.roll(p2, 1, 0), jnp.where(r4 == 3, pltpu.roll(p2, 2, 0), 1.0))
  out[4] = p4
  p, h = p4, 4
  while h < L:
    p = p * jnp.where((row & h) != 0, _bcast_row_in_blocks(p, 2 * h, h - 1), 1.0)
    h *= 2
    out[h] = p
  return out


def _inc_suffix_products(f, row):
  L = f.shape[0]
  up = lambda x, s: pltpu.roll(x, L - s, 0)
  out = {1: f}
  p2 = f * jnp.where((row & 1) == 0, up(f, 1), 1.0)
  out[2] = p2
  r4 = row & 3
  p4 = p2 * jnp.where(r4 == 1, up(p2, 1), jnp.where(r4 == 0, up(p2, 2), 1.0))
  out[4] = p4
  p, h = p4, 4
  while h < L:
    p = p * jnp.where((row & h) == 0, _bcast_row_in_blocks(p, 2 * h, h), 1.0)
    h *= 2
    out[h] = p
  return out


def _hg_chunk(q, f, v, st_ref, masks_ref, hmask, row, reverse):
  L = q.shape[0]
  k = 1.0 - f
  pre = _inc_prefix_products(f, row)
  suf = _inc_suffix_products(f, row)
  if not reverse:
    q_dec = pre
    k_dec = {h: (jnp.where((row & (h - 1)) == h - 1, 1.0, pltpu.roll(suf[h], L - 1, 0)) if h > 1 else None)
             for h in suf}
  else:
    q_dec = suf
    k_dec = {h: (jnp.where((row & (h - 1)) == 0, 1.0, pltpu.roll(pre[h], 1, 0)) if h > 1 else None)
             for h in pre}

  def head_blockdiag(x):
    return jnp.concatenate([x] * HG_HEADS, axis=0) * hmask

  nt = (((1,), (1,)), ((), ()))
  kb = k.astype(BF16)
  p = masks_ref[0] * lax.dot_general(q.astype(BF16), head_blockdiag(kb), nt, preferred_element_type=F32)
  for li, h in enumerate(_HG_LEVELS):
    qh = (q * q_dec[h]).astype(BF16)
    kh = kb if h == 1 else (k * k_dec[h]).astype(BF16)
    p = p + masks_ref[li + 1] * lax.dot_general(qh, head_blockdiag(kh), nt, preferred_element_type=F32)

  vb = v.astype(BF16)
  o = jnp.dot(p.astype(BF16), head_blockdiag(vb), preferred_element_type=F32)
  st = st_ref[...]
  o = o + lax.dot_general((q * q_dec[L]).astype(BF16), st.astype(BF16), nt, preferred_element_type=F32)

  ke = (k * k_dec[L]).astype(BF16)
  end = q_dec[L][0:1, :] if reverse else q_dec[L][L - 1:L, :]
  tn = (((0,), (0,)), ((), ()))
  upd = lax.dot_general(vb, ke, tn, preferred_element_type=F32)
  st_ref[...] = st * end + upd * hmask.astype(F32)
  return o


def _hgrn2_kernel(pb_ref, lb_ref, ng_ref, mf_ref, mb_ref, hm_ref, ones_ref, o_ref,
                  of_ref, ob_ref, sf_ref, sb_ref):
  S = pb_ref.shape[0]
  W = HG_WIDTH
  L = HG_CHUNK
  n = S // L
  row = lax.broadcasted_iota(jnp.int32, (L, W), 0)
  hmask = hm_ref[...]
  sf_ref[...] = jnp.zeros_like(sf_ref)
  sb_ref[...] = jnp.zeros_like(sb_ref)

  def load(c, fcol, lb):
    sl = pl.ds(pl.multiple_of(c * L, L), L)
    q = _silu(pb_ref[sl, 0:W]) * (HG_DK ** -0.5)
    f = lb + (1.0 - lb) * _sigmoid(pb_ref[sl, fcol:fcol + W])
    v = pb_ref[sl, 3 * W:4 * W]
    return sl, q, f, v

  def body(c, _):
    sl, q, f, v = load(c, W, lb_ref[0:1, :])
    of_ref[sl, :] = _hg_chunk(q, f, v, sf_ref, mf_ref, hmask, row, False)
    sl, q, f, v = load(n - 1 - c, 2 * W, lb_ref[1:2, :])
    ob_ref[sl, :] = _hg_chunk(q, f, v, sb_ref, mb_ref, hmask, row, True)
    return 0

  lax.fori_loop(0, n, body, 0)

  o = of_ref[...] + ob_ref[...]
  ms = jnp.dot((o * o).astype(BF16), ones_ref[...], preferred_element_type=F32) * (1.0 / HG_DK)
  y = o * lax.rsqrt(ms + EPS) * ng_ref[...]
  o_ref[...] = (y * _silu(pb_ref[:, 4 * W:5 * W])).astype(o_ref.dtype)


def _hg_constants():
  L = HG_CHUNK
  t = jnp.arange(L)[:, None]
  s = jnp.arange(L)[None, :]
  fwd, bwd = [t == s], [t == s]
  for h in _HG_LEVELS:
    same = (t // (2 * h)) == (s // (2 * h))
    fwd.append(same & ((t & h) != 0) & ((s & h) == 0))
    bwd.append(same & ((t & h) == 0) & ((s & h) != 0))
  tile = lambda m: jnp.tile(jnp.stack(m).astype(F32), (1, 1, HG_HEADS))
  hid = jnp.arange(HG_WIDTH) // HG_DK
  hmask = (hid[:, None] == hid[None, :])
  return tile(fwd), tile(bwd), hmask.astype(BF16), hmask.astype(BF16)


def _hgrn2(pb, lb, norm_g, B, S):
  T = B * S
  W = HG_WIDTH
  mf, mb, hmask, ones = _hg_constants()
  const = lambda shape: pl.BlockSpec(shape, lambda b: (0,) * len(shape))
  return pl.pallas_call(
      _hgrn2_kernel,
      grid=(B,),
      in_specs=[
          pl.BlockSpec((S, PB_W), lambda b: (b, 0)),
          const((2, W)), const((1, W)), const(mf.shape), const(mb.shape), const((W, W)), const((W, W)),
      ],
      out_specs=pl.BlockSpec((S, W), lambda b: (b, 0)),
      out_shape=jax.ShapeDtypeStruct((T, W), BF16),
      scratch_shapes=[pltpu.VMEM((S, W), F32), pltpu.VMEM((S, W), F32),
                      pltpu.VMEM((W, W), F32), pltpu.VMEM((W, W), F32)],
      compiler_params=_cparams(("parallel",)),
      name="hgrn2",
  )(pb, lb, norm_g, mf, mb, hmask, ones)


ROPE_HALF = MLA_ROPE // 2
_SLOT_LANES = tuple(
    [ROPE_HALF + d for d in range(HEAD_SLOT // 2 - ROPE_HALF)]
    + [HEAD_SLOT // 2 + ROPE_HALF + d for d in range(MLA_NOPE - (HEAD_SLOT // 2 - ROPE_HALF))]
    + list(range(ROPE_HALF)) + [HEAD_SLOT // 2 + d for d in range(ROPE_HALF)])
ATTN_MAX_BOUND = 60.0
ATTN_KEY_BLOCK = 256


def _rope_tables_kernel(pos_ref, invf_ref, sgn_ref, cos_ref, sin_ref):
  ang = pos_ref[...].astype(F32) * invf_ref[...]
  cos_ref[...] = jnp.cos(ang)
  sin_ref[...] = sgn_ref[...] * jnp.sin(ang)


def _rope_tables(pos2, tm):
  T = pos2.shape[0]
  inv_freq = ROPE_THETA ** (-jnp.arange(0, ROPE_HALF, dtype=F32) * (2.0 / MLA_ROPE))
  x1 = jnp.asarray(_SLOT_LANES[MLA_NOPE:MLA_NOPE + ROPE_HALF])
  x2 = jnp.asarray(_SLOT_LANES[MLA_NOPE + ROPE_HALF:])
  invf = jnp.zeros((1, HEAD_SLOT), F32).at[0, x1].set(inv_freq).at[0, x2].set(inv_freq)
  sgn = jnp.zeros((1, HEAD_SLOT), F32).at[0, x1].set(-1.0).at[0, x2].set(1.0)
  row = pl.BlockSpec((1, HEAD_SLOT), lambda i: (0, 0))
  return pl.pallas_call(
      _rope_tables_kernel,
      grid=(T // tm,),
      in_specs=[pl.BlockSpec((tm, 1), lambda i: (i, 0)), row, row],
      out_specs=[pl.BlockSpec((tm, HEAD_SLOT), lambda i: (i, 0))] * 2,
      out_shape=[jax.ShapeDtypeStruct((T, HEAD_SLOT), F32)] * 2,
      compiler_params=_cparams(("parallel",)),
      name="rope_tables",
  )(pos2, invf, sgn)


def _rope(t, cos_t, sin_t):
  return t * cos_t + pltpu.roll(t, HEAD_SLOT // 2, 1) * sin_t


def _mla_prep_kernel(pc_ref, cos_ref, sin_ref, qn_ref, kvn_ref, wq_ref, wk_ref, wv_ref, vone_ref, gq_ref, gk_ref,
                     q_ref, k_ref, v_ref):
  cos_t = cos_ref[...]
  sin_t = sin_ref[...]
  cq = _rms(pc_ref[:, 0:MLA_Q_RANK], qn_ref[...]).astype(BF16)
  ckv = _rms(pc_ref[:, MLA_Q_RANK:MLA_Q_RANK + MLA_KV_RANK], kvn_ref[...]).astype(BF16)
  kr_slot = pc_ref[:, MLA_Q_RANK + MLA_KV_RANK:]
  q_all = jnp.dot(cq, wq_ref[...], preferred_element_type=F32)
  k_all = jnp.dot(ckv, wk_ref[...], preferred_element_type=F32)
  v_ref[...] = (jnp.dot(ckv, wv_ref[...], preferred_element_type=F32) + vone_ref[...]).astype(v_ref.dtype)
  scale = MLA_QK ** -0.5 * math.log2(math.e)
  for h in range(MLA_HEADS):
    sl = slice(h * HEAD_SLOT, (h + 1) * HEAD_SLOT)
    qh = _rope(_rms(q_all[:, sl], gq_ref[...], MLA_QK), cos_t, sin_t)
    q_ref[h] = (qh * scale).astype(q_ref.dtype)
    kh = _rope(_rms(k_all[:, sl] + kr_slot, gk_ref[...], MLA_QK), cos_t, sin_t)
    k_ref[h] = kh.astype(k_ref.dtype)


def _mla_prep(pc, cos_t, sin_t, q_norm, kv_norm, wq, wk, wv, vone, gq, gk, B, S, ts):
  H = MLA_HEADS
  nb = S // ts
  const = lambda shape: pl.BlockSpec(shape, lambda b, s: (0,) * len(shape))
  tok = lambda w: pl.BlockSpec((ts, w), lambda b, s: (b * nb + s, 0))
  return pl.pallas_call(
      _mla_prep_kernel,
      grid=(B, nb),
      in_specs=[
          tok(PC_W), tok(HEAD_SLOT), tok(HEAD_SLOT),
          const((1, MLA_Q_RANK)), const((1, MLA_KV_RANK)),
          const(wq.shape), const(wk.shape), const(wv.shape), const(vone.shape),
          const((1, HEAD_SLOT)), const((1, HEAD_SLOT)),
      ],
      out_specs=[
          pl.BlockSpec((None, H, ts, HEAD_SLOT), lambda b, s: (b, 0, s, 0)),
          pl.BlockSpec((None, H, ts, HEAD_SLOT), lambda b, s: (b, 0, s, 0)),
          pl.BlockSpec((ts, H * HEAD_SLOT), lambda b, s: (b * nb + s, 0)),
      ],
      out_shape=[
          jax.ShapeDtypeStruct((B, H, S, HEAD_SLOT), BF16),
          jax.ShapeDtypeStruct((B, H, S, HEAD_SLOT), BF16),
          jax.ShapeDtypeStruct((B * S, H * HEAD_SLOT), BF16),
      ],
      compiler_params=_cparams(("parallel", "parallel")),
      name="mla_prep",
  )(pc, cos_t, sin_t, q_norm, kv_norm, wq, wk, wv, vone, gq, gk)


def _attention_kernel(bound_ref, q_ref, k_ref, v_ref, o_ref):
  tq = q_ref.shape[1]
  S = k_ref.shape[1]
  bound = bound_ref[0]
  nt = (((1,), (1,)), ((), ()))

  def finish(res):
    lane = lax.broadcasted_iota(jnp.int32, (tq, HEAD_SLOT), 1)
    even = res[0] * (1.0 / res[0][:, MLA_V:MLA_V + 1])
    odd = res[1] * (1.0 / res[1][:, 0:1])
    o_ref[...] = jnp.where(lane < MLA_V, even, odd).astype(o_ref.dtype)

  @pl.when(bound <= ATTN_MAX_BOUND)
  def _():
    res = []
    for h in range(2):
      acc = None
      for j in range(S // ATTN_KEY_BLOCK):
        ks = slice(j * ATTN_KEY_BLOCK, (j + 1) * ATTN_KEY_BLOCK)
        s = lax.dot_general(q_ref[h], k_ref[h, ks, :], nt, preferred_element_type=F32)
        p = jnp.exp2(s - bound).astype(BF16)
        c = jnp.dot(p, v_ref[ks, h * HEAD_SLOT:(h + 1) * HEAD_SLOT], preferred_element_type=F32)
        acc = c if acc is None else acc + c
      res.append(acc)
    finish(res)

  @pl.when(bound > ATTN_MAX_BOUND)
  def _():
    res = []
    for h in range(2):
      s = lax.dot_general(q_ref[h], k_ref[h], nt, preferred_element_type=F32)
      p = jnp.exp2(s - jnp.max(s, axis=-1, keepdims=True)).astype(BF16)
      res.append(jnp.dot(p, v_ref[:, h * HEAD_SLOT:(h + 1) * HEAD_SLOT], preferred_element_type=F32))
    finish(res)


def _attention(bound, q, k, v, B, S, tq):
  H = MLA_HEADS
  nq = S // tq
  return pl.pallas_call(
      _attention_kernel,
      grid=(B, H // 2, nq),
      in_specs=[
          pl.BlockSpec(memory_space=pltpu.SMEM),
          pl.BlockSpec((None, 2, tq, HEAD_SLOT), lambda b, hp, i: (b, hp, i, 0)),
          pl.BlockSpec((None, 2, S, HEAD_SLOT), lambda b, hp, i: (b, hp, 0, 0)),
          pl.BlockSpec((S, 2 * HEAD_SLOT), lambda b, hp, i: (b, hp)),
      ],
      out_specs=pl.BlockSpec((tq, 2 * MLA_V), lambda b, hp, i: (b * nq + i, hp)),
      out_shape=jax.ShapeDtypeStruct((B * S, MLA_WIDTH), F32),
      compiler_params=_cparams(("parallel", "parallel", "parallel")),
      name="mla_attention",
  )(bound, q, k, v)


def _outproj_kernel(x_ref, ya_ref, yb_ref, yc_ref, gc_ref, w_ref, o_ref):
  yc = _rms(yc_ref[...], gc_ref[...]).astype(BF16)
  a0, a1 = LRU_WIDTH, LRU_WIDTH + HG_WIDTH
  acc = jnp.dot(ya_ref[...], w_ref[0:a0, :], preferred_element_type=F32)
  acc = acc + jnp.dot(yb_ref[...], w_ref[a0:a1, :], preferred_element_type=F32)
  acc = acc + jnp.dot(yc, w_ref[a1:, :], preferred_element_type=F32)
  o_ref[...] = x_ref[...] + acc


def _outproj(x2, ya, yb, yc, gc, w, tm):
  T = x2.shape[0]
  tok = lambda wd: pl.BlockSpec((tm, wd), lambda i: (i, 0))
  return pl.pallas_call(
      _outproj_kernel,
      grid=(T // tm,),
      in_specs=[tok(D_MODEL), tok(LRU_WIDTH), tok(HG_WIDTH), tok(MLA_WIDTH),
                pl.BlockSpec((1, MLA_WIDTH), lambda i: (0, 0)),
                pl.BlockSpec((D_MODEL, D_MODEL), lambda i: (0, 0))],
      out_specs=tok(D_MODEL),
      out_shape=jax.ShapeDtypeStruct((T, D_MODEL), F32),
      compiler_params=_cparams(("parallel",)),
      name="outproj",
  )(x2, ya, yb, yc, gc, w)


def _ffn_kernel(x_ref, g_ref, wg_ref, wu_ref, wd_ref, o_ref, h_ref, acc_ref):
  f = pl.program_id(1)

  @pl.when(f == 0)
  def _():
    h_ref[...] = _rms(x_ref[...], g_ref[...]).astype(BF16)
    acc_ref[...] = jnp.zeros_like(acc_ref)

  h = h_ref[...]
  g = jnp.dot(h, wg_ref[...], preferred_element_type=F32)
  u = jnp.dot(h, wu_ref[...], preferred_element_type=F32)
  a = (_silu(g) * u).astype(BF16)
  acc_ref[...] += jnp.dot(a, wd_ref[...], preferred_element_type=F32)

  @pl.when(f == pl.num_programs(1) - 1)
  def _():
    o_ref[...] = x_ref[...] + acc_ref[...]


def _ffn(x2, gain, wg, wu, wd, tm, tf):
  T = x2.shape[0]
  F = wg.shape[1]
  return pl.pallas_call(
      _ffn_kernel,
      grid=(T // tm, F // tf),
      in_specs=[
          pl.BlockSpec((tm, D_MODEL), lambda i, f: (i, 0)),
          pl.BlockSpec((1, D_MODEL), lambda i, f: (0, 0)),
          pl.BlockSpec((D_MODEL, tf), lambda i, f: (0, f)),
          pl.BlockSpec((D_MODEL, tf), lambda i, f: (0, f)),
          pl.BlockSpec((tf, D_MODEL), lambda i, f: (f, 0)),
      ],
      out_specs=pl.BlockSpec((tm, D_MODEL), lambda i, f: (i, 0)),
      out_shape=jax.ShapeDtypeStruct((T, D_MODEL), F32),
      scratch_shapes=[pltpu.VMEM((tm, D_MODEL), BF16), pltpu.VMEM((tm, D_MODEL), F32)],
      compiler_params=_cparams(("parallel", "arbitrary")),
      name="ffn_dense",
  )(x2, gain, wg, wu, wd)


def _router_kernel(x_ref, g_ref, wr_ref, sel_ref):
  h = _rms(x_ref[...], g_ref[...])
  logits = jnp.dot(h, wr_ref[...], preferred_element_type=F32, precision=lax.Precision.HIGHEST)
  lane = lax.broadcasted_iota(jnp.int32, logits.shape, 1)
  neg = jnp.float32(-jnp.inf)
  lg = jnp.where(lane < N_EXPERTS, logits, neg)
  m1 = jnp.max(lg, axis=-1, keepdims=True)
  i1 = jnp.min(jnp.where(lg == m1, lane, LANE), axis=-1, keepdims=True)
  lg2 = jnp.where(lane == i1, neg, lg)
  m2 = jnp.max(lg2, axis=-1, keepdims=True)
  i2 = jnp.min(jnp.where(lg2 == m2, lane, LANE), axis=-1, keepdims=True)
  e2 = jnp.exp(m2 - m1)
  w1 = 1.0 / (1.0 + e2)
  w2 = e2 / (1.0 + e2)
  sel_ref[...] = jnp.where(lane == 0, i1.astype(F32), jnp.where(lane == 1, i2.astype(F32),
                           jnp.where(lane == 2, w1, jnp.where(lane == 3, w2, 0.0))))


def _router(x2, gain, wr, tm):
  T = x2.shape[0]
  return pl.pallas_call(
      _router_kernel,
      grid=(T // tm,),
      in_specs=[pl.BlockSpec((tm, D_MODEL), lambda i: (i, 0)),
                pl.BlockSpec((1, D_MODEL), lambda i: (0, 0)),
                pl.BlockSpec((D_MODEL, LANE), lambda i: (0, 0))],
      out_specs=pl.BlockSpec((tm, LANE), lambda i: (i, 0)),
      out_shape=jax.ShapeDtypeStruct((T, LANE), F32),
      compiler_params=_cparams(("parallel",)),
      name="moe_router",
  )(x2, gain, wr)


def _route_plan(sel, tm):
  T = sel.shape[0]
  e_flat = sel[:, 0:2].astype(jnp.int32).reshape(1, 2 * T)
  onehot = (e_flat == jnp.arange(N_EXPERTS, dtype=jnp.int32)[:, None]).astype(jnp.int32)
  csum = jnp.cumsum(onehot, axis=1)
  counts = csum[:, -1]
  tiles_e = (counts + tm - 1) // tm
  tile_end = jnp.cumsum(tiles_e)
  offs = (tile_end - tiles_e) * tm
  pos = jnp.sum(onehot * (csum - 1 + offs[:, None]), axis=0)
  n_tiles = (2 * T) // tm + N_EXPERTS
  tile_expert = jnp.sum(jnp.arange(n_tiles, dtype=jnp.int32)[:, None] >= tile_end[None, :], axis=1)
  tile_expert = jnp.minimum(tile_expert, N_EXPERTS - 1).astype(jnp.int32)
  last_tile = jnp.where(tiles_e > 0, offs + (tiles_e - 1) * tm, -1)
  spare = tile_end[-1] + jnp.arange(N_EXPERTS, dtype=jnp.int32)
  spare_tile = jnp.where(spare < n_tiles, spare * tm, -1)
  clear = jnp.concatenate([last_tile, spare_tile]).astype(jnp.int32)
  return pos.astype(jnp.int32), tile_expert, tile_end[-1:].astype(jnp.int32), clear, n_tiles


def _row_copy(src, dst, sem, rows=1, src_row=0, dst_row=0):
  return pltpu.make_async_copy(src.at[pl.ds(src_row, rows), :], dst.at[pl.ds(dst_row, rows), :], sem)


def _dispatch_kernel(pos_ref, lt_ref, x_ref, xs_hbm, zbuf, sem, zsem, *, tm):
  tt = x_ref.shape[0]
  zr = zbuf.shape[0]

  @pl.when(pl.program_id(0) == 0)
  def _():
    zbuf[...] = jnp.zeros_like(zbuf)
    for e in range(lt_ref.shape[0]):
      @pl.when(lt_ref[e] >= 0)
      def _():
        rows = [pl.multiple_of(lt_ref[e] + c * zr, SUBLANE) for c in range(tm // zr)]
        for r0 in rows:
          _row_copy(zbuf, xs_hbm, zsem, zr, 0, r0).start()
        for r0 in rows:
          _row_copy(zbuf, xs_hbm, zsem, zr, 0, r0).wait()

  def body(r, _):
    for c in range(2):
      _row_copy(x_ref, xs_hbm, sem, 1, r, pos_ref[0, 0, 2 * r + c]).start()
    return 0

  lax.fori_loop(0, tt, body, 0, unroll=8)
  for c in range(2):
    _row_copy(x_ref, xs_hbm, sem, tt).wait()


def _dispatch(x2, pos, last_tile, n_rows, tt, tm):
  T = x2.shape[0]
  pos3 = pos.reshape(T // tt, 1, 2 * tt)
  zr = min(MOE_ZERO_ROWS, tm)
  return pl.pallas_call(
      functools.partial(_dispatch_kernel, tm=tm),
      grid=(T // tt,),
      in_specs=[pl.BlockSpec((1, 1, 2 * tt), lambda i: (i, 0, 0), memory_space=pltpu.SMEM),
                pl.BlockSpec(memory_space=pltpu.SMEM),
                pl.BlockSpec((tt, D_MODEL), lambda i: (i, 0))],
      out_specs=pl.BlockSpec(memory_space=pl.ANY),
      out_shape=jax.ShapeDtypeStruct((n_rows, D_MODEL), F32),
      scratch_shapes=[pltpu.VMEM((zr, D_MODEL), F32), pltpu.SemaphoreType.DMA(()), pltpu.SemaphoreType.DMA(())],
      compiler_params=_cparams(("arbitrary",)),
      name="moe_dispatch",
  )(pos3, last_tile, x2)


def _moe_group_kernel(te_ref, nu_ref, xs_ref, g_ref, w1_ref, w3_ref, w2_ref, ys_ref, h_ref, acc_ref):
  del te_ref
  i = pl.program_id(0)
  f = pl.program_id(1)
  last = pl.num_programs(1) - 1
  used = i < nu_ref[0]

  @pl.when(used & (f == 0))
  def _():
    h_ref[...] = _rms(xs_ref[...], g_ref[...]).astype(BF16)
    acc_ref[...] = jnp.zeros_like(acc_ref)

  @pl.when(used)
  def _():
    h = h_ref[...]
    g = jnp.dot(h, w1_ref[...], preferred_element_type=F32)
    u = jnp.dot(h, w3_ref[...], preferred_element_type=F32)
    a = (_silu(g) * u).astype(BF16)
    acc_ref[...] += jnp.dot(a, w2_ref[...], preferred_element_type=F32)

  @pl.when(used & (f == last))
  def _():
    ys_ref[...] = acc_ref[...]

  @pl.when(jnp.logical_not(used) & (f == last))
  def _():
    ys_ref[...] = jnp.zeros_like(ys_ref)


def _moe_group(xs, gain, tile_expert, n_used, w1, w3, w2, tm, tf):
  P = xs.shape[0]
  F = w1.shape[2]
  nf = F // tf
  fidx = lambda i, f, te, nu: jnp.where(i < nu[0], f, nf - 1)
  grid_spec = pltpu.PrefetchScalarGridSpec(
      num_scalar_prefetch=2,
      grid=(P // tm, nf),
      in_specs=[
          pl.BlockSpec((tm, D_MODEL), lambda i, f, te, nu: (jnp.minimum(i, nu[0] - 1), 0)),
          pl.BlockSpec((1, D_MODEL), lambda i, f, te, nu: (0, 0)),
          pl.BlockSpec((None, D_MODEL, tf), lambda i, f, te, nu: (te[i], 0, fidx(i, f, te, nu))),
          pl.BlockSpec((None, D_MODEL, tf), lambda i, f, te, nu: (te[i], 0, fidx(i, f, te, nu))),
          pl.BlockSpec((None, tf, D_MODEL), lambda i, f, te, nu: (te[i], fidx(i, f, te, nu), 0)),
      ],
      out_specs=pl.BlockSpec((tm, D_MODEL), lambda i, f, te, nu: (i, 0)),
      scratch_shapes=[pltpu.VMEM((tm, D_MODEL), BF16), pltpu.VMEM((tm, D_MODEL), F32)],
  )
  return pl.pallas_call(
      _moe_group_kernel,
      grid_spec=grid_spec,
      out_shape=jax.ShapeDtypeStruct((P, D_MODEL), F32),
      compiler_params=_cparams(("arbitrary", "arbitrary")),
      name="moe_group",
  )(tile_expert, n_used, xs, gain, w1, w3, w2)


def _combine_kernel(pos_ref, x_ref, sel_ref, ys_hbm, o_ref, ybuf, sem):
  tt = x_ref.shape[0]

  def body(r, _):
    for c in range(2):
      _row_copy(ys_hbm, ybuf.at[c], sem, 1, pos_ref[0, 0, 2 * r + c], r).start()
    return 0

  lax.fori_loop(0, tt, body, 0, unroll=8)
  for c in range(2):
    _row_copy(ys_hbm, ybuf.at[c], sem, tt).wait()
  sel = sel_ref[...]
  o_ref[...] = x_ref[...] + sel[:, 2:3] * ybuf[0] + sel[:, 3:4] * ybuf[1]


def _combine(x2, sel, pos, ys, tt):
  T = x2.shape[0]
  pos3 = pos.reshape(T // tt, 1, 2 * tt)
  return pl.pallas_call(
      _combine_kernel,
      grid=(T // tt,),
      in_specs=[pl.BlockSpec((1, 1, 2 * tt), lambda i: (i, 0, 0), memory_space=pltpu.SMEM),
                pl.BlockSpec((tt, D_MODEL), lambda i: (i, 0)),
                pl.BlockSpec((tt, LANE), lambda i: (i, 0)),
                pl.BlockSpec(memory_space=pl.ANY)],
      out_specs=pl.BlockSpec((tt, D_MODEL), lambda i: (i, 0)),
      out_shape=jax.ShapeDtypeStruct((T, D_MODEL), F32),
      scratch_shapes=[pltpu.VMEM((2, tt, D_MODEL), F32), pltpu.SemaphoreType.DMA(())],
      compiler_params=_cparams(("arbitrary",)),
      name="moe_combine",
  )(pos3, x2, sel, ys)


def _block_diag(w):
  n, d, e = w.shape
  eye = jnp.eye(n, dtype=w.dtype)
  return (eye[:, None, :, None] * w[:, :, None, :]).reshape(n * d, n * e)


def _head_slots(w, lanes):
  r = w.shape[0]
  w3 = w.reshape(r, MLA_HEADS, len(lanes))
  out = jnp.zeros((r, MLA_HEADS, HEAD_SLOT), w.dtype).at[:, :, jnp.asarray(lanes)].set(w3)
  return out.reshape(r, MLA_HEADS * HEAD_SLOT)


def _slot_gain(g):
  return jnp.zeros((1, HEAD_SLOT), F32).at[0, jnp.asarray(_SLOT_LANES)].set(g.astype(F32))


def _value_slots(wv):
  r = wv.shape[0]
  w4 = wv.reshape(r, MLA_HEADS // 2, 2, MLA_V)
  out = jnp.zeros((r, MLA_HEADS // 2, 2, HEAD_SLOT), wv.dtype)
  out = out.at[:, :, 0, :MLA_V].set(w4[:, :, 0]).at[:, :, 1, MLA_V:].set(w4[:, :, 1])
  one = jnp.zeros((1, MLA_HEADS // 2, 2, HEAD_SLOT), F32).at[:, :, 0, MLA_V].set(1.0).at[:, :, 1, 0].set(1.0)
  return out.reshape(r, MLA_HEADS * HEAD_SLOT), one.reshape(1, MLA_HEADS * HEAD_SLOT)


def _score_bound(gq, gk):
  b = MLA_QK ** 0.5 * math.log2(math.e) * 1.02 * jnp.max(jnp.abs(gq)) * jnp.max(jnp.abs(gk))
  return b.astype(F32).reshape(1)


def kernel(x, positions, norm_mix, w_in, conv_w, conv_b, lru_wa, lru_ba, lru_wx, lru_bx, lru_lam,
           out_g_a, hg_lb_logits, hg_norm_g, mla_q_norm, mla_w_uq, mla_kv_norm, mla_w_ukv,
           qk_norm_q, qk_norm_k, out_g_c, w_out, norm_ffn, ffn_w_gate_up, ffn_w_down,
           moe_router, moe_w1, moe_w3, moe_w2):
  B, S, D = x.shape
  T = B * S
  depth = w_in.shape[0]
  tm = 512
  row = lambda v: v.astype(F32).reshape(1, -1)

  p = jax.nn.softmax(hg_lb_logits.astype(F32), axis=0)
  lower_bounds = jnp.cumsum(p, axis=0) - p[0:1]

  x2 = x.reshape(T, D)
  cos_t, sin_t = _rope_tables(positions.reshape(T, 1), tm)

  for l in range(depth):
    n_main = w_in.shape[2] - MLA_ROPE
    kr_slot_w = jnp.zeros((D, HEAD_SLOT), F32).at[:, jnp.asarray(_SLOT_LANES[MLA_NOPE:])].set(w_in[l][:, n_main:])
    w_in_l = jnp.concatenate([w_in[l][:, :n_main], kr_slot_w], axis=1).astype(BF16)
    wg_lru = jnp.concatenate(
        [_block_diag(lru_wa[l, 0]), _block_diag(lru_wx[l, 0]),
         _block_diag(lru_wa[l, 1]), _block_diag(lru_wx[l, 1])], axis=1).astype(BF16)
    bg_lru = jnp.concatenate([lru_ba[l, 0], lru_bx[l, 0], lru_ba[l, 1], lru_bx[l, 1]]).reshape(1, -1)
    wq = _head_slots(mla_w_uq[l], _SLOT_LANES).astype(BF16)
    wkv = mla_w_ukv[l].reshape(MLA_KV_RANK, MLA_HEADS, MLA_NOPE + MLA_V)
    wk = _head_slots(wkv[:, :, :MLA_NOPE].reshape(MLA_KV_RANK, -1), _SLOT_LANES[:MLA_NOPE]).astype(BF16)
    wv, vone = _value_slots(wkv[:, :, MLA_NOPE:].reshape(MLA_KV_RANK, MLA_WIDTH))
    wv = wv.astype(BF16)

    pa, pb, pc = _norm_inproj(x2, row(norm_mix[l]), w_in_l, tm)
    ya = _rglru(pa, conv_w[l], row(conv_b[l]), wg_lru, bg_lru, lru_lam[l], row(out_g_a[l]), B, S)
    yb = _hgrn2(pb, lower_bounds[l], row(jnp.tile(hg_norm_g[l], HG_HEADS)), B, S)
    q, k, v = _mla_prep(pc, cos_t, sin_t, row(mla_q_norm[l]), row(mla_kv_norm[l]), wq, wk, wv, vone,
                        _slot_gain(qk_norm_q[l]), _slot_gain(qk_norm_k[l]), B, S, min(tm, S))
    yc = _attention(_score_bound(qk_norm_q[l], qk_norm_k[l]), q, k, v, B, S, min(512, S))
    x2 = _outproj(x2, ya, yb, yc, row(out_g_c[l]), w_out[l].astype(BF16), tm)

    if l % 2 == 0:
      wgu = ffn_w_gate_up[l // 2]
      pad = D_FF_PAD - D_FF
      wg = jnp.pad(wgu[:, :D_FF], ((0, 0), (0, pad))).astype(BF16)
      wu = jnp.pad(wgu[:, D_FF:], ((0, 0), (0, pad))).astype(BF16)
      wd = jnp.pad(ffn_w_down[l // 2], ((0, pad), (0, 0))).astype(BF16)
      x2 = _ffn(x2, row(norm_ffn[l]), wg, wu, wd, tm, D_FF_PAD // 2)
    else:
      wr = jnp.pad(moe_router[l // 2], ((0, 0), (0, LANE - N_EXPERTS)))
      gain = row(norm_ffn[l])
      tme = 1024 if T >= 8192 else 128
      sel = _router(x2, gain, wr, tm)
      pos, tile_expert, n_used, last_tile, n_tiles = _route_plan(sel, tme)
      xs = _dispatch(x2, pos, last_tile, n_tiles * tme, tm, tme)
      ys = _moe_group(xs, gain, tile_expert, n_used, moe_w1[l // 2].astype(BF16), moe_w3[l // 2].astype(BF16),
                      moe_w2[l // 2].astype(BF16), tme, D_EXPERT // 7)
      x2 = _combine(x2, sel, pos, ys, tm)
  return x2.reshape(B, S, D)
```

```python
import functools
import math

import jax
import jax.numpy as jnp
import numpy as np
from jax import lax
from jax.experimental import pallas as pl
from jax.experimental.pallas import tpu as pltpu

F32 = jnp.float32
BF16 = jnp.bfloat16

D_MODEL = 1024
LRU_WIDTH = 256
LRU_BLOCKS = 4
CONV_WIDTH = 4
LRU_C = 8.0
HG_HEADS = 4
HG_DK = 64
HG_WIDTH = 256
MLA_HEADS = 8
MLA_NOPE = 64
MLA_ROPE = 32
MLA_V = 64
MLA_QK = MLA_NOPE + MLA_ROPE
MLA_Q_RANK = 256
MLA_KV_RANK = 128
MLA_WIDTH = MLA_HEADS * MLA_V
ROPE_THETA = 10000.0
D_FF = 2752
N_EXPERTS = 8
D_EXPERT = 3584
EPS = 1e-6

LANE = 128
SUBLANE = 8
HEAD_SLOT = 128
HG_CHUNK = 64
D_FF_PAD = 2816
MOE_ZERO_ROWS = 256
VMEM_LIMIT = 56 * 1024 * 1024

PA_W, PB_W, PC_W = 512, 1280, 512


def _cparams(sem):
  return pltpu.CompilerParams(dimension_semantics=sem, vmem_limit_bytes=VMEM_LIMIT)


def _sigmoid(x):
  return 1.0 / (1.0 + jnp.exp(-x))


def _silu(x):
  return x * _sigmoid(x)


def _rms(x, gain, n=None):
  n = x.shape[-1] if n is None else n
  ms = jnp.sum(x * x, axis=-1, keepdims=True) * (1.0 / n)
  return x * lax.rsqrt(ms + EPS) * gain


def _norm_inproj_kernel(x_ref, g_ref, w_ref, oa_ref, ob_ref, oc_ref):
  h = _rms(x_ref[...], g_ref[...]).astype(BF16)
  p = jnp.dot(h, w_ref[...], preferred_element_type=F32)
  oa_ref[...] = p[:, :PA_W]
  ob_ref[...] = p[:, PA_W:PA_W + PB_W]
  oc_ref[...] = p[:, PA_W + PB_W:]


def _norm_inproj(x2, gain, w, tm):
  T = x2.shape[0]
  n = w.shape[1]
  return pl.pallas_call(
      _norm_inproj_kernel,
      grid=(T // tm,),
      in_specs=[
          pl.BlockSpec((tm, D_MODEL), lambda i: (i, 0)),
          pl.BlockSpec((1, D_MODEL), lambda i: (0, 0)),
          pl.BlockSpec((D_MODEL, n), lambda i: (0, 0)),
      ],
      out_specs=[
          pl.BlockSpec((tm, PA_W), lambda i: (i, 0)),
          pl.BlockSpec((tm, PB_W), lambda i: (i, 0)),
          pl.BlockSpec((tm, PC_W), lambda i: (i, 0)),
      ],
      out_shape=[
          jax.ShapeDtypeStruct((T, PA_W), F32),
          jax.ShapeDtypeStruct((T, PB_W), F32),
          jax.ShapeDtypeStruct((T, PC_W), F32),
      ],
      compiler_params=_cparams(("parallel",)),
      name="norm_inproj",
  )(x2, gain, w)


def _rglru_kernel(pa_ref, cw_ref, cb_ref, wg_ref, bg_ref, lam_ref, og_ref, o_ref,
                  af_ref, bf_ref, ab_ref, bb_ref, hf_ref, hb_ref):
  S = pa_ref.shape[0]
  W = LRU_WIDTH
  xa = pa_ref[:, 0:W]
  row = lax.broadcasted_iota(jnp.int32, (S, W), 0)

  xc = cb_ref[...] + cw_ref[2:3, :] * xa
  for k in (0, 1, 3):
    d = k - CONV_WIDTH // 2
    shifted = pltpu.roll(xa, (-d) % S, 0)
    valid = (row >= -d) if d < 0 else (row < S - d)
    xc = xc + cw_ref[k:k + 1, :] * jnp.where(valid, shifted, 0.0)

  z = jnp.dot(xc.astype(BF16), wg_ref[...], preferred_element_type=F32) + bg_ref[...]
  lam = lam_ref[...]
  sp = jnp.maximum(-lam, 0.0) + jnp.log1p(jnp.exp(-jnp.abs(lam)))
  for d, (a_ref, b_ref) in enumerate(((af_ref, bf_ref), (ab_ref, bb_ref))):
    r = _sigmoid(z[:, d * 2 * W:d * 2 * W + W])
    i = _sigmoid(z[:, d * 2 * W + W:(d + 1) * 2 * W])
    log_a = (-LRU_C) * r * sp[d:d + 1, :]
    a = jnp.exp(log_a)
    a_ref[...] = a
    y2 = 2.0 * log_a
    u = a * a
    near = jnp.where(u == 1.0, y2, (u - 1.0) * y2 / jnp.log(u))
    em1 = jnp.where(y2 > -0.5, near, u - 1.0)
    b_ref[...] = jnp.sqrt(-em1) * (i * xc)

  n = S // SUBLANE
  r8 = lax.broadcasted_iota(jnp.int32, (SUBLANE, W), 0)

  def body(c, carry):
    cf, cb = carry
    sl = pl.ds(pl.multiple_of(c * SUBLANE, SUBLANE), SUBLANE)
    a = af_ref[sl, :]
    b = bf_ref[sl, :]
    for s in (1, 2, 4):
      m = r8 >= s
      b = jnp.where(m, a * pltpu.roll(b, s, 0) + b, b)
      a = jnp.where(m, a * pltpu.roll(a, s, 0), a)
    h = b + a * cf
    hf_ref[sl, :] = h
    cf = h[SUBLANE - 1:SUBLANE, :]

    sl = pl.ds(pl.multiple_of((n - 1 - c) * SUBLANE, SUBLANE), SUBLANE)
    a = ab_ref[sl, :]
    b = bb_ref[sl, :]
    for s in (1, 2, 4):
      m = r8 < SUBLANE - s
      b = jnp.where(m, a * pltpu.roll(b, SUBLANE - s, 0) + b, b)
      a = jnp.where(m, a * pltpu.roll(a, SUBLANE - s, 0), a)
    h = b + a * cb
    hb_ref[sl, :] = h
    cb = h[0:1, :]
    return cf, cb

  zero = jnp.zeros((1, W), F32)
  lax.fori_loop(0, n, body, (zero, zero), unroll=4)

  g = pa_ref[:, W:2 * W]
  gelu = 0.5 * g * (1.0 + jnp.tanh(math.sqrt(2.0 / math.pi) * (g + 0.044715 * (g * g * g))))
  y = (hf_ref[...] + hb_ref[...]) * gelu
  o_ref[...] = _rms(y, og_ref[...]).astype(o_ref.dtype)


def _rglru(pa, conv_w, conv_b, wg, bg, lam, out_g, B, S):
  T = B * S
  W = LRU_WIDTH
  const = lambda shape: pl.BlockSpec(shape, lambda b: (0,) * len(shape))
  return pl.pallas_call(
      _rglru_kernel,
      grid=(B,),
      in_specs=[
          pl.BlockSpec((S, PA_W), lambda b: (b, 0)),
          const((CONV_WIDTH, W)), const((1, W)), const((W, 4 * W)), const((1, 4 * W)),
          const((2, W)), const((1, W)),
      ],
      out_specs=pl.BlockSpec((S, W), lambda b: (b, 0)),
      out_shape=jax.ShapeDtypeStruct((T, W), BF16),
      scratch_shapes=[pltpu.VMEM((S, W), F32) for _ in range(6)],
      compiler_params=_cparams(("parallel",)),
      name="rglru",
  )(pa, conv_w, conv_b, wg, bg, lam, out_g)


_HG_LEVELS = (1, 2, 4, 8, 16, 32)
HG_SAFE_LOG = -80.0


def _bcast_row_in_blocks(x, block, row):
  L, C = x.shape
  x3 = x.reshape(L // block, block, C)
  return jnp.broadcast_to(x3[:, row:row + 1, :], x3.shape).reshape(L, C)


def _inc_prefix_products(f, row):
  L = f.shape[0]
  out = {1: f}
  p2 = f * jnp.where((row & 1) == 1, pltpu.roll(f, 1, 0), 1.0)
  out[2] = p2
  r4 = row & 3
  p4 = p2 * jnp.where(r4 == 2, pltpu.roll(p2, 1, 0), jnp.where(r4 == 3, pltpu.roll(p2, 2, 0), 1.0))
  out[4] = p4
  p, h = p4, 4
  while h < L:
    p = p * jnp.where((row & h) != 0, _bcast_row_in_blocks(p, 2 * h, h - 1), 1.0)
    h *= 2
    out[h] = p
  return out


def _inc_suffix_products(f, row):
  L = f.shape[0]
  up = lambda x, s: pltpu.roll(x, L - s, 0)
  out = {1: f}
  p2 = f * jnp.where((row & 1) == 0, up(f, 1), 1.0)
  out[2] = p2
  r4 = row & 3
  p4 = p2 * jnp.where(r4 == 1, up(p2, 1), jnp.where(r4 == 0, up(p2, 2), 1.0))
  out[4] = p4
  p, h = p4, 4
  while h < L:
    p = p * jnp.where((row & h) == 0, _bcast_row_in_blocks(p, 2 * h, h), 1.0)
    h *= 2
    out[h] = p
  return out


def _hg_chunk(q, f, v, st_ref, masks_ref, hmask, row, reverse):
  L = q.shape[0]
  k = 1.0 - f
  pre = _inc_prefix_products(f, row)
  suf = _inc_suffix_products(f, row)
  if not reverse:
    q_dec = pre
    k_dec = {h: (jnp.where((row & (h - 1)) == h - 1, 1.0, pltpu.roll(suf[h], L - 1, 0)) if h > 1 else None)
             for h in suf}
  else:
    q_dec = suf
    k_dec = {h: (jnp.where((row & (h - 1)) == 0, 1.0, pltpu.roll(pre[h], 1, 0)) if h > 1 else None)
             for h in pre}

  def head_blockdiag(x):
    return jnp.concatenate([x] * HG_HEADS, axis=0) * hmask

  nt = (((1,), (1,)), ((), ()))
  kb = k.astype(BF16)
  p = masks_ref[0] * lax.dot_general(q.astype(BF16), head_blockdiag(kb), nt, preferred_element_type=F32)
  for li, h in enumerate(_HG_LEVELS):
    qh = (q * q_dec[h]).astype(BF16)
    kh = kb if h == 1 else (k * k_dec[h]).astype(BF16)
    p = p + masks_ref[li + 1] * lax.dot_general(qh, head_blockdiag(kh), nt, preferred_element_type=F32)

  vb = v.astype(BF16)
  o = jnp.dot(p.astype(BF16), head_blockdiag(vb), preferred_element_type=F32)
  st = st_ref[...]
  o = o + lax.dot_general((q * q_dec[L]).astype(BF16), st.astype(BF16), nt, preferred_element_type=F32)

  ke = (k * k_dec[L]).astype(BF16)
  end = q_dec[L][0:1, :] if reverse else q_dec[L][L - 1:L, :]
  tn = (((0,), (0,)), ((), ()))
  upd = lax.dot_general(vb, ke, tn, preferred_element_type=F32)
  st_ref[...] = st * end + upd * hmask.astype(F32)
  return o


def _hg_chunk_fast(q, f, v, st_ref, masks_ref, hmask, row, reverse):
  L = q.shape[0]
  a = (_inc_suffix_products(f, row) if reverse else _inc_prefix_products(f, row))[L]
  qt = (q * a).astype(BF16)
  kt = ((1.0 - f) * (1.0 / a)).astype(BF16)

  def head_blockdiag(x):
    return jnp.concatenate([x] * HG_HEADS, axis=0) * hmask

  nt = (((1,), (1,)), ((), ()))
  tn = (((0,), (0,)), ((), ()))
  p = jnp.where(masks_ref[len(_HG_LEVELS) + 1] != 0.0,
                lax.dot_general(qt, head_blockdiag(kt), nt, preferred_element_type=F32), 0.0)
  vb = v.astype(BF16)
  st = st_ref[...]
  o = jnp.dot(p.astype(BF16), head_blockdiag(vb), preferred_element_type=F32)
  o = o + lax.dot_general(qt, st.astype(BF16), nt, preferred_element_type=F32)
  end = a[0:1, :] if reverse else a[L - 1:L, :]
  upd = lax.dot_general(vb, kt, tn, preferred_element_type=F32)
  st_ref[...] = (st + upd * hmask.astype(F32)) * end
  return o


def _hgrn2_kernel(pb_ref, lb_ref, ng_ref, mf_ref, mb_ref, hm_ref, ones_ref, o_ref,
                  q_ref, ff_ref, fb_ref, of_ref, ob_ref, sf_ref, sb_ref):
  S = pb_ref.shape[0]
  W = HG_WIDTH
  L = HG_CHUNK
  n = S // L
  row = lax.broadcasted_iota(jnp.int32, (L, W), 0)
  hmask = hm_ref[...]
  sf_ref[...] = jnp.zeros_like(sf_ref)
  sb_ref[...] = jnp.zeros_like(sb_ref)

  def full_reduce(x, op):
    return op(op(x, axis=1, keepdims=True), axis=0, keepdims=True)

  q = _silu(pb_ref[:, 0:W]) * (HG_DK ** -0.5)
  q_ref[...] = q
  margin = (jnp.log(1.0 + full_reduce(jnp.abs(q), jnp.max))
            + jnp.log(1.0 + full_reduce(jnp.abs(pb_ref[:, 3 * W:4 * W]), jnp.max)))
  log_min = None
  for d, f_ref in enumerate((ff_ref, fb_ref)):
    lb = lb_ref[d:d + 1, :]
    f = lb + (1.0 - lb) * _sigmoid(pb_ref[:, (d + 1) * W:(d + 2) * W])
    f_ref[...] = f
    chunk_log = jnp.sum(jnp.log(f).reshape(n, L, W), axis=1)
    m = full_reduce(chunk_log, jnp.min)
    log_min = m if log_min is None else jnp.minimum(log_min, m)
  in_range = (log_min - margin)[0, 0] >= HG_SAFE_LOG

  def make_body(chunk_fn):
    def body(c, _):
      for f_ref, out_ref, st_ref, m_ref, rev, cc in ((ff_ref, of_ref, sf_ref, mf_ref, False, c),
                                                     (fb_ref, ob_ref, sb_ref, mb_ref, True, n - 1 - c)):
        sl = pl.ds(pl.multiple_of(cc * L, L), L)
        out_ref[sl, :] = chunk_fn(q_ref[sl, :], f_ref[sl, :], pb_ref[sl, 3 * W:4 * W], st_ref, m_ref,
                                  hmask, row, rev)
      return 0
    return body

  @pl.when(in_range)
  def _():
    lax.fori_loop(0, n, make_body(_hg_chunk_fast), 0)

  @pl.when(jnp.logical_not(in_range))
  def _():
    lax.fori_loop(0, n, make_body(_hg_chunk), 0)

  o = of_ref[...] + ob_ref[...]
  ms = jnp.dot((o * o).astype(BF16), ones_ref[...], preferred_element_type=F32) * (1.0 / HG_DK)
  y = o * lax.rsqrt(ms + EPS) * ng_ref[...]
  o_ref[...] = (y * _silu(pb_ref[:, 4 * W:5 * W])).astype(o_ref.dtype)


def _hg_constants():
  L = HG_CHUNK
  t = jnp.arange(L)[:, None]
  s = jnp.arange(L)[None, :]
  fwd, bwd = [t == s], [t == s]
  for h in _HG_LEVELS:
    same = (t // (2 * h)) == (s // (2 * h))
    fwd.append(same & ((t & h) != 0) & ((s & h) == 0))
    bwd.append(same & ((t & h) == 0) & ((s & h) != 0))
  fwd.append(t >= s)
  bwd.append(t <= s)
  tile = lambda m: jnp.tile(jnp.stack(m).astype(F32), (1, 1, HG_HEADS))
  hid = jnp.arange(HG_WIDTH) // HG_DK
  hmask = (hid[:, None] == hid[None, :])
  return tile(fwd), tile(bwd), hmask.astype(BF16), hmask.astype(BF16)


def _hgrn2(pb, lb, norm_g, B, S):
  T = B * S
  W = HG_WIDTH
  mf, mb, hmask, ones = _hg_constants()
  const = lambda shape: pl.BlockSpec(shape, lambda b: (0,) * len(shape))
  return pl.pallas_call(
      _hgrn2_kernel,
      grid=(B,),
      in_specs=[
          pl.BlockSpec((S, PB_W), lambda b: (b, 0)),
          const((2, W)), const((1, W)), const(mf.shape), const(mb.shape), const((W, W)), const((W, W)),
      ],
      out_specs=pl.BlockSpec((S, W), lambda b: (b, 0)),
      out_shape=jax.ShapeDtypeStruct((T, W), BF16),
      scratch_shapes=[pltpu.VMEM((S, W), F32) for _ in range(5)]
                     + [pltpu.VMEM((W, W), F32), pltpu.VMEM((W, W), F32)],
      compiler_params=_cparams(("parallel",)),
      name="hgrn2",
  )(pb, lb, norm_g, mf, mb, hmask, ones)


ROPE_HALF = MLA_ROPE // 2
ATTN_MAX_BOUND = 60.0
ATTN_KEY_BLOCK = 256
ATTN_Q_BLOCK = 1024


def _rope_tables_kernel(pos_ref, invf_ref, sgn_ref, cos_ref, sin_ref):
  ang = pos_ref[...].astype(F32) * invf_ref[...]
  cos_ref[...] = jnp.cos(ang)
  sin_ref[...] = sgn_ref[...] * jnp.sin(ang)


def _rope_tables(pos2, tm):
  T = pos2.shape[0]
  inv_freq = (ROPE_THETA ** (-jnp.arange(0, ROPE_HALF, dtype=F32) * (2.0 / MLA_ROPE))).reshape(1, ROPE_HALF)
  ones = jnp.ones((1, ROPE_HALF), F32)
  no_nope = jnp.zeros((1, MLA_NOPE), F32)
  invf = _slot_lanes(no_nope, jnp.concatenate([inv_freq, inv_freq], axis=1))
  sgn = _slot_lanes(no_nope, jnp.concatenate([-ones, ones], axis=1))
  row = pl.BlockSpec((1, HEAD_SLOT), lambda i: (0, 0))
  return pl.pallas_call(
      _rope_tables_kernel,
      grid=(T // tm,),
      in_specs=[pl.BlockSpec((tm, 1), lambda i: (i, 0)), row, row],
      out_specs=[pl.BlockSpec((tm, HEAD_SLOT), lambda i: (i, 0))] * 2,
      out_shape=[jax.ShapeDtypeStruct((T, HEAD_SLOT), F32)] * 2,
      compiler_params=_cparams(("parallel",)),
      name="rope_tables",
  )(pos2, invf, sgn)


def _rope(t, cos_t, sin_t):
  return t * cos_t + pltpu.roll(t, HEAD_SLOT // 2, 1) * sin_t


def _mla_prep_kernel(pc_ref, cos_ref, sin_ref, qn_ref, kvn_ref, wq_ref, wk_ref, wv_ref, vone_ref, gq_ref, gk_ref,
                     q_ref, k_ref, v_ref):
  cos_t = cos_ref[...]
  sin_t = sin_ref[...]
  cq = _rms(pc_ref[:, 0:MLA_Q_RANK], qn_ref[...]).astype(BF16)
  ckv = _rms(pc_ref[:, MLA_Q_RANK:MLA_Q_RANK + MLA_KV_RANK], kvn_ref[...]).astype(BF16)
  kr_slot = pc_ref[:, MLA_Q_RANK + MLA_KV_RANK:]
  q_all = jnp.dot(cq, wq_ref[...], preferred_element_type=F32)
  k_all = jnp.dot(ckv, wk_ref[...], preferred_element_type=F32)
  v_ref[...] = (jnp.dot(ckv, wv_ref[...], preferred_element_type=F32) + vone_ref[...]).astype(v_ref.dtype)
  scale = MLA_QK ** -0.5 * math.log2(math.e)
  for h in range(MLA_HEADS):
    sl = slice(h * HEAD_SLOT, (h + 1) * HEAD_SLOT)
    qh = _rope(_rms(q_all[:, sl], gq_ref[...], MLA_QK), cos_t, sin_t)
    q_ref[h] = (qh * scale).astype(q_ref.dtype)
    kh = _rope(_rms(k_all[:, sl] + kr_slot, gk_ref[...], MLA_QK), cos_t, sin_t)
    k_ref[h] = kh.astype(k_ref.dtype)


def _mla_prep(pc, cos_t, sin_t, q_norm, kv_norm, wq, wk, wv, vone, gq, gk, B, S, ts):
  H = MLA_HEADS
  nb = S // ts
  const = lambda shape: pl.BlockSpec(shape, lambda b, s: (0,) * len(shape))
  tok = lambda w: pl.BlockSpec((ts, w), lambda b, s: (b * nb + s, 0))
  return pl.pallas_call(
      _mla_prep_kernel,
      grid=(B, nb),
      in_specs=[
          tok(PC_W), tok(HEAD_SLOT), tok(HEAD_SLOT),
          const((1, MLA_Q_RANK)), const((1, MLA_KV_RANK)),
          const(wq.shape), const(wk.shape), const(wv.shape), const(vone.shape),
          const((1, HEAD_SLOT)), const((1, HEAD_SLOT)),
      ],
      out_specs=[
          pl.BlockSpec((None, H, ts, HEAD_SLOT), lambda b, s: (b, 0, s, 0)),
          pl.BlockSpec((None, H, ts, HEAD_SLOT), lambda b, s: (b, 0, s, 0)),
          pl.BlockSpec((ts, H * HEAD_SLOT), lambda b, s: (b * nb + s, 0)),
      ],
      out_shape=[
          jax.ShapeDtypeStruct((B, H, S, HEAD_SLOT), BF16),
          jax.ShapeDtypeStruct((B, H, S, HEAD_SLOT), BF16),
          jax.ShapeDtypeStruct((B * S, H * HEAD_SLOT), BF16),
      ],
      compiler_params=_cparams(("parallel", "parallel")),
      name="mla_prep",
  )(pc, cos_t, sin_t, q_norm, kv_norm, wq, wk, wv, vone, gq, gk)


def _attention_kernel(bound_ref, q_ref, k_ref, v_ref, o_ref):
  tq = q_ref.shape[1]
  S = k_ref.shape[1]
  bound = bound_ref[0]
  nt = (((1,), (1,)), ((), ()))

  def finish(res):
    lane = lax.broadcasted_iota(jnp.int32, (tq, HEAD_SLOT), 1)
    even = res[0] * (1.0 / res[0][:, MLA_V:MLA_V + 1])
    odd = res[1] * (1.0 / res[1][:, 0:1])
    o_ref[...] = jnp.where(lane < MLA_V, even, odd).astype(o_ref.dtype)

  @pl.when(bound <= ATTN_MAX_BOUND)
  def _():
    res = []
    for h in range(2):
      acc = None
      for j in range(S // ATTN_KEY_BLOCK):
        ks = slice(j * ATTN_KEY_BLOCK, (j + 1) * ATTN_KEY_BLOCK)
        s = lax.dot_general(q_ref[h], k_ref[h, ks, :], nt, preferred_element_type=F32)
        p = jnp.exp2(s - bound).astype(BF16)
        c = jnp.dot(p, v_ref[ks, h * HEAD_SLOT:(h + 1) * HEAD_SLOT], preferred_element_type=F32)
        acc = c if acc is None else acc + c
      res.append(acc)
    finish(res)

  @pl.when(bound > ATTN_MAX_BOUND)
  def _():
    res = []
    for h in range(2):
      s = lax.dot_general(q_ref[h], k_ref[h], nt, preferred_element_type=F32)
      p = jnp.exp2(s - jnp.max(s, axis=-1, keepdims=True)).astype(BF16)
      res.append(jnp.dot(p, v_ref[:, h * HEAD_SLOT:(h + 1) * HEAD_SLOT], preferred_element_type=F32))
    finish(res)


def _attention(bound, q, k, v, B, S, tq):
  H = MLA_HEADS
  nq = S // tq
  return pl.pallas_call(
      _attention_kernel,
      grid=(B, H // 2, nq),
      in_specs=[
          pl.BlockSpec(memory_space=pltpu.SMEM),
          pl.BlockSpec((None, 2, tq, HEAD_SLOT), lambda b, hp, i: (b, hp, i, 0)),
          pl.BlockSpec((None, 2, S, HEAD_SLOT), lambda b, hp, i: (b, hp, 0, 0)),
          pl.BlockSpec((S, 2 * HEAD_SLOT), lambda b, hp, i: (b, hp)),
      ],
      out_specs=pl.BlockSpec((tq, 2 * MLA_V), lambda b, hp, i: (b * nq + i, hp)),
      out_shape=jax.ShapeDtypeStruct((B * S, MLA_WIDTH), F32),
      compiler_params=_cparams(("parallel", "parallel", "parallel")),
      name="mla_attention",
  )(bound, q, k, v)


def _outproj_kernel(x_ref, ya_ref, yb_ref, yc_ref, gc_ref, w_ref, o_ref):
  yc = _rms(yc_ref[...], gc_ref[...]).astype(BF16)
  a0, a1 = LRU_WIDTH, LRU_WIDTH + HG_WIDTH
  acc = jnp.dot(ya_ref[...], w_ref[0:a0, :], preferred_element_type=F32)
  acc = acc + jnp.dot(yb_ref[...], w_ref[a0:a1, :], preferred_element_type=F32)
  acc = acc + jnp.dot(yc, w_ref[a1:, :], preferred_element_type=F32)
  o_ref[...] = x_ref[...] + acc


def _outproj(x2, ya, yb, yc, gc, w, tm):
  T = x2.shape[0]
  tok = lambda wd: pl.BlockSpec((tm, wd), lambda i: (i, 0))
  return pl.pallas_call(
      _outproj_kernel,
      grid=(T // tm,),
      in_specs=[tok(D_MODEL), tok(LRU_WIDTH), tok(HG_WIDTH), tok(MLA_WIDTH),
                pl.BlockSpec((1, MLA_WIDTH), lambda i: (0, 0)),
                pl.BlockSpec((D_MODEL, D_MODEL), lambda i: (0, 0))],
      out_specs=tok(D_MODEL),
      out_shape=jax.ShapeDtypeStruct((T, D_MODEL), F32),
      compiler_params=_cparams(("parallel",)),
      name="outproj",
  )(x2, ya, yb, yc, gc, w)


def _ffn_kernel(x_ref, g_ref, wg_ref, wu_ref, wd_ref, o_ref, h_ref, acc_ref):
  f = pl.program_id(1)

  @pl.when(f == 0)
  def _():
    h_ref[...] = _rms(x_ref[...], g_ref[...]).astype(BF16)
    acc_ref[...] = jnp.zeros_like(acc_ref)

  h = h_ref[...]
  g = jnp.dot(h, wg_ref[...], preferred_element_type=F32)
  u = jnp.dot(h, wu_ref[...], preferred_element_type=F32)
  a = (_silu(g) * u).astype(BF16)
  acc_ref[...] += jnp.dot(a, wd_ref[...], preferred_element_type=F32)

  @pl.when(f == pl.num_programs(1) - 1)
  def _():
    o_ref[...] = x_ref[...] + acc_ref[...]


def _ffn(x2, gain, wg, wu, wd, tm, tf):
  T = x2.shape[0]
  F = wg.shape[1]
  return pl.pallas_call(
      _ffn_kernel,
      grid=(T // tm, F // tf),
      in_specs=[
          pl.BlockSpec((tm, D_MODEL), lambda i, f: (i, 0)),
          pl.BlockSpec((1, D_MODEL), lambda i, f: (0, 0)),
          pl.BlockSpec((D_MODEL, tf), lambda i, f: (0, f)),
          pl.BlockSpec((D_MODEL, tf), lambda i, f: (0, f)),
          pl.BlockSpec((tf, D_MODEL), lambda i, f: (f, 0)),
      ],
      out_specs=pl.BlockSpec((tm, D_MODEL), lambda i, f: (i, 0)),
      out_shape=jax.ShapeDtypeStruct((T, D_MODEL), F32),
      scratch_shapes=[pltpu.VMEM((tm, D_MODEL), BF16), pltpu.VMEM((tm, D_MODEL), F32)],
      compiler_params=_cparams(("parallel", "arbitrary")),
      name="ffn_dense",
  )(x2, gain, wg, wu, wd)


def _router_kernel(x_ref, g_ref, wr_ref, sel_ref):
  h = _rms(x_ref[...], g_ref[...])
  logits = jnp.dot(h, wr_ref[...], preferred_element_type=F32, precision=lax.Precision.HIGHEST)
  lane = lax.broadcasted_iota(jnp.int32, logits.shape, 1)
  neg = jnp.float32(-jnp.inf)
  lg = jnp.where(lane < N_EXPERTS, logits, neg)
  m1 = jnp.max(lg, axis=-1, keepdims=True)
  i1 = jnp.min(jnp.where(lg == m1, lane, LANE), axis=-1, keepdims=True)
  lg2 = jnp.where(lane == i1, neg, lg)
  m2 = jnp.max(lg2, axis=-1, keepdims=True)
  i2 = jnp.min(jnp.where(lg2 == m2, lane, LANE), axis=-1, keepdims=True)
  e2 = jnp.exp(m2 - m1)
  w1 = 1.0 / (1.0 + e2)
  w2 = e2 / (1.0 + e2)
  sel_ref[...] = jnp.where(lane == 0, i1.astype(F32), jnp.where(lane == 1, i2.astype(F32),
                           jnp.where(lane == 2, w1, jnp.where(lane == 3, w2, 0.0))))


def _router(x2, gain, wr, tm):
  T = x2.shape[0]
  return pl.pallas_call(
      _router_kernel,
      grid=(T // tm,),
      in_specs=[pl.BlockSpec((tm, D_MODEL), lambda i: (i, 0)),
                pl.BlockSpec((1, D_MODEL), lambda i: (0, 0)),
                pl.BlockSpec((D_MODEL, LANE), lambda i: (0, 0))],
      out_specs=pl.BlockSpec((tm, LANE), lambda i: (i, 0)),
      out_shape=jax.ShapeDtypeStruct((T, LANE), F32),
      compiler_params=_cparams(("parallel",)),
      name="moe_router",
  )(x2, gain, wr)


def _route_plan(sel, tm):
  T = sel.shape[0]
  e_flat = sel[:, 0:2].astype(jnp.int32).reshape(1, 2 * T)
  onehot = (e_flat == jnp.arange(N_EXPERTS, dtype=jnp.int32)[:, None]).astype(jnp.int32)
  csum = jnp.cumsum(onehot, axis=1)
  counts = csum[:, -1]
  tiles_e = (counts + tm - 1) // tm
  tile_end = jnp.cumsum(tiles_e)
  offs = (tile_end - tiles_e) * tm
  pos = jnp.sum(onehot * (csum - 1 + offs[:, None]), axis=0)
  n_tiles = (2 * T) // tm + N_EXPERTS
  tile_expert = jnp.sum(jnp.arange(n_tiles, dtype=jnp.int32)[:, None] >= tile_end[None, :], axis=1)
  tile_expert = jnp.minimum(tile_expert, N_EXPERTS - 1).astype(jnp.int32)
  last_tile = jnp.where(tiles_e > 0, offs + (tiles_e - 1) * tm, -1)
  spare = tile_end[-1] + jnp.arange(N_EXPERTS, dtype=jnp.int32)
  spare_tile = jnp.where(spare < n_tiles, spare * tm, -1)
  clear = jnp.concatenate([last_tile, spare_tile]).astype(jnp.int32)
  return pos.astype(jnp.int32), tile_expert, tile_end[-1:].astype(jnp.int32), clear, n_tiles


def _row_copy(src, dst, sem, rows=1, src_row=0, dst_row=0):
  return pltpu.make_async_copy(src.at[pl.ds(src_row, rows), :], dst.at[pl.ds(dst_row, rows), :], sem)


def _dispatch_kernel(pos_ref, lt_ref, x_ref, xs_hbm, zbuf, sem, zsem, *, tm):
  tt = x_ref.shape[0]
  zr = zbuf.shape[0]

  @pl.when(pl.program_id(0) == 0)
  def _():
    zbuf[...] = jnp.zeros_like(zbuf)
    for e in range(lt_ref.shape[0]):
      @pl.when(lt_ref[e] >= 0)
      def _():
        rows = [pl.multiple_of(lt_ref[e] + c * zr, SUBLANE) for c in range(tm // zr)]
        for r0 in rows:
          _row_copy(zbuf, xs_hbm, zsem, zr, 0, r0).start()
        for r0 in rows:
          _row_copy(zbuf, xs_hbm, zsem, zr, 0, r0).wait()

  def body(r, _):
    for c in range(2):
      _row_copy(x_ref, xs_hbm, sem, 1, r, pos_ref[0, 0, 2 * r + c]).start()
    return 0

  lax.fori_loop(0, tt, body, 0, unroll=8)
  for c in range(2):
    _row_copy(x_ref, xs_hbm, sem, tt).wait()


def _dispatch(x2, pos, last_tile, n_rows, tt, tm):
  T = x2.shape[0]
  pos3 = pos.reshape(T // tt, 1, 2 * tt)
  zr = min(MOE_ZERO_ROWS, tm)
  return pl.pallas_call(
      functools.partial(_dispatch_kernel, tm=tm),
      grid=(T // tt,),
      in_specs=[pl.BlockSpec((1, 1, 2 * tt), lambda i: (i, 0, 0), memory_space=pltpu.SMEM),
                pl.BlockSpec(memory_space=pltpu.SMEM),
                pl.BlockSpec((tt, D_MODEL), lambda i: (i, 0))],
      out_specs=pl.BlockSpec(memory_space=pl.ANY),
      out_shape=jax.ShapeDtypeStruct((n_rows, D_MODEL), F32),
      scratch_shapes=[pltpu.VMEM((zr, D_MODEL), F32), pltpu.SemaphoreType.DMA(()), pltpu.SemaphoreType.DMA(())],
      compiler_params=_cparams(("arbitrary",)),
      name="moe_dispatch",
  )(pos3, last_tile, x2)


def _moe_group_kernel(te_ref, nu_ref, xs_ref, g_ref, w1_ref, w3_ref, w2_ref, ys_ref, h_ref, acc_ref):
  del te_ref
  i = pl.program_id(0)
  f = pl.program_id(1)
  last = pl.num_programs(1) - 1
  used = i < nu_ref[0]

  @pl.when(used & (f == 0))
  def _():
    h_ref[...] = _rms(xs_ref[...], g_ref[...]).astype(BF16)
    acc_ref[...] = jnp.zeros_like(acc_ref)

  @pl.when(used)
  def _():
    h = h_ref[...]
    g = jnp.dot(h, w1_ref[...], preferred_element_type=F32)
    u = jnp.dot(h, w3_ref[...], preferred_element_type=F32)
    a = (_silu(g) * u).astype(BF16)
    acc_ref[...] += jnp.dot(a, w2_ref[...], preferred_element_type=F32)

  @pl.when(used & (f == last))
  def _():
    ys_ref[...] = acc_ref[...]

  @pl.when(jnp.logical_not(used) & (f == last))
  def _():
    ys_ref[...] = jnp.zeros_like(ys_ref)


def _moe_group(xs, gain, tile_expert, n_used, w1, w3, w2, tm, tf):
  P = xs.shape[0]
  F = w1.shape[2]
  nf = F // tf
  fidx = lambda i, f, te, nu: jnp.where(i < nu[0], f, nf - 1)
  grid_spec = pltpu.PrefetchScalarGridSpec(
      num_scalar_prefetch=2,
      grid=(P // tm, nf),
      in_specs=[
          pl.BlockSpec((tm, D_MODEL), lambda i, f, te, nu: (jnp.minimum(i, nu[0] - 1), 0)),
          pl.BlockSpec((1, D_MODEL), lambda i, f, te, nu: (0, 0)),
          pl.BlockSpec((None, D_MODEL, tf), lambda i, f, te, nu: (te[i], 0, fidx(i, f, te, nu))),
          pl.BlockSpec((None, D_MODEL, tf), lambda i, f, te, nu: (te[i], 0, fidx(i, f, te, nu))),
          pl.BlockSpec((None, tf, D_MODEL), lambda i, f, te, nu: (te[i], fidx(i, f, te, nu), 0)),
      ],
      out_specs=pl.BlockSpec((tm, D_MODEL), lambda i, f, te, nu: (i, 0)),
      scratch_shapes=[pltpu.VMEM((tm, D_MODEL), BF16), pltpu.VMEM((tm, D_MODEL), F32)],
  )
  return pl.pallas_call(
      _moe_group_kernel,
      grid_spec=grid_spec,
      out_shape=jax.ShapeDtypeStruct((P, D_MODEL), F32),
      compiler_params=_cparams(("arbitrary", "arbitrary")),
      name="moe_group",
  )(tile_expert, n_used, xs, gain, w1, w3, w2)


def _combine_kernel(pos_ref, x_ref, sel_ref, ys_hbm, o_ref, ybuf, sem):
  tt = x_ref.shape[0]

  def body(r, _):
    for c in range(2):
      _row_copy(ys_hbm, ybuf.at[c], sem, 1, pos_ref[0, 0, 2 * r + c], r).start()
    return 0

  lax.fori_loop(0, tt, body, 0, unroll=8)
  for c in range(2):
    _row_copy(ys_hbm, ybuf.at[c], sem, tt).wait()
  sel = sel_ref[...]
  o_ref[...] = x_ref[...] + sel[:, 2:3] * ybuf[0] + sel[:, 3:4] * ybuf[1]


def _combine(x2, sel, pos, ys, tt):
  T = x2.shape[0]
  pos3 = pos.reshape(T // tt, 1, 2 * tt)
  return pl.pallas_call(
      _combine_kernel,
      grid=(T // tt,),
      in_specs=[pl.BlockSpec((1, 1, 2 * tt), lambda i: (i, 0, 0), memory_space=pltpu.SMEM),
                pl.BlockSpec((tt, D_MODEL), lambda i: (i, 0)),
                pl.BlockSpec((tt, LANE), lambda i: (i, 0)),
                pl.BlockSpec(memory_space=pl.ANY)],
      out_specs=pl.BlockSpec((tt, D_MODEL), lambda i: (i, 0)),
      out_shape=jax.ShapeDtypeStruct((T, D_MODEL), F32),
      scratch_shapes=[pltpu.VMEM((2, tt, D_MODEL), F32), pltpu.SemaphoreType.DMA(())],
      compiler_params=_cparams(("arbitrary",)),
      name="moe_combine",
  )(pos3, x2, sel, ys)


def _block_diag(w):
  n, d, e = w.shape
  eye = jnp.eye(n, dtype=w.dtype)
  return (eye[:, None, :, None] * w[:, :, None, :]).reshape(n * d, n * e)


def _slot_lanes(nope, rope):
  z = lambda n: jnp.zeros(nope.shape[:-1] + (n,), nope.dtype)
  cut = HEAD_SLOT // 2 - ROPE_HALF
  x1, x2 = (z(ROPE_HALF), z(ROPE_HALF)) if rope is None else (rope[..., :ROPE_HALF], rope[..., ROPE_HALF:])
  return jnp.concatenate([x1, nope[..., :cut], x2, nope[..., cut:], z(HEAD_SLOT - MLA_QK)], axis=-1)


def _head_slots(w, width):
  r = w.shape[0]
  w3 = w.reshape(r, MLA_HEADS, width)
  rope = w3[..., MLA_NOPE:] if width == MLA_QK else None
  return _slot_lanes(w3[..., :MLA_NOPE], rope).reshape(r, MLA_HEADS * HEAD_SLOT)


def _slot_gain(g):
  g = g.astype(F32).reshape(1, MLA_QK)
  return _slot_lanes(g[:, :MLA_NOPE], g[:, MLA_NOPE:])


def _value_slots(wv):
  r = wv.shape[0]
  w4 = wv.reshape(r, MLA_HEADS // 2, 2, MLA_V)
  z = jnp.zeros((r, MLA_HEADS // 2, HEAD_SLOT - MLA_V), wv.dtype)
  out = jnp.stack([jnp.concatenate([w4[:, :, 0], z], axis=-1), jnp.concatenate([z, w4[:, :, 1]], axis=-1)], axis=2)
  one = np.zeros((1, MLA_HEADS // 2, 2, HEAD_SLOT), np.float32)
  one[:, :, 0, MLA_V] = 1.0
  one[:, :, 1, 0] = 1.0
  return out.reshape(r, MLA_HEADS * HEAD_SLOT), jnp.asarray(one.reshape(1, MLA_HEADS * HEAD_SLOT))


def _score_bound(gq, gk):
  b = MLA_QK ** 0.5 * math.log2(math.e) * 1.02 * jnp.max(jnp.abs(gq)) * jnp.max(jnp.abs(gk))
  return b.astype(F32).reshape(1)


def kernel(x, positions, norm_mix, w_in, conv_w, conv_b, lru_wa, lru_ba, lru_wx, lru_bx, lru_lam,
           out_g_a, hg_lb_logits, hg_norm_g, mla_q_norm, mla_w_uq, mla_kv_norm, mla_w_ukv,
           qk_norm_q, qk_norm_k, out_g_c, w_out, norm_ffn, ffn_w_gate_up, ffn_w_down,
           moe_router, moe_w1, moe_w3, moe_w2):
  B, S, D = x.shape
  T = B * S
  depth = w_in.shape[0]
  tm = 512
  row = lambda v: v.astype(F32).reshape(1, -1)

  p = jax.nn.softmax(hg_lb_logits.astype(F32), axis=0)
  lower_bounds = jnp.cumsum(p, axis=0) - p[0:1]

  x2 = x.reshape(T, D)
  cos_t, sin_t = _rope_tables(positions.reshape(T, 1), tm)

  for l in range(depth):
    n_main = w_in.shape[2] - MLA_ROPE
    kr_slot_w = _slot_lanes(jnp.zeros((D, MLA_NOPE), F32), w_in[l][:, n_main:])
    w_in_l = jnp.concatenate([w_in[l][:, :n_main], kr_slot_w], axis=1).astype(BF16)
    wg_lru = jnp.concatenate(
        [_block_diag(lru_wa[l, 0]), _block_diag(lru_wx[l, 0]),
         _block_diag(lru_wa[l, 1]), _block_diag(lru_wx[l, 1])], axis=1).astype(BF16)
    bg_lru = jnp.concatenate([lru_ba[l, 0], lru_bx[l, 0], lru_ba[l, 1], lru_bx[l, 1]]).reshape(1, -1)
    wq = _head_slots(mla_w_uq[l], MLA_QK).astype(BF16)
    wkv = mla_w_ukv[l].reshape(MLA_KV_RANK, MLA_HEADS, MLA_NOPE + MLA_V)
    wk = _head_slots(wkv[:, :, :MLA_NOPE].reshape(MLA_KV_RANK, -1), MLA_NOPE).astype(BF16)
    wv, vone = _value_slots(wkv[:, :, MLA_NOPE:].reshape(MLA_KV_RANK, MLA_WIDTH))
    wv = wv.astype(BF16)

    pa, pb, pc = _norm_inproj(x2, row(norm_mix[l]), w_in_l, tm)
    ya = _rglru(pa, conv_w[l], row(conv_b[l]), wg_lru, bg_lru, lru_lam[l], row(out_g_a[l]), B, S)
    yb = _hgrn2(pb, lower_bounds[l], row(jnp.tile(hg_norm_g[l], HG_HEADS)), B, S)
    q, k, v = _mla_prep(pc, cos_t, sin_t, row(mla_q_norm[l]), row(mla_kv_norm[l]), wq, wk, wv, vone,
                        _slot_gain(qk_norm_q[l]), _slot_gain(qk_norm_k[l]), B, S, min(tm, S))
    yc = _attention(_score_bound(qk_norm_q[l], qk_norm_k[l]), q, k, v, B, S, min(ATTN_Q_BLOCK, S))
    x2 = _outproj(x2, ya, yb, yc, row(out_g_c[l]), w_out[l].astype(BF16), tm)

    if l % 2 == 0:
      wgu = ffn_w_gate_up[l // 2]
      pad = D_FF_PAD - D_FF
      wg = jnp.pad(wgu[:, :D_FF], ((0, 0), (0, pad))).astype(BF16)
      wu = jnp.pad(wgu[:, D_FF:], ((0, 0), (0, pad))).astype(BF16)
      wd = jnp.pad(ffn_w_down[l // 2], ((0, pad), (0, 0))).astype(BF16)
      x2 = _ffn(x2, row(norm_ffn[l]), wg, wu, wd, tm, D_FF_PAD // 2)
    else:
      wr = jnp.pad(moe_router[l // 2], ((0, 0), (0, LANE - N_EXPERTS)))
      gain = row(norm_ffn[l])
      tme = 1024 if T >= 8192 else 128
      sel = _router(x2, gain, wr, tm)
      pos, tile_expert, n_used, last_tile, n_tiles = _route_plan(sel, tme)
      xs = _dispatch(x2, pos, last_tile, n_tiles * tme, tm, tme)
      ys = _moe_group(xs, gain, tile_expert, n_used, moe_w1[l // 2].astype(BF16), moe_w3[l // 2].astype(BF16),
                      moe_w2[l // 2].astype(BF16), tme, D_EXPERT // 7)
      x2 = _combine(x2, sel, pos, ys, tm)
  return x2.reshape(B, S, D)
```

```python
import functools
import math

import jax
import jax.numpy as jnp
import numpy as np
from jax import lax
from jax.experimental import pallas as pl
from jax.experimental.pallas import tpu as pltpu

F32 = jnp.float32
BF16 = jnp.bfloat16

D_MODEL = 1024
LRU_WIDTH = 256
LRU_BLOCKS = 4
CONV_WIDTH = 4
LRU_C = 8.0
HG_HEADS = 4
HG_DK = 64
HG_WIDTH = 256
MLA_HEADS = 8
MLA_NOPE = 64
MLA_ROPE = 32
MLA_V = 64
MLA_QK = MLA_NOPE + MLA_ROPE
MLA_Q_RANK = 256
MLA_KV_RANK = 128
MLA_WIDTH = MLA_HEADS * MLA_V
ROPE_THETA = 10000.0
D_FF = 2752
N_EXPERTS = 8
D_EXPERT = 3584
EPS = 1e-6

LANE = 128
SUBLANE = 8
HEAD_SLOT = 128
HG_CHUNK = 64
D_FF_PAD = 2816
MOE_ZERO_ROWS = 256
FFN_CHUNK = 1024
VMEM_LIMIT = 56 * 1024 * 1024

PA_W, PB_W, PC_W = 512, 1280, 512


def _cparams(sem):
  return pltpu.CompilerParams(dimension_semantics=sem, vmem_limit_bytes=VMEM_LIMIT)


def _sigmoid(x):
  return 1.0 / (1.0 + jnp.exp(-x))


def _silu(x):
  return x * _sigmoid(x)


def _rms(x, gain, n=None):
  n = x.shape[-1] if n is None else n
  ms = jnp.sum(x * x, axis=-1, keepdims=True) * (1.0 / n)
  return x * lax.rsqrt(ms + EPS) * gain


def _norm_inproj_kernel(x_ref, g_ref, w_ref, oa_ref, ob_ref, oc_ref):
  h = _rms(x_ref[...], g_ref[...]).astype(BF16)
  p = jnp.dot(h, w_ref[...], preferred_element_type=F32)
  oa_ref[...] = p[:, :PA_W]
  ob_ref[...] = p[:, PA_W:PA_W + PB_W]
  oc_ref[...] = p[:, PA_W + PB_W:]


def _norm_inproj(x2, gain, w, tm):
  T = x2.shape[0]
  n = w.shape[1]
  return pl.pallas_call(
      _norm_inproj_kernel,
      grid=(T // tm,),
      in_specs=[
          pl.BlockSpec((tm, D_MODEL), lambda i: (i, 0)),
          pl.BlockSpec((1, D_MODEL), lambda i: (0, 0)),
          pl.BlockSpec((D_MODEL, n), lambda i: (0, 0)),
      ],
      out_specs=[
          pl.BlockSpec((tm, PA_W), lambda i: (i, 0)),
          pl.BlockSpec((tm, PB_W), lambda i: (i, 0)),
          pl.BlockSpec((tm, PC_W), lambda i: (i, 0)),
      ],
      out_shape=[
          jax.ShapeDtypeStruct((T, PA_W), F32),
          jax.ShapeDtypeStruct((T, PB_W), F32),
          jax.ShapeDtypeStruct((T, PC_W), F32),
      ],
      compiler_params=_cparams(("parallel",)),
      name="norm_inproj",
  )(x2, gain, w)


def _rglru_kernel(pa_ref, cw_ref, cb_ref, wg_ref, bg_ref, lam_ref, og_ref, o_ref,
                  af_ref, bf_ref, ab_ref, bb_ref, hf_ref, hb_ref):
  S = pa_ref.shape[0]
  W = LRU_WIDTH
  xa = pa_ref[:, 0:W]
  row = lax.broadcasted_iota(jnp.int32, (S, W), 0)

  xc = cb_ref[...] + cw_ref[2:3, :] * xa
  for k in (0, 1, 3):
    d = k - CONV_WIDTH // 2
    shifted = pltpu.roll(xa, (-d) % S, 0)
    valid = (row >= -d) if d < 0 else (row < S - d)
    xc = xc + cw_ref[k:k + 1, :] * jnp.where(valid, shifted, 0.0)

  z = jnp.dot(xc.astype(BF16), wg_ref[...], preferred_element_type=F32) + bg_ref[...]
  lam = lam_ref[...]
  sp = jnp.maximum(-lam, 0.0) + jnp.log1p(jnp.exp(-jnp.abs(lam)))
  for d, (a_ref, b_ref) in enumerate(((af_ref, bf_ref), (ab_ref, bb_ref))):
    r = _sigmoid(z[:, d * 2 * W:d * 2 * W + W])
    i = _sigmoid(z[:, d * 2 * W + W:(d + 1) * 2 * W])
    log_a = (-LRU_C) * r * sp[d:d + 1, :]
    a = jnp.exp(log_a)
    a_ref[...] = a
    y2 = 2.0 * log_a
    u = a * a
    near = jnp.where(u == 1.0, y2, (u - 1.0) * y2 / jnp.log(u))
    em1 = jnp.where(y2 > -0.5, near, u - 1.0)
    b_ref[...] = jnp.sqrt(-em1) * (i * xc)

  n = S // SUBLANE
  r8 = lax.broadcasted_iota(jnp.int32, (SUBLANE, W), 0)

  def body(c, carry):
    cf, cb = carry
    sl = pl.ds(pl.multiple_of(c * SUBLANE, SUBLANE), SUBLANE)
    a = af_ref[sl, :]
    b = bf_ref[sl, :]
    for s in (1, 2, 4):
      m = r8 >= s
      b = jnp.where(m, a * pltpu.roll(b, s, 0) + b, b)
      a = jnp.where(m, a * pltpu.roll(a, s, 0), a)
    h = b + a * cf
    hf_ref[sl, :] = h
    cf = h[SUBLANE - 1:SUBLANE, :]

    sl = pl.ds(pl.multiple_of((n - 1 - c) * SUBLANE, SUBLANE), SUBLANE)
    a = ab_ref[sl, :]
    b = bb_ref[sl, :]
    for s in (1, 2, 4):
      m = r8 < SUBLANE - s
      b = jnp.where(m, a * pltpu.roll(b, SUBLANE - s, 0) + b, b)
      a = jnp.where(m, a * pltpu.roll(a, SUBLANE - s, 0), a)
    h = b + a * cb
    hb_ref[sl, :] = h
    cb = h[0:1, :]
    return cf, cb

  zero = jnp.zeros((1, W), F32)
  lax.fori_loop(0, n, body, (zero, zero), unroll=4)

  g = pa_ref[:, W:2 * W]
  gelu = 0.5 * g * (1.0 + jnp.tanh(math.sqrt(2.0 / math.pi) * (g + 0.044715 * (g * g * g))))
  y = (hf_ref[...] + hb_ref[...]) * gelu
  o_ref[...] = _rms(y, og_ref[...]).astype(o_ref.dtype)


def _rglru(pa, conv_w, conv_b, wg, bg, lam, out_g, B, S):
  T = B * S
  W = LRU_WIDTH
  const = lambda shape: pl.BlockSpec(shape, lambda b: (0,) * len(shape))
  return pl.pallas_call(
      _rglru_kernel,
      grid=(B,),
      in_specs=[
          pl.BlockSpec((S, PA_W), lambda b: (b, 0)),
          const((CONV_WIDTH, W)), const((1, W)), const((W, 4 * W)), const((1, 4 * W)),
          const((2, W)), const((1, W)),
      ],
      out_specs=pl.BlockSpec((S, W), lambda b: (b, 0)),
      out_shape=jax.ShapeDtypeStruct((T, W), BF16),
      scratch_shapes=[pltpu.VMEM((S, W), F32) for _ in range(6)],
      compiler_params=_cparams(("parallel",)),
      name="rglru",
  )(pa, conv_w, conv_b, wg, bg, lam, out_g)


_HG_LEVELS = (1, 2, 4, 8, 16, 32)
HG_SAFE_LOG = -80.0


def _bcast_row_in_blocks(x, block, row):
  L, C = x.shape
  x3 = x.reshape(L // block, block, C)
  return jnp.broadcast_to(x3[:, row:row + 1, :], x3.shape).reshape(L, C)


def _inc_prefix_products(f, row):
  L = f.shape[0]
  out = {1: f}
  p2 = f * jnp.where((row & 1) == 1, pltpu.roll(f, 1, 0), 1.0)
  out[2] = p2
  r4 = row & 3
  p4 = p2 * jnp.where(r4 == 2, pltpu.roll(p2, 1, 0), jnp.where(r4 == 3, pltpu.roll(p2, 2, 0), 1.0))
  out[4] = p4
  p, h = p4, 4
  while h < L:
    p = p * jnp.where((row & h) != 0, _bcast_row_in_blocks(p, 2 * h, h - 1), 1.0)
    h *= 2
    out[h] = p
  return out


def _inc_suffix_products(f, row):
  L = f.shape[0]
  up = lambda x, s: pltpu.roll(x, L - s, 0)
  out = {1: f}
  p2 = f * jnp.where((row & 1) == 0, up(f, 1), 1.0)
  out[2] = p2
  r4 = row & 3
  p4 = p2 * jnp.where(r4 == 1, up(p2, 1), jnp.where(r4 == 0, up(p2, 2), 1.0))
  out[4] = p4
  p, h = p4, 4
  while h < L:
    p = p * jnp.where((row & h) == 0, _bcast_row_in_blocks(p, 2 * h, h), 1.0)
    h *= 2
    out[h] = p
  return out


def _hg_chunk(q, f, v, st_ref, masks_ref, hmask, row, reverse):
  L = q.shape[0]
  k = 1.0 - f
  pre = _inc_prefix_products(f, row)
  suf = _inc_suffix_products(f, row)
  if not reverse:
    q_dec = pre
    k_dec = {h: (jnp.where((row & (h - 1)) == h - 1, 1.0, pltpu.roll(suf[h], L - 1, 0)) if h > 1 else None)
             for h in suf}
  else:
    q_dec = suf
    k_dec = {h: (jnp.where((row & (h - 1)) == 0, 1.0, pltpu.roll(pre[h], 1, 0)) if h > 1 else None)
             for h in pre}

  def head_blockdiag(x):
    return jnp.concatenate([x] * HG_HEADS, axis=0) * hmask

  nt = (((1,), (1,)), ((), ()))
  kb = k.astype(BF16)
  p = masks_ref[0] * lax.dot_general(q.astype(BF16), head_blockdiag(kb), nt, preferred_element_type=F32)
  for li, h in enumerate(_HG_LEVELS):
    qh = (q * q_dec[h]).astype(BF16)
    kh = kb if h == 1 else (k * k_dec[h]).astype(BF16)
    p = p + masks_ref[li + 1] * lax.dot_general(qh, head_blockdiag(kh), nt, preferred_element_type=F32)

  vb = v.astype(BF16)
  o = jnp.dot(p.astype(BF16), head_blockdiag(vb), preferred_element_type=F32)
  st = st_ref[...]
  o = o + lax.dot_general((q * q_dec[L]).astype(BF16), st.astype(BF16), nt, preferred_element_type=F32)

  ke = (k * k_dec[L]).astype(BF16)
  end = q_dec[L][0:1, :] if reverse else q_dec[L][L - 1:L, :]
  tn = (((0,), (0,)), ((), ()))
  upd = lax.dot_general(vb, ke, tn, preferred_element_type=F32)
  st_ref[...] = st * end + upd * hmask.astype(F32)
  return o


def _hg_chunk_fast(q, f, v, st_ref, masks_ref, hmask, row, reverse):
  L = q.shape[0]
  a = (_inc_suffix_products(f, row) if reverse else _inc_prefix_products(f, row))[L]
  qt = (q * a).astype(BF16)
  kt = ((1.0 - f) * (1.0 / a)).astype(BF16)

  def head_blockdiag(x):
    return jnp.concatenate([x] * HG_HEADS, axis=0) * hmask

  nt = (((1,), (1,)), ((), ()))
  tn = (((0,), (0,)), ((), ()))
  p = jnp.where(masks_ref[len(_HG_LEVELS) + 1] != 0.0,
                lax.dot_general(qt, head_blockdiag(kt), nt, preferred_element_type=F32), 0.0)
  vb = v.astype(BF16)
  st = st_ref[...]
  o = jnp.dot(p.astype(BF16), head_blockdiag(vb), preferred_element_type=F32)
  o = o + lax.dot_general(qt, st.astype(BF16), nt, preferred_element_type=F32)
  end = a[0:1, :] if reverse else a[L - 1:L, :]
  upd = lax.dot_general(vb, kt, tn, preferred_element_type=F32)
  st_ref[...] = (st + upd * hmask.astype(F32)) * end
  return o


def _hgrn2_kernel(pb_ref, lb_ref, ng_ref, mf_ref, mb_ref, hm_ref, ones_ref, o_ref,
                  q_ref, ff_ref, fb_ref, of_ref, ob_ref, sf_ref, sb_ref):
  S = pb_ref.shape[0]
  W = HG_WIDTH
  L = HG_CHUNK
  n = S // L
  row = lax.broadcasted_iota(jnp.int32, (L, W), 0)
  hmask = hm_ref[...]
  sf_ref[...] = jnp.zeros_like(sf_ref)
  sb_ref[...] = jnp.zeros_like(sb_ref)

  def full_reduce(x, op):
    return op(op(x, axis=1, keepdims=True), axis=0, keepdims=True)

  q = _silu(pb_ref[:, 0:W]) * (HG_DK ** -0.5)
  q_ref[...] = q
  margin = (jnp.log(1.0 + full_reduce(jnp.abs(q), jnp.max))
            + jnp.log(1.0 + full_reduce(jnp.abs(pb_ref[:, 3 * W:4 * W]), jnp.max)))
  log_min = None
  for d, f_ref in enumerate((ff_ref, fb_ref)):
    lb = lb_ref[d:d + 1, :]
    f = lb + (1.0 - lb) * _sigmoid(pb_ref[:, (d + 1) * W:(d + 2) * W])
    f_ref[...] = f
    chunk_log = jnp.sum(jnp.log(f).reshape(n, L, W), axis=1)
    m = full_reduce(chunk_log, jnp.min)
    log_min = m if log_min is None else jnp.minimum(log_min, m)
  in_range = (log_min - margin)[0, 0] >= HG_SAFE_LOG

  def make_body(chunk_fn):
    def body(c, _):
      for f_ref, out_ref, st_ref, m_ref, rev, cc in ((ff_ref, of_ref, sf_ref, mf_ref, False, c),
                                                     (fb_ref, ob_ref, sb_ref, mb_ref, True, n - 1 - c)):
        sl = pl.ds(pl.multiple_of(cc * L, L), L)
        out_ref[sl, :] = chunk_fn(q_ref[sl, :], f_ref[sl, :], pb_ref[sl, 3 * W:4 * W], st_ref, m_ref,
                                  hmask, row, rev)
      return 0
    return body

  @pl.when(in_range)
  def _():
    lax.fori_loop(0, n, make_body(_hg_chunk_fast), 0)

  @pl.when(jnp.logical_not(in_range))
  def _():
    lax.fori_loop(0, n, make_body(_hg_chunk), 0)

  o = of_ref[...] + ob_ref[...]
  ms = jnp.dot((o * o).astype(BF16), ones_ref[...], preferred_element_type=F32) * (1.0 / HG_DK)
  y = o * lax.rsqrt(ms + EPS) * ng_ref[...]
  o_ref[...] = (y * _silu(pb_ref[:, 4 * W:5 * W])).astype(o_ref.dtype)


def _hg_constants():
  L = HG_CHUNK
  t = jnp.arange(L)[:, None]
  s = jnp.arange(L)[None, :]
  fwd, bwd = [t == s], [t == s]
  for h in _HG_LEVELS:
    same = (t // (2 * h)) == (s // (2 * h))
    fwd.append(same & ((t & h) != 0) & ((s & h) == 0))
    bwd.append(same & ((t & h) == 0) & ((s & h) != 0))
  fwd.append(t >= s)
  bwd.append(t <= s)
  tile = lambda m: jnp.tile(jnp.stack(m).astype(F32), (1, 1, HG_HEADS))
  hid = jnp.arange(HG_WIDTH) // HG_DK
  hmask = (hid[:, None] == hid[None, :])
  return tile(fwd), tile(bwd), hmask.astype(BF16), hmask.astype(BF16)


def _hgrn2(pb, lb, norm_g, B, S):
  T = B * S
  W = HG_WIDTH
  mf, mb, hmask, ones = _hg_constants()
  const = lambda shape: pl.BlockSpec(shape, lambda b: (0,) * len(shape))
  return pl.pallas_call(
      _hgrn2_kernel,
      grid=(B,),
      in_specs=[
          pl.BlockSpec((S, PB_W), lambda b: (b, 0)),
          const((2, W)), const((1, W)), const(mf.shape), const(mb.shape), const((W, W)), const((W, W)),
      ],
      out_specs=pl.BlockSpec((S, W), lambda b: (b, 0)),
      out_shape=jax.ShapeDtypeStruct((T, W), BF16),
      scratch_shapes=[pltpu.VMEM((S, W), F32) for _ in range(5)]
                     + [pltpu.VMEM((W, W), F32), pltpu.VMEM((W, W), F32)],
      compiler_params=_cparams(("parallel",)),
      name="hgrn2",
  )(pb, lb, norm_g, mf, mb, hmask, ones)


ROPE_HALF = MLA_ROPE // 2
ATTN_MAX_BOUND = 60.0
ATTN_KEY_BLOCK = 256
ATTN_Q_BLOCK = 1024


def _rope_tables_kernel(pos_ref, invf_ref, sgn_ref, cos_ref, sin_ref):
  ang = pos_ref[...].astype(F32) * invf_ref[...]
  cos_ref[...] = jnp.cos(ang)
  sin_ref[...] = sgn_ref[...] * jnp.sin(ang)


def _rope_tables(pos2, tm):
  T = pos2.shape[0]
  inv_freq = (ROPE_THETA ** (-jnp.arange(0, ROPE_HALF, dtype=F32) * (2.0 / MLA_ROPE))).reshape(1, ROPE_HALF)
  ones = jnp.ones((1, ROPE_HALF), F32)
  no_nope = jnp.zeros((1, MLA_NOPE), F32)
  invf = _slot_lanes(no_nope, jnp.concatenate([inv_freq, inv_freq], axis=1))
  sgn = _slot_lanes(no_nope, jnp.concatenate([-ones, ones], axis=1))
  row = pl.BlockSpec((1, HEAD_SLOT), lambda i: (0, 0))
  return pl.pallas_call(
      _rope_tables_kernel,
      grid=(T // tm,),
      in_specs=[pl.BlockSpec((tm, 1), lambda i: (i, 0)), row, row],
      out_specs=[pl.BlockSpec((tm, HEAD_SLOT), lambda i: (i, 0))] * 2,
      out_shape=[jax.ShapeDtypeStruct((T, HEAD_SLOT), F32)] * 2,
      compiler_params=_cparams(("parallel",)),
      name="rope_tables",
  )(pos2, invf, sgn)


def _rope(t, cos_t, sin_t):
  return t * cos_t + pltpu.roll(t, HEAD_SLOT // 2, 1) * sin_t


def _mla_prep_kernel(pc_ref, cos_ref, sin_ref, qn_ref, kvn_ref, wq_ref, wk_ref, wv_ref, vone_ref, gq_ref, gk_ref,
                     q_ref, k_ref, v_ref):
  cos_t = cos_ref[...]
  sin_t = sin_ref[...]
  cq = _rms(pc_ref[:, 0:MLA_Q_RANK], qn_ref[...]).astype(BF16)
  ckv = _rms(pc_ref[:, MLA_Q_RANK:MLA_Q_RANK + MLA_KV_RANK], kvn_ref[...]).astype(BF16)
  kr_slot = pc_ref[:, MLA_Q_RANK + MLA_KV_RANK:]
  q_all = jnp.dot(cq, wq_ref[...], preferred_element_type=F32)
  k_all = jnp.dot(ckv, wk_ref[...], preferred_element_type=F32)
  v_ref[...] = (jnp.dot(ckv, wv_ref[...], preferred_element_type=F32) + vone_ref[...]).astype(v_ref.dtype)
  for h in range(MLA_HEADS):
    sl = slice(h * HEAD_SLOT, (h + 1) * HEAD_SLOT)
    qh = _rope(_rms(q_all[:, sl], gq_ref[...], MLA_QK), cos_t, sin_t)
    q_ref[h] = qh.astype(q_ref.dtype)
    kh = _rope(_rms(k_all[:, sl] + kr_slot, gk_ref[...], MLA_QK), cos_t, sin_t)
    k_ref[h] = kh.astype(k_ref.dtype)


def _mla_prep(pc, cos_t, sin_t, q_norm, kv_norm, wq, wk, wv, vone, gq, gk, B, S, ts):
  H = MLA_HEADS
  nb = S // ts
  const = lambda shape: pl.BlockSpec(shape, lambda b, s: (0,) * len(shape))
  tok = lambda w: pl.BlockSpec((ts, w), lambda b, s: (b * nb + s, 0))
  return pl.pallas_call(
      _mla_prep_kernel,
      grid=(B, nb),
      in_specs=[
          tok(PC_W), tok(HEAD_SLOT), tok(HEAD_SLOT),
          const((1, MLA_Q_RANK)), const((1, MLA_KV_RANK)),
          const(wq.shape), const(wk.shape), const(wv.shape), const(vone.shape),
          const((1, HEAD_SLOT)), const((1, HEAD_SLOT)),
      ],
      out_specs=[
          pl.BlockSpec((None, H, ts, HEAD_SLOT), lambda b, s: (b, 0, s, 0)),
          pl.BlockSpec((None, H, ts, HEAD_SLOT), lambda b, s: (b, 0, s, 0)),
          pl.BlockSpec((ts, H * HEAD_SLOT), lambda b, s: (b * nb + s, 0)),
      ],
      out_shape=[
          jax.ShapeDtypeStruct((B, H, S, HEAD_SLOT), BF16),
          jax.ShapeDtypeStruct((B, H, S, HEAD_SLOT), BF16),
          jax.ShapeDtypeStruct((B * S, H * HEAD_SLOT), BF16),
      ],
      compiler_params=_cparams(("parallel", "parallel")),
      name="mla_prep",
  )(pc, cos_t, sin_t, q_norm, kv_norm, wq, wk, wv, vone, gq, gk)


def _attention_kernel(bound_ref, q_ref, k_ref, v_ref, o_ref):
  tq = q_ref.shape[1]
  S = k_ref.shape[1]
  bound = bound_ref[0]
  nt = (((1,), (1,)), ((), ()))

  def finish(res):
    lane = lax.broadcasted_iota(jnp.int32, (tq, HEAD_SLOT), 1)
    even = res[0] * (1.0 / res[0][:, MLA_V:MLA_V + 1])
    odd = res[1] * (1.0 / res[1][:, 0:1])
    o_ref[...] = jnp.where(lane < MLA_V, even, odd).astype(o_ref.dtype)

  @pl.when(bound <= ATTN_MAX_BOUND)
  def _():
    res = []
    for h in range(2):
      acc = None
      for j in range(S // ATTN_KEY_BLOCK):
        ks = slice(j * ATTN_KEY_BLOCK, (j + 1) * ATTN_KEY_BLOCK)
        s = lax.dot_general(q_ref[h], k_ref[h, ks, :], nt, preferred_element_type=F32)
        p = jnp.exp2(s - bound).astype(BF16)
        c = jnp.dot(p, v_ref[ks, h * HEAD_SLOT:(h + 1) * HEAD_SLOT], preferred_element_type=F32)
        acc = c if acc is None else acc + c
      res.append(acc)
    finish(res)

  @pl.when(bound > ATTN_MAX_BOUND)
  def _():
    res = []
    for h in range(2):
      s = lax.dot_general(q_ref[h], k_ref[h], nt, preferred_element_type=F32)
      p = jnp.exp2(s - jnp.max(s, axis=-1, keepdims=True)).astype(BF16)
      res.append(jnp.dot(p, v_ref[:, h * HEAD_SLOT:(h + 1) * HEAD_SLOT], preferred_element_type=F32))
    finish(res)


def _attention(bound, q, k, v, B, S, tq):
  H = MLA_HEADS
  nq = S // tq
  return pl.pallas_call(
      _attention_kernel,
      grid=(B, H // 2, nq),
      in_specs=[
          pl.BlockSpec(memory_space=pltpu.SMEM),
          pl.BlockSpec((None, 2, tq, HEAD_SLOT), lambda b, hp, i: (b, hp, i, 0)),
          pl.BlockSpec((None, 2, S, HEAD_SLOT), lambda b, hp, i: (b, hp, 0, 0)),
          pl.BlockSpec((S, 2 * HEAD_SLOT), lambda b, hp, i: (b, hp)),
      ],
      out_specs=pl.BlockSpec((tq, 2 * MLA_V), lambda b, hp, i: (b * nq + i, hp)),
      out_shape=jax.ShapeDtypeStruct((B * S, MLA_WIDTH), F32),
      compiler_params=_cparams(("parallel", "parallel", "parallel")),
      name="mla_attention",
  )(bound, q, k, v)


def _outproj_kernel(x_ref, ya_ref, yb_ref, yc_ref, gc_ref, w_ref, o_ref):
  yc = _rms(yc_ref[...], gc_ref[...]).astype(BF16)
  a0, a1 = LRU_WIDTH, LRU_WIDTH + HG_WIDTH
  acc = jnp.dot(ya_ref[...], w_ref[0:a0, :], preferred_element_type=F32)
  acc = acc + jnp.dot(yb_ref[...], w_ref[a0:a1, :], preferred_element_type=F32)
  acc = acc + jnp.dot(yc, w_ref[a1:, :], preferred_element_type=F32)
  o_ref[...] = x_ref[...] + acc


def _outproj(x2, ya, yb, yc, gc, w, tm):
  T = x2.shape[0]
  tok = lambda wd: pl.BlockSpec((tm, wd), lambda i: (i, 0))
  return pl.pallas_call(
      _outproj_kernel,
      grid=(T // tm,),
      in_specs=[tok(D_MODEL), tok(LRU_WIDTH), tok(HG_WIDTH), tok(MLA_WIDTH),
                pl.BlockSpec((1, MLA_WIDTH), lambda i: (0, 0)),
                pl.BlockSpec((D_MODEL, D_MODEL), lambda i: (0, 0))],
      out_specs=tok(D_MODEL),
      out_shape=jax.ShapeDtypeStruct((T, D_MODEL), F32),
      compiler_params=_cparams(("parallel",)),
      name="outproj",
  )(x2, ya, yb, yc, gc, w)


def _ffn_kernel(x_ref, g_ref, wg_ref, wu_ref, wd_ref, o_ref):
  x = x_ref[...]
  h = _rms(x, g_ref[...]).astype(BF16)
  F = wg_ref.shape[1]
  acc = x
  for c0 in range(0, F, FFN_CHUNK):
    c1 = min(c0 + FFN_CHUNK, F)
    g = jnp.dot(h, wg_ref[:, c0:c1], preferred_element_type=F32)
    u = jnp.dot(h, wu_ref[:, c0:c1], preferred_element_type=F32)
    a = (_silu(g) * u).astype(BF16)
    acc = acc + jnp.dot(a, wd_ref[c0:c1, :], preferred_element_type=F32)
  o_ref[...] = acc


def _ffn(x2, gain, wg, wu, wd, tm):
  T = x2.shape[0]
  F = wg.shape[1]
  resident = lambda shape: pl.BlockSpec(shape, lambda i: (0, 0), pipeline_mode=pl.Buffered(1))
  return pl.pallas_call(
      _ffn_kernel,
      grid=(T // tm,),
      in_specs=[
          pl.BlockSpec((tm, D_MODEL), lambda i: (i, 0)),
          pl.BlockSpec((1, D_MODEL), lambda i: (0, 0)),
          resident((D_MODEL, F)), resident((D_MODEL, F)), resident((F, D_MODEL)),
      ],
      out_specs=pl.BlockSpec((tm, D_MODEL), lambda i: (i, 0)),
      out_shape=jax.ShapeDtypeStruct((T, D_MODEL), F32),
      compiler_params=_cparams(("parallel",)),
      name="ffn_dense",
  )(x2, gain, wg, wu, wd)


def _router_kernel(x_ref, g_ref, wr_ref, sel_ref):
  def split(v):
    hi = v.astype(BF16)
    return hi, (v - hi.astype(F32)).astype(BF16)

  h_hi, h_lo = split(_rms(x_ref[...], g_ref[...]))
  w_hi, w_lo = split(wr_ref[...])
  logits = (jnp.dot(h_hi, w_hi, preferred_element_type=F32)
            + jnp.dot(h_lo, w_hi, preferred_element_type=F32)
            + jnp.dot(h_hi, w_lo, preferred_element_type=F32))
  lane = lax.broadcasted_iota(jnp.int32, logits.shape, 1)
  neg = jnp.float32(-jnp.inf)
  lg = jnp.where(lane < N_EXPERTS, logits, neg)
  m1 = jnp.max(lg, axis=-1, keepdims=True)
  i1 = jnp.min(jnp.where(lg == m1, lane, LANE), axis=-1, keepdims=True)
  lg2 = jnp.where(lane == i1, neg, lg)
  m2 = jnp.max(lg2, axis=-1, keepdims=True)
  i2 = jnp.min(jnp.where(lg2 == m2, lane, LANE), axis=-1, keepdims=True)
  e2 = jnp.exp(m2 - m1)
  w1 = 1.0 / (1.0 + e2)
  w2 = e2 / (1.0 + e2)
  sel_ref[...] = jnp.where(lane == 0, i1.astype(F32), jnp.where(lane == 1, i2.astype(F32),
                           jnp.where(lane == 2, w1, jnp.where(lane == 3, w2, 0.0))))


def _router(x2, gain, wr, tm):
  T = x2.shape[0]
  return pl.pallas_call(
      _router_kernel,
      grid=(T // tm,),
      in_specs=[pl.BlockSpec((tm, D_MODEL), lambda i: (i, 0)),
                pl.BlockSpec((1, D_MODEL), lambda i: (0, 0)),
                pl.BlockSpec((D_MODEL, LANE), lambda i: (0, 0))],
      out_specs=pl.BlockSpec((tm, LANE), lambda i: (i, 0)),
      out_shape=jax.ShapeDtypeStruct((T, LANE), F32),
      compiler_params=_cparams(("parallel",)),
      name="moe_router",
  )(x2, gain, wr)


def _route_plan(sel, tm):
  T = sel.shape[0]
  e_flat = sel[:, 0:2].astype(jnp.int32).reshape(1, 2 * T)
  onehot = (e_flat == jnp.arange(N_EXPERTS, dtype=jnp.int32)[:, None]).astype(jnp.int32)
  csum = jnp.cumsum(onehot, axis=1)
  counts = csum[:, -1]
  tiles_e = (counts + tm - 1) // tm
  tile_end = jnp.cumsum(tiles_e)
  offs = (tile_end - tiles_e) * tm
  pos = jnp.sum(onehot * (csum - 1 + offs[:, None]), axis=0)
  n_tiles = (2 * T) // tm + N_EXPERTS
  tile_expert = jnp.sum(jnp.arange(n_tiles, dtype=jnp.int32)[:, None] >= tile_end[None, :], axis=1)
  tile_expert = jnp.minimum(tile_expert, N_EXPERTS - 1).astype(jnp.int32)
  last_tile = jnp.where(tiles_e > 0, offs + (tiles_e - 1) * tm, -1)
  spare = tile_end[-1] + jnp.arange(N_EXPERTS, dtype=jnp.int32)
  spare_tile = jnp.where(spare < n_tiles, spare * tm, -1)
  clear = jnp.concatenate([last_tile, spare_tile]).astype(jnp.int32)
  return pos.astype(jnp.int32), tile_expert, tile_end[-1:].astype(jnp.int32), clear, n_tiles


def _row_copy(src, dst, sem, rows=1, src_row=0, dst_row=0):
  return pltpu.make_async_copy(src.at[pl.ds(src_row, rows), :], dst.at[pl.ds(dst_row, rows), :], sem)


def _dispatch_kernel(pos_ref, lt_ref, x_ref, xs_hbm, zbuf, sem, zsem, *, tm):
  tt = x_ref.shape[0]
  zr = zbuf.shape[0]

  @pl.when(pl.program_id(0) == 0)
  def _():
    zbuf[...] = jnp.zeros_like(zbuf)
    for e in range(lt_ref.shape[0]):
      @pl.when(lt_ref[e] >= 0)
      def _():
        rows = [pl.multiple_of(lt_ref[e] + c * zr, SUBLANE) for c in range(tm // zr)]
        for r0 in rows:
          _row_copy(zbuf, xs_hbm, zsem, zr, 0, r0).start()
        for r0 in rows:
          _row_copy(zbuf, xs_hbm, zsem, zr, 0, r0).wait()

  def body(r, _):
    for c in range(2):
      _row_copy(x_ref, xs_hbm, sem, 1, r, pos_ref[0, 0, 2 * r + c]).start()
    return 0

  lax.fori_loop(0, tt, body, 0, unroll=8)
  for c in range(2):
    _row_copy(x_ref, xs_hbm, sem, tt).wait()


def _dispatch(x2, pos, last_tile, n_rows, tt, tm):
  T = x2.shape[0]
  pos3 = pos.reshape(T // tt, 1, 2 * tt)
  zr = min(MOE_ZERO_ROWS, tm)
  return pl.pallas_call(
      functools.partial(_dispatch_kernel, tm=tm),
      grid=(T // tt,),
      in_specs=[pl.BlockSpec((1, 1, 2 * tt), lambda i: (i, 0, 0), memory_space=pltpu.SMEM),
                pl.BlockSpec(memory_space=pltpu.SMEM),
                pl.BlockSpec((tt, D_MODEL), lambda i: (i, 0))],
      out_specs=pl.BlockSpec(memory_space=pl.ANY),
      out_shape=jax.ShapeDtypeStruct((n_rows, D_MODEL), F32),
      scratch_shapes=[pltpu.VMEM((zr, D_MODEL), F32), pltpu.SemaphoreType.DMA(()), pltpu.SemaphoreType.DMA(())],
      compiler_params=_cparams(("arbitrary",)),
      name="moe_dispatch",
  )(pos3, last_tile, x2)


def _moe_group_kernel(te_ref, nu_ref, xs_ref, g_ref, w1_ref, w3_ref, w2_ref, ys_ref, h_ref, acc_ref):
  del te_ref
  i = pl.program_id(0)
  f = pl.program_id(1)
  last = pl.num_programs(1) - 1
  used = i < nu_ref[0]

  @pl.when(used & (f == 0))
  def _():
    h_ref[...] = _rms(xs_ref[...], g_ref[...]).astype(BF16)
    acc_ref[...] = jnp.zeros_like(acc_ref)

  @pl.when(used)
  def _():
    h = h_ref[...]
    g = jnp.dot(h, w1_ref[...].astype(BF16), preferred_element_type=F32)
    u = jnp.dot(h, w3_ref[...].astype(BF16), preferred_element_type=F32)
    a = (_silu(g) * u).astype(BF16)
    acc_ref[...] += jnp.dot(a, w2_ref[...].astype(BF16), preferred_element_type=F32)

  @pl.when(used & (f == last))
  def _():
    ys_ref[...] = acc_ref[...]

  @pl.when(jnp.logical_not(used) & (f == last))
  def _():
    ys_ref[...] = jnp.zeros_like(ys_ref)


def _moe_group(xs, gain, tile_expert, n_used, w1, w3, w2, tm, tf):
  P = xs.shape[0]
  F = w1.shape[2]
  nf = F // tf
  fidx = lambda i, f, te, nu: jnp.where(i < nu[0], f, nf - 1)
  grid_spec = pltpu.PrefetchScalarGridSpec(
      num_scalar_prefetch=2,
      grid=(P // tm, nf),
      in_specs=[
          pl.BlockSpec((tm, D_MODEL), lambda i, f, te, nu: (jnp.minimum(i, nu[0] - 1), 0)),
          pl.BlockSpec((1, D_MODEL), lambda i, f, te, nu: (0, 0)),
          pl.BlockSpec((None, D_MODEL, tf), lambda i, f, te, nu: (te[i], 0, fidx(i, f, te, nu))),
          pl.BlockSpec((None, D_MODEL, tf), lambda i, f, te, nu: (te[i], 0, fidx(i, f, te, nu))),
          pl.BlockSpec((None, tf, D_MODEL), lambda i, f, te, nu: (te[i], fidx(i, f, te, nu), 0)),
      ],
      out_specs=pl.BlockSpec((tm, D_MODEL), lambda i, f, te, nu: (i, 0)),
      scratch_shapes=[pltpu.VMEM((tm, D_MODEL), BF16), pltpu.VMEM((tm, D_MODEL), F32)],
  )
  return pl.pallas_call(
      _moe_group_kernel,
      grid_spec=grid_spec,
      out_shape=jax.ShapeDtypeStruct((P, D_MODEL), F32),
      compiler_params=_cparams(("arbitrary", "arbitrary")),
      name="moe_group",
  )(tile_expert, n_used, xs, gain, w1, w3, w2)


def _combine_kernel(pos_ref, x_ref, sel_ref, ys_hbm, o_ref, ybuf, sem):
  tt = x_ref.shape[0]

  def body(r, _):
    for c in range(2):
      _row_copy(ys_hbm, ybuf.at[c], sem, 1, pos_ref[0, 0, 2 * r + c], r).start()
    return 0

  lax.fori_loop(0, tt, body, 0, unroll=8)
  for c in range(2):
    _row_copy(ys_hbm, ybuf.at[c], sem, tt).wait()
  sel = sel_ref[...]
  o_ref[...] = x_ref[...] + sel[:, 2:3] * ybuf[0] + sel[:, 3:4] * ybuf[1]


def _combine(x2, sel, pos, ys, tt):
  T = x2.shape[0]
  pos3 = pos.reshape(T // tt, 1, 2 * tt)
  return pl.pallas_call(
      _combine_kernel,
      grid=(T // tt,),
      in_specs=[pl.BlockSpec((1, 1, 2 * tt), lambda i: (i, 0, 0), memory_space=pltpu.SMEM),
                pl.BlockSpec((tt, D_MODEL), lambda i: (i, 0)),
                pl.BlockSpec((tt, LANE), lambda i: (i, 0)),
                pl.BlockSpec(memory_space=pl.ANY)],
      out_specs=pl.BlockSpec((tt, D_MODEL), lambda i: (i, 0)),
      out_shape=jax.ShapeDtypeStruct((T, D_MODEL), F32),
      scratch_shapes=[pltpu.VMEM((2, tt, D_MODEL), F32), pltpu.SemaphoreType.DMA(())],
      compiler_params=_cparams(("arbitrary",)),
      name="moe_combine",
  )(pos3, x2, sel, ys)


def _block_diag(w):
  n, d, e = w.shape
  eye = jnp.eye(n, dtype=w.dtype)
  return (eye[:, None, :, None] * w[:, :, None, :]).reshape(n * d, n * e)


def _slot_lanes(nope, rope):
  z = lambda n: jnp.zeros(nope.shape[:-1] + (n,), nope.dtype)
  cut = HEAD_SLOT // 2 - ROPE_HALF
  x1, x2 = (z(ROPE_HALF), z(ROPE_HALF)) if rope is None else (rope[..., :ROPE_HALF], rope[..., ROPE_HALF:])
  return jnp.concatenate([x1, nope[..., :cut], x2, nope[..., cut:], z(HEAD_SLOT - MLA_QK)], axis=-1)


def _head_slots(w, width):
  r = w.shape[0]
  w3 = w.reshape(r, MLA_HEADS, width)
  rope = w3[..., MLA_NOPE:] if width == MLA_QK else None
  return _slot_lanes(w3[..., :MLA_NOPE], rope).reshape(r, MLA_HEADS * HEAD_SLOT)


def _slot_gain(g):
  g = g.astype(F32).reshape(1, MLA_QK)
  return _slot_lanes(g[:, :MLA_NOPE], g[:, MLA_NOPE:])


def _value_slots(wv):
  r = wv.shape[0]
  w4 = wv.reshape(r, MLA_HEADS // 2, 2, MLA_V)
  z = jnp.zeros((r, MLA_HEADS // 2, HEAD_SLOT - MLA_V), wv.dtype)
  out = jnp.stack([jnp.concatenate([w4[:, :, 0], z], axis=-1), jnp.concatenate([z, w4[:, :, 1]], axis=-1)], axis=2)
  one = np.zeros((1, MLA_HEADS // 2, 2, HEAD_SLOT), np.float32)
  one[:, :, 0, MLA_V] = 1.0
  one[:, :, 1, 0] = 1.0
  return out.reshape(r, MLA_HEADS * HEAD_SLOT), jnp.asarray(one.reshape(1, MLA_HEADS * HEAD_SLOT))


def _score_bound(gq, gk):
  b = MLA_QK ** 0.5 * math.log2(math.e) * 1.02 * jnp.max(jnp.abs(gq)) * jnp.max(jnp.abs(gk))
  return b.astype(F32).reshape(1)


def kernel(x, positions, norm_mix, w_in, conv_w, conv_b, lru_wa, lru_ba, lru_wx, lru_bx, lru_lam,
           out_g_a, hg_lb_logits, hg_norm_g, mla_q_norm, mla_w_uq, mla_kv_norm, mla_w_ukv,
           qk_norm_q, qk_norm_k, out_g_c, w_out, norm_ffn, ffn_w_gate_up, ffn_w_down,
           moe_router, moe_w1, moe_w3, moe_w2):
  B, S, D = x.shape
  T = B * S
  depth = w_in.shape[0]
  tm = 512
  row = lambda v: v.astype(F32).reshape(1, -1)

  p = jax.nn.softmax(hg_lb_logits.astype(F32), axis=0)
  lower_bounds = jnp.cumsum(p, axis=0) - p[0:1]

  x2 = x.reshape(T, D)
  cos_t, sin_t = _rope_tables(positions.reshape(T, 1), tm)

  for l in range(depth):
    n_main = w_in.shape[2] - MLA_ROPE
    kr_slot_w = _slot_lanes(jnp.zeros((D, MLA_NOPE), F32), w_in[l][:, n_main:])
    w_in_l = jnp.concatenate([w_in[l][:, :n_main], kr_slot_w], axis=1).astype(BF16)
    wg_lru = jnp.concatenate(
        [_block_diag(lru_wa[l, 0]), _block_diag(lru_wx[l, 0]),
         _block_diag(lru_wa[l, 1]), _block_diag(lru_wx[l, 1])], axis=1).astype(BF16)
    bg_lru = jnp.concatenate([lru_ba[l, 0], lru_bx[l, 0], lru_ba[l, 1], lru_bx[l, 1]]).reshape(1, -1)
    wq = _head_slots(mla_w_uq[l], MLA_QK).astype(BF16)
    wkv = mla_w_ukv[l].reshape(MLA_KV_RANK, MLA_HEADS, MLA_NOPE + MLA_V)
    wk = _head_slots(wkv[:, :, :MLA_NOPE].reshape(MLA_KV_RANK, -1), MLA_NOPE).astype(BF16)
    wv, vone = _value_slots(wkv[:, :, MLA_NOPE:].reshape(MLA_KV_RANK, MLA_WIDTH))
    wv = wv.astype(BF16)

    pa, pb, pc = _norm_inproj(x2, row(norm_mix[l]), w_in_l, tm)
    ya = _rglru(pa, conv_w[l], row(conv_b[l]), wg_lru, bg_lru, lru_lam[l], row(out_g_a[l]), B, S)
    yb = _hgrn2(pb, lower_bounds[l], row(jnp.tile(hg_norm_g[l], HG_HEADS)), B, S)
    q, k, v = _mla_prep(pc, cos_t, sin_t, row(mla_q_norm[l]), row(mla_kv_norm[l]), wq, wk, wv, vone,
                        _slot_gain(qk_norm_q[l]) * (MLA_QK ** -0.5 * math.log2(math.e)),
                        _slot_gain(qk_norm_k[l]), B, S, min(tm, S))
    yc = _attention(_score_bound(qk_norm_q[l], qk_norm_k[l]), q, k, v, B, S, min(ATTN_Q_BLOCK, S))
    x2 = _outproj(x2, ya, yb, yc, row(out_g_c[l]), w_out[l].astype(BF16), tm)

    if l % 2 == 0:
      wgu = ffn_w_gate_up[l // 2]
      pad = D_FF_PAD - D_FF
      wg = jnp.pad(wgu[:, :D_FF], ((0, 0), (0, pad))).astype(BF16)
      wu = jnp.pad(wgu[:, D_FF:], ((0, 0), (0, pad))).astype(BF16)
      wd = jnp.pad(ffn_w_down[l // 2], ((0, pad), (0, 0))).astype(BF16)
      x2 = _ffn(x2, row(norm_ffn[l]), wg, wu, wd, tm)
    else:
      wr = jnp.pad(moe_router[l // 2].astype(F32), ((0, 0), (0, LANE - N_EXPERTS)))
      gain = row(norm_ffn[l])
      tme = 1024 if T >= 8192 else 128
      sel = _router(x2, gain, wr, tm)
      pos, tile_expert, n_used, last_tile, n_tiles = _route_plan(sel, tme)
      xs = _dispatch(x2, pos, last_tile, n_tiles * tme, tm, tme)
      ys = _moe_group(xs, gain, tile_expert, n_used, moe_w1[l // 2], moe_w3[l // 2], moe_w2[l // 2],
                      tme, D_EXPERT // 7)
      x2 = _combine(x2, sel, pos, ys, tm)
  return x2.reshape(B, S, D)
```

```python
import functools
import math

import jax
import jax.numpy as jnp
import numpy as np
from jax import lax
from jax.experimental import pallas as pl
from jax.experimental.pallas import tpu as pltpu

F32 = jnp.float32
BF16 = jnp.bfloat16

D_MODEL = 1024
LRU_WIDTH = 256
LRU_BLOCKS = 4
CONV_WIDTH = 4
LRU_C = 8.0
HG_HEADS = 4
HG_DK = 64
HG_WIDTH = 256
MLA_HEADS = 8
MLA_NOPE = 64
MLA_ROPE = 32
MLA_V = 64
MLA_QK = MLA_NOPE + MLA_ROPE
MLA_Q_RANK = 256
MLA_KV_RANK = 128
MLA_WIDTH = MLA_HEADS * MLA_V
ROPE_THETA = 10000.0
D_FF = 2752
N_EXPERTS = 8
D_EXPERT = 3584
EPS = 1e-6

LANE = 128
SUBLANE = 8
HEAD_SLOT = 128
HG_CHUNK = 64
D_FF_PAD = 2816
MOE_ZERO_ROWS = 256
FFN_CHUNK = 1024
VMEM_LIMIT = 56 * 1024 * 1024

PA_W, PB_W, PC_W = 512, 1280, 512


def _cparams(sem):
  return pltpu.CompilerParams(dimension_semantics=sem, vmem_limit_bytes=VMEM_LIMIT)


def _sigmoid(x):
  return 0.5 + 0.5 * jnp.tanh(0.5 * x)


def _silu(x):
  return x * _sigmoid(x)


def _rms(x, gain, n=None):
  n = x.shape[-1] if n is None else n
  ms = jnp.sum(x * x, axis=-1, keepdims=True) * (1.0 / n)
  return x * lax.rsqrt(ms + EPS) * gain


def _norm_inproj_kernel(x_ref, g_ref, w_ref, oa_ref, ob_ref, oc_ref):
  h = _rms(x_ref[...], g_ref[...]).astype(BF16)
  p = jnp.dot(h, w_ref[...], preferred_element_type=F32)
  oa_ref[...] = p[:, :PA_W]
  ob_ref[...] = p[:, PA_W:PA_W + PB_W]
  oc_ref[...] = p[:, PA_W + PB_W:]


def _norm_inproj(x2, gain, w, tm):
  T = x2.shape[0]
  n = w.shape[1]
  return pl.pallas_call(
      _norm_inproj_kernel,
      grid=(T // tm,),
      in_specs=[
          pl.BlockSpec((tm, D_MODEL), lambda i: (i, 0)),
          pl.BlockSpec((1, D_MODEL), lambda i: (0, 0)),
          pl.BlockSpec((D_MODEL, n), lambda i: (0, 0)),
      ],
      out_specs=[
          pl.BlockSpec((tm, PA_W), lambda i: (i, 0)),
          pl.BlockSpec((tm, PB_W), lambda i: (i, 0)),
          pl.BlockSpec((tm, PC_W), lambda i: (i, 0)),
      ],
      out_shape=[
          jax.ShapeDtypeStruct((T, PA_W), F32),
          jax.ShapeDtypeStruct((T, PB_W), F32),
          jax.ShapeDtypeStruct((T, PC_W), F32),
      ],
      compiler_params=_cparams(("parallel",)),
      name="norm_inproj",
  )(x2, gain, w)


def _rglru_kernel(pa_ref, cw_ref, cb_ref, wg_ref, bg_ref, lam_ref, og_ref, o_ref,
                  af_ref, bf_ref, ab_ref, bb_ref, hf_ref, hb_ref):
  S = pa_ref.shape[0]
  W = LRU_WIDTH
  xa = pa_ref[:, 0:W]
  row = lax.broadcasted_iota(jnp.int32, (S, W), 0)

  xc = cb_ref[...] + cw_ref[2:3, :] * xa
  for k in (0, 1, 3):
    d = k - CONV_WIDTH // 2
    shifted = pltpu.roll(xa, (-d) % S, 0)
    valid = (row >= -d) if d < 0 else (row < S - d)
    xc = xc + cw_ref[k:k + 1, :] * jnp.where(valid, shifted, 0.0)

  z = jnp.dot(xc.astype(BF16), wg_ref[...], preferred_element_type=F32) + bg_ref[...]
  lam = lam_ref[...]
  sp = jnp.maximum(-lam, 0.0) + jnp.log1p(jnp.exp(-jnp.abs(lam)))
  for d, (a_ref, b_ref) in enumerate(((af_ref, bf_ref), (ab_ref, bb_ref))):
    r = _sigmoid(z[:, d * 2 * W:d * 2 * W + W])
    i = _sigmoid(z[:, d * 2 * W + W:(d + 1) * 2 * W])
    log_a = (-LRU_C) * r * sp[d:d + 1, :]
    a = jnp.exp(log_a)
    a_ref[...] = a
    y2 = 2.0 * log_a
    u = a * a
    near = jnp.where(u == 1.0, y2, (u - 1.0) * y2 / jnp.log(u))
    em1 = jnp.where(y2 > -0.5, near, u - 1.0)
    b_ref[...] = jnp.sqrt(-em1) * (i * xc)

  n = S // SUBLANE
  r8 = lax.broadcasted_iota(jnp.int32, (SUBLANE, W), 0)

  def body(c, carry):
    cf, cb = carry
    sl = pl.ds(pl.multiple_of(c * SUBLANE, SUBLANE), SUBLANE)
    a = af_ref[sl, :]
    b = bf_ref[sl, :]
    for s in (1, 2, 4):
      m = r8 >= s
      b = jnp.where(m, a * pltpu.roll(b, s, 0) + b, b)
      a = jnp.where(m, a * pltpu.roll(a, s, 0), a)
    h = b + a * cf
    hf_ref[sl, :] = h
    cf = h[SUBLANE - 1:SUBLANE, :]

    sl = pl.ds(pl.multiple_of((n - 1 - c) * SUBLANE, SUBLANE), SUBLANE)
    a = ab_ref[sl, :]
    b = bb_ref[sl, :]
    for s in (1, 2, 4):
      m = r8 < SUBLANE - s
      b = jnp.where(m, a * pltpu.roll(b, SUBLANE - s, 0) + b, b)
      a = jnp.where(m, a * pltpu.roll(a, SUBLANE - s, 0), a)
    h = b + a * cb
    hb_ref[sl, :] = h
    cb = h[0:1, :]
    return cf, cb

  zero = jnp.zeros((1, W), F32)
  lax.fori_loop(0, n, body, (zero, zero), unroll=4)

  g = pa_ref[:, W:2 * W]
  gelu = 0.5 * g * (1.0 + jnp.tanh(math.sqrt(2.0 / math.pi) * (g + 0.044715 * (g * g * g))))
  y = (hf_ref[...] + hb_ref[...]) * gelu
  o_ref[...] = _rms(y, og_ref[...]).astype(o_ref.dtype)


def _rglru(pa, conv_w, conv_b, wg, bg, lam, out_g, B, S):
  T = B * S
  W = LRU_WIDTH
  const = lambda shape: pl.BlockSpec(shape, lambda b: (0,) * len(shape))
  return pl.pallas_call(
      _rglru_kernel,
      grid=(B,),
      in_specs=[
          pl.BlockSpec((S, PA_W), lambda b: (b, 0)),
          const((CONV_WIDTH, W)), const((1, W)), const((W, 4 * W)), const((1, 4 * W)),
          const((2, W)), const((1, W)),
      ],
      out_specs=pl.BlockSpec((S, W), lambda b: (b, 0)),
      out_shape=jax.ShapeDtypeStruct((T, W), BF16),
      scratch_shapes=[pltpu.VMEM((S, W), F32) for _ in range(6)],
      compiler_params=_cparams(("parallel",)),
      name="rglru",
  )(pa, conv_w, conv_b, wg, bg, lam, out_g)


_HG_LEVELS = (1, 2, 4, 8, 16, 32)
HG_SAFE_LOG = -80.0


def _bcast_row_in_blocks(x, block, row):
  L, C = x.shape
  x3 = x.reshape(L // block, block, C)
  return jnp.broadcast_to(x3[:, row:row + 1, :], x3.shape).reshape(L, C)


def _inc_prefix_products(f, row):
  L = f.shape[0]
  out = {1: f}
  p2 = f * jnp.where((row & 1) == 1, pltpu.roll(f, 1, 0), 1.0)
  out[2] = p2
  r4 = row & 3
  p4 = p2 * jnp.where(r4 == 2, pltpu.roll(p2, 1, 0), jnp.where(r4 == 3, pltpu.roll(p2, 2, 0), 1.0))
  out[4] = p4
  p, h = p4, 4
  while h < L:
    p = p * jnp.where((row & h) != 0, _bcast_row_in_blocks(p, 2 * h, h - 1), 1.0)
    h *= 2
    out[h] = p
  return out


def _inc_suffix_products(f, row):
  L = f.shape[0]
  up = lambda x, s: pltpu.roll(x, L - s, 0)
  out = {1: f}
  p2 = f * jnp.where((row & 1) == 0, up(f, 1), 1.0)
  out[2] = p2
  r4 = row & 3
  p4 = p2 * jnp.where(r4 == 1, up(p2, 1), jnp.where(r4 == 0, up(p2, 2), 1.0))
  out[4] = p4
  p, h = p4, 4
  while h < L:
    p = p * jnp.where((row & h) == 0, _bcast_row_in_blocks(p, 2 * h, h), 1.0)
    h *= 2
    out[h] = p
  return out


def _hg_chunk(q, f, v, st_ref, masks_ref, hmask, row, reverse):
  L = q.shape[0]
  k = 1.0 - f
  pre = _inc_prefix_products(f, row)
  suf = _inc_suffix_products(f, row)
  if not reverse:
    q_dec = pre
    k_dec = {h: (jnp.where((row & (h - 1)) == h - 1, 1.0, pltpu.roll(suf[h], L - 1, 0)) if h > 1 else None)
             for h in suf}
  else:
    q_dec = suf
    k_dec = {h: (jnp.where((row & (h - 1)) == 0, 1.0, pltpu.roll(pre[h], 1, 0)) if h > 1 else None)
             for h in pre}

  def head_blockdiag(x):
    return jnp.concatenate([x] * HG_HEADS, axis=0) * hmask

  nt = (((1,), (1,)), ((), ()))
  kb = k.astype(BF16)
  p = masks_ref[0] * lax.dot_general(q.astype(BF16), head_blockdiag(kb), nt, preferred_element_type=F32)
  for li, h in enumerate(_HG_LEVELS):
    qh = (q * q_dec[h]).astype(BF16)
    kh = kb if h == 1 else (k * k_dec[h]).astype(BF16)
    p = p + masks_ref[li + 1] * lax.dot_general(qh, head_blockdiag(kh), nt, preferred_element_type=F32)

  vb = v.astype(BF16)
  o = jnp.dot(p.astype(BF16), head_blockdiag(vb), preferred_element_type=F32)
  st = st_ref[...]
  o = o + lax.dot_general((q * q_dec[L]).astype(BF16), st.astype(BF16), nt, preferred_element_type=F32)

  ke = (k * k_dec[L]).astype(BF16)
  end = q_dec[L][0:1, :] if reverse else q_dec[L][L - 1:L, :]
  tn = (((0,), (0,)), ((), ()))
  upd = lax.dot_general(vb, ke, tn, preferred_element_type=F32)
  st_ref[...] = st * end + upd * hmask.astype(F32)
  return o


def _hg_chunk_fast(q, f, v, st_ref, masks_ref, hmask, row, reverse):
  L = q.shape[0]
  a = (_inc_suffix_products(f, row) if reverse else _inc_prefix_products(f, row))[L]
  qt = (q * a).astype(BF16)
  kt = ((1.0 - f) * (1.0 / a)).astype(BF16)

  def head_blockdiag(x):
    return jnp.concatenate([x] * HG_HEADS, axis=0) * hmask

  nt = (((1,), (1,)), ((), ()))
  tn = (((0,), (0,)), ((), ()))
  p = jnp.where(masks_ref[len(_HG_LEVELS) + 1] != 0.0,
                lax.dot_general(qt, head_blockdiag(kt), nt, preferred_element_type=F32), 0.0)
  vb = v.astype(BF16)
  st = st_ref[...]
  o = jnp.dot(p.astype(BF16), head_blockdiag(vb), preferred_element_type=F32)
  o = o + lax.dot_general(qt, st.astype(BF16), nt, preferred_element_type=F32)
  end = a[0:1, :] if reverse else a[L - 1:L, :]
  upd = lax.dot_general(vb, kt, tn, preferred_element_type=F32)
  st_ref[...] = (st + upd * hmask.astype(F32)) * end
  return o


def _hgrn2_kernel(pb_ref, lb_ref, ng_ref, mf_ref, mb_ref, hm_ref, ones_ref, o_ref,
                  q_ref, ff_ref, fb_ref, of_ref, ob_ref, sf_ref, sb_ref):
  S = pb_ref.shape[0]
  W = HG_WIDTH
  L = HG_CHUNK
  n = S // L
  row = lax.broadcasted_iota(jnp.int32, (L, W), 0)
  hmask = hm_ref[...]
  sf_ref[...] = jnp.zeros_like(sf_ref)
  sb_ref[...] = jnp.zeros_like(sb_ref)

  def full_reduce(x, op):
    return op(op(x, axis=1, keepdims=True), axis=0, keepdims=True)

  q = _silu(pb_ref[:, 0:W]) * (HG_DK ** -0.5)
  q_ref[...] = q
  margin = (jnp.log(1.0 + full_reduce(jnp.abs(q), jnp.max))
            + jnp.log(1.0 + full_reduce(jnp.abs(pb_ref[:, 3 * W:4 * W]), jnp.max)))
  log_min = None
  for d, f_ref in enumerate((ff_ref, fb_ref)):
    lb = lb_ref[d:d + 1, :]
    f = lb + (1.0 - lb) * _sigmoid(pb_ref[:, (d + 1) * W:(d + 2) * W])
    f_ref[...] = f
    chunk_log = jnp.sum(jnp.log(f).reshape(n, L, W), axis=1)
    m = full_reduce(chunk_log, jnp.min)
    log_min = m if log_min is None else jnp.minimum(log_min, m)
  in_range = (log_min - margin)[0, 0] >= HG_SAFE_LOG

  def make_body(chunk_fn):
    def body(c, _):
      for f_ref, out_ref, st_ref, m_ref, rev, cc in ((ff_ref, of_ref, sf_ref, mf_ref, False, c),
                                                     (fb_ref, ob_ref, sb_ref, mb_ref, True, n - 1 - c)):
        sl = pl.ds(pl.multiple_of(cc * L, L), L)
        out_ref[sl, :] = chunk_fn(q_ref[sl, :], f_ref[sl, :], pb_ref[sl, 3 * W:4 * W], st_ref, m_ref,
                                  hmask, row, rev)
      return 0
    return body

  @pl.when(in_range)
  def _():
    lax.fori_loop(0, n, make_body(_hg_chunk_fast), 0, unroll=4)

  @pl.when(jnp.logical_not(in_range))
  def _():
    lax.fori_loop(0, n, make_body(_hg_chunk), 0)

  o = of_ref[...] + ob_ref[...]
  ms = jnp.dot((o * o).astype(BF16), ones_ref[...], preferred_element_type=F32) * (1.0 / HG_DK)
  y = o * lax.rsqrt(ms + EPS) * ng_ref[...]
  o_ref[...] = (y * _silu(pb_ref[:, 4 * W:5 * W])).astype(o_ref.dtype)


def _hg_constants():
  L = HG_CHUNK
  t = jnp.arange(L)[:, None]
  s = jnp.arange(L)[None, :]
  fwd, bwd = [t == s], [t == s]
  for h in _HG_LEVELS:
    same = (t // (2 * h)) == (s // (2 * h))
    fwd.append(same & ((t & h) != 0) & ((s & h) == 0))
    bwd.append(same & ((t & h) == 0) & ((s & h) != 0))
  fwd.append(t >= s)
  bwd.append(t <= s)
  tile = lambda m: jnp.tile(jnp.stack(m).astype(F32), (1, 1, HG_HEADS))
  hid = jnp.arange(HG_WIDTH) // HG_DK
  hmask = (hid[:, None] == hid[None, :])
  return tile(fwd), tile(bwd), hmask.astype(BF16), hmask.astype(BF16)


def _hgrn2(pb, lb, norm_g, B, S):
  T = B * S
  W = HG_WIDTH
  mf, mb, hmask, ones = _hg_constants()
  const = lambda shape: pl.BlockSpec(shape, lambda b: (0,) * len(shape))
  return pl.pallas_call(
      _hgrn2_kernel,
      grid=(B,),
      in_specs=[
          pl.BlockSpec((S, PB_W), lambda b: (b, 0)),
          const((2, W)), const((1, W)), const(mf.shape), const(mb.shape), const((W, W)), const((W, W)),
      ],
      out_specs=pl.BlockSpec((S, W), lambda b: (b, 0)),
      out_shape=jax.ShapeDtypeStruct((T, W), BF16),
      scratch_shapes=[pltpu.VMEM((S, W), F32) for _ in range(5)]
                     + [pltpu.VMEM((W, W), F32), pltpu.VMEM((W, W), F32)],
      compiler_params=_cparams(("parallel",)),
      name="hgrn2",
  )(pb, lb, norm_g, mf, mb, hmask, ones)


ROPE_HALF = MLA_ROPE // 2
ATTN_MAX_BOUND = 60.0
ATTN_KEY_BLOCK = 256
ATTN_Q_BLOCK = 1024
MLA_PREP_ROWS = 1024


def _rope_tables_kernel(pos_ref, invf_ref, sgn_ref, cos_ref, sin_ref):
  ang = pos_ref[...].astype(F32) * invf_ref[...]
  cos_ref[...] = jnp.cos(ang)
  sin_ref[...] = sgn_ref[...] * jnp.sin(ang)


def _rope_tables(pos2, tm):
  T = pos2.shape[0]
  inv_freq = (ROPE_THETA ** (-jnp.arange(0, ROPE_HALF, dtype=F32) * (2.0 / MLA_ROPE))).reshape(1, ROPE_HALF)
  ones = jnp.ones((1, ROPE_HALF), F32)
  no_nope = jnp.zeros((1, MLA_NOPE), F32)
  invf = _slot_lanes(no_nope, jnp.concatenate([inv_freq, inv_freq], axis=1))
  sgn = _slot_lanes(no_nope, jnp.concatenate([-ones, ones], axis=1))
  row = pl.BlockSpec((1, HEAD_SLOT), lambda i: (0, 0))
  return pl.pallas_call(
      _rope_tables_kernel,
      grid=(T // tm,),
      in_specs=[pl.BlockSpec((tm, 1), lambda i: (i, 0)), row, row],
      out_specs=[pl.BlockSpec((tm, HEAD_SLOT), lambda i: (i, 0))] * 2,
      out_shape=[jax.ShapeDtypeStruct((T, HEAD_SLOT), F32)] * 2,
      compiler_params=_cparams(("parallel",)),
      name="rope_tables",
  )(pos2, invf, sgn)


def _rope(t, cos_t, sin_t):
  return t * cos_t + pltpu.roll(t, HEAD_SLOT // 2, 1) * sin_t


def _mla_prep_kernel(pc_ref, cos_ref, sin_ref, qn_ref, kvn_ref, wq_ref, wk_ref, wv_ref, vone_ref, gq_ref, gk_ref,
                     q_ref, k_ref, v_ref):
  cos_t = cos_ref[...]
  sin_t = sin_ref[...]
  cq = _rms(pc_ref[:, 0:MLA_Q_RANK], qn_ref[...]).astype(BF16)
  ckv = _rms(pc_ref[:, MLA_Q_RANK:MLA_Q_RANK + MLA_KV_RANK], kvn_ref[...]).astype(BF16)
  kr_slot = pc_ref[:, MLA_Q_RANK + MLA_KV_RANK:]
  q_all = jnp.dot(cq, wq_ref[...], preferred_element_type=F32)
  k_all = jnp.dot(ckv, wk_ref[...], preferred_element_type=F32)
  v_ref[...] = (jnp.dot(ckv, wv_ref[...], preferred_element_type=F32) + vone_ref[...]).astype(v_ref.dtype)
  for h in range(MLA_HEADS):
    sl = slice(h * HEAD_SLOT, (h + 1) * HEAD_SLOT)
    qh = _rope(_rms(q_all[:, sl], gq_ref[...], MLA_QK), cos_t, sin_t)
    q_ref[h] = qh.astype(q_ref.dtype)
    kh = _rope(_rms(k_all[:, sl] + kr_slot, gk_ref[...], MLA_QK), cos_t, sin_t)
    k_ref[h] = kh.astype(k_ref.dtype)


def _mla_prep(pc, cos_t, sin_t, q_norm, kv_norm, wq, wk, wv, vone, gq, gk, B, S, ts):
  H = MLA_HEADS
  nb = S // ts
  const = lambda shape: pl.BlockSpec(shape, lambda b, s: (0,) * len(shape))
  tok = lambda w: pl.BlockSpec((ts, w), lambda b, s: (b * nb + s, 0))
  return pl.pallas_call(
      _mla_prep_kernel,
      grid=(B, nb),
      in_specs=[
          tok(PC_W), tok(HEAD_SLOT), tok(HEAD_SLOT),
          const((1, MLA_Q_RANK)), const((1, MLA_KV_RANK)),
          const(wq.shape), const(wk.shape), const(wv.shape), const(vone.shape),
          const((1, HEAD_SLOT)), const((1, HEAD_SLOT)),
      ],
      out_specs=[
          pl.BlockSpec((None, H, ts, HEAD_SLOT), lambda b, s: (b, 0, s, 0)),
          pl.BlockSpec((None, H, ts, HEAD_SLOT), lambda b, s: (b, 0, s, 0)),
          pl.BlockSpec((ts, H * HEAD_SLOT), lambda b, s: (b * nb + s, 0)),
      ],
      out_shape=[
          jax.ShapeDtypeStruct((B, H, S, HEAD_SLOT), BF16),
          jax.ShapeDtypeStruct((B, H, S, HEAD_SLOT), BF16),
          jax.ShapeDtypeStruct((B * S, H * HEAD_SLOT), BF16),
      ],
      compiler_params=_cparams(("parallel", "parallel")),
      name="mla_prep",
  )(pc, cos_t, sin_t, q_norm, kv_norm, wq, wk, wv, vone, gq, gk)


def _attention_kernel(bound_ref, q_ref, k_ref, v_ref, o_ref):
  tq = q_ref.shape[1]
  S = k_ref.shape[1]
  bound = bound_ref[0]
  nt = (((1,), (1,)), ((), ()))

  def finish(res):
    lane = lax.broadcasted_iota(jnp.int32, (tq, HEAD_SLOT), 1)
    even = res[0] * (1.0 / res[0][:, MLA_V:MLA_V + 1])
    odd = res[1] * (1.0 / res[1][:, 0:1])
    o_ref[...] = jnp.where(lane < MLA_V, even, odd).astype(o_ref.dtype)

  @pl.when(bound <= ATTN_MAX_BOUND)
  def _():
    res = []
    for h in range(2):
      acc = None
      for j in range(S // ATTN_KEY_BLOCK):
        ks = slice(j * ATTN_KEY_BLOCK, (j + 1) * ATTN_KEY_BLOCK)
        s = lax.dot_general(q_ref[h], k_ref[h, ks, :], nt, preferred_element_type=F32)
        p = jnp.exp2(s - bound).astype(BF16)
        c = jnp.dot(p, v_ref[ks, h * HEAD_SLOT:(h + 1) * HEAD_SLOT], preferred_element_type=F32)
        acc = c if acc is None else acc + c
      res.append(acc)
    finish(res)

  @pl.when(bound > ATTN_MAX_BOUND)
  def _():
    res = []
    for h in range(2):
      s = lax.dot_general(q_ref[h], k_ref[h], nt, preferred_element_type=F32)
      p = jnp.exp2(s - jnp.max(s, axis=-1, keepdims=True)).astype(BF16)
      res.append(jnp.dot(p, v_ref[:, h * HEAD_SLOT:(h + 1) * HEAD_SLOT], preferred_element_type=F32))
    finish(res)


def _attention(bound, q, k, v, B, S, tq):
  H = MLA_HEADS
  nq = S // tq
  return pl.pallas_call(
      _attention_kernel,
      grid=(B, H // 2, nq),
      in_specs=[
          pl.BlockSpec(memory_space=pltpu.SMEM),
          pl.BlockSpec((None, 2, tq, HEAD_SLOT), lambda b, hp, i: (b, hp, i, 0)),
          pl.BlockSpec((None, 2, S, HEAD_SLOT), lambda b, hp, i: (b, hp, 0, 0)),
          pl.BlockSpec((S, 2 * HEAD_SLOT), lambda b, hp, i: (b, hp)),
      ],
      out_specs=pl.BlockSpec((tq, 2 * MLA_V), lambda b, hp, i: (b * nq + i, hp)),
      out_shape=jax.ShapeDtypeStruct((B * S, MLA_WIDTH), F32),
      compiler_params=_cparams(("parallel", "parallel", "parallel")),
      name="mla_attention",
  )(bound, q, k, v)


def _outproj_kernel(x_ref, ya_ref, yb_ref, yc_ref, gc_ref, w_ref, o_ref):
  yc = _rms(yc_ref[...], gc_ref[...]).astype(BF16)
  a0, a1 = LRU_WIDTH, LRU_WIDTH + HG_WIDTH
  acc = jnp.dot(ya_ref[...], w_ref[0:a0, :], preferred_element_type=F32)
  acc = acc + jnp.dot(yb_ref[...], w_ref[a0:a1, :], preferred_element_type=F32)
  acc = acc + jnp.dot(yc, w_ref[a1:, :], preferred_element_type=F32)
  o_ref[...] = x_ref[...] + acc


def _outproj(x2, ya, yb, yc, gc, w, tm):
  T = x2.shape[0]
  tok = lambda wd: pl.BlockSpec((tm, wd), lambda i: (i, 0))
  return pl.pallas_call(
      _outproj_kernel,
      grid=(T // tm,),
      in_specs=[tok(D_MODEL), tok(LRU_WIDTH), tok(HG_WIDTH), tok(MLA_WIDTH),
                pl.BlockSpec((1, MLA_WIDTH), lambda i: (0, 0)),
                pl.BlockSpec((D_MODEL, D_MODEL), lambda i: (0, 0))],
      out_specs=tok(D_MODEL),
      out_shape=jax.ShapeDtypeStruct((T, D_MODEL), F32),
      compiler_params=_cparams(("parallel",)),
      name="outproj",
  )(x2, ya, yb, yc, gc, w)


def _ffn_kernel(x_ref, g_ref, wg_ref, wu_ref, wd_ref, o_ref):
  x = x_ref[...]
  h = _rms(x, g_ref[...]).astype(BF16)
  F = wg_ref.shape[1]
  acc = x
  for c0 in range(0, F, FFN_CHUNK):
    c1 = min(c0 + FFN_CHUNK, F)
    g = jnp.dot(h, wg_ref[:, c0:c1], preferred_element_type=F32)
    u = jnp.dot(h, wu_ref[:, c0:c1], preferred_element_type=F32)
    a = (_silu(g) * u).astype(BF16)
    acc = acc + jnp.dot(a, wd_ref[c0:c1, :], preferred_element_type=F32)
  o_ref[...] = acc


def _ffn(x2, gain, wg, wu, wd, tm):
  T = x2.shape[0]
  F = wg.shape[1]
  resident = lambda shape: pl.BlockSpec(shape, lambda i: (0, 0), pipeline_mode=pl.Buffered(1))
  return pl.pallas_call(
      _ffn_kernel,
      grid=(T // tm,),
      in_specs=[
          pl.BlockSpec((tm, D_MODEL), lambda i: (i, 0)),
          pl.BlockSpec((1, D_MODEL), lambda i: (0, 0)),
          resident((D_MODEL, F)), resident((D_MODEL, F)), resident((F, D_MODEL)),
      ],
      out_specs=pl.BlockSpec((tm, D_MODEL), lambda i: (i, 0)),
      out_shape=jax.ShapeDtypeStruct((T, D_MODEL), F32),
      compiler_params=_cparams(("parallel",)),
      name="ffn_dense",
  )(x2, gain, wg, wu, wd)


def _router_kernel(x_ref, g_ref, wr_ref, sel_ref):
  def split(v):
    hi = v.astype(BF16)
    return hi, (v - hi.astype(F32)).astype(BF16)

  h_hi, h_lo = split(_rms(x_ref[...], g_ref[...]))
  w_hi, w_lo = split(wr_ref[...])
  logits = (jnp.dot(h_hi, w_hi, preferred_element_type=F32)
            + jnp.dot(h_lo, w_hi, preferred_element_type=F32)
            + jnp.dot(h_hi, w_lo, preferred_element_type=F32))
  lane = lax.broadcasted_iota(jnp.int32, logits.shape, 1)
  neg = jnp.float32(-jnp.inf)
  lg = jnp.where(lane < N_EXPERTS, logits, neg)
  m1 = jnp.max(lg, axis=-1, keepdims=True)
  i1 = jnp.min(jnp.where(lg == m1, lane, LANE), axis=-1, keepdims=True)
  lg2 = jnp.where(lane == i1, neg, lg)
  m2 = jnp.max(lg2, axis=-1, keepdims=True)
  i2 = jnp.min(jnp.where(lg2 == m2, lane, LANE), axis=-1, keepdims=True)
  e2 = jnp.exp(m2 - m1)
  w1 = 1.0 / (1.0 + e2)
  w2 = e2 / (1.0 + e2)
  sel_ref[...] = jnp.where(lane == 0, i1.astype(F32), jnp.where(lane == 1, i2.astype(F32),
                           jnp.where(lane == 2, w1, jnp.where(lane == 3, w2, 0.0))))


def _router(x2, gain, wr, tm):
  T = x2.shape[0]
  return pl.pallas_call(
      _router_kernel,
      grid=(T // tm,),
      in_specs=[pl.BlockSpec((tm, D_MODEL), lambda i: (i, 0)),
                pl.BlockSpec((1, D_MODEL), lambda i: (0, 0)),
                pl.BlockSpec((D_MODEL, LANE), lambda i: (0, 0))],
      out_specs=pl.BlockSpec((tm, LANE), lambda i: (i, 0)),
      out_shape=jax.ShapeDtypeStruct((T, LANE), F32),
      compiler_params=_cparams(("parallel",)),
      name="moe_router",
  )(x2, gain, wr)


def _route_plan(sel, tm):
  T = sel.shape[0]
  e_flat = sel[:, 0:2].astype(jnp.int32).reshape(1, 2 * T)
  onehot = (e_flat == jnp.arange(N_EXPERTS, dtype=jnp.int32)[:, None]).astype(jnp.int32)
  csum = jnp.cumsum(onehot, axis=1)
  counts = csum[:, -1]
  tiles_e = (counts + tm - 1) // tm
  tile_end = jnp.cumsum(tiles_e)
  offs = (tile_end - tiles_e) * tm
  pos = jnp.sum(onehot * (csum - 1 + offs[:, None]), axis=0)
  n_tiles = (2 * T) // tm + N_EXPERTS
  tile_expert = jnp.sum(jnp.arange(n_tiles, dtype=jnp.int32)[:, None] >= tile_end[None, :], axis=1)
  tile_expert = jnp.minimum(tile_expert, N_EXPERTS - 1).astype(jnp.int32)
  last_tile = jnp.where(tiles_e > 0, offs + (tiles_e - 1) * tm, -1)
  spare = tile_end[-1] + jnp.arange(N_EXPERTS, dtype=jnp.int32)
  spare_tile = jnp.where(spare < n_tiles, spare * tm, -1)
  clear = jnp.concatenate([last_tile, spare_tile]).astype(jnp.int32)
  return pos.astype(jnp.int32), tile_expert, tile_end[-1:].astype(jnp.int32), clear, n_tiles


def _row_copy(src, dst, sem, rows=1, src_row=0, dst_row=0):
  return pltpu.make_async_copy(src.at[pl.ds(src_row, rows), :], dst.at[pl.ds(dst_row, rows), :], sem)


def _dispatch_kernel(pos_ref, lt_ref, x_ref, xs_hbm, zbuf, sem, zsem, *, tm):
  tt = x_ref.shape[0]
  zr = zbuf.shape[0]

  @pl.when(pl.program_id(0) == 0)
  def _():
    zbuf[...] = jnp.zeros_like(zbuf)
    for e in range(lt_ref.shape[0]):
      @pl.when(lt_ref[e] >= 0)
      def _():
        rows = [pl.multiple_of(lt_ref[e] + c * zr, SUBLANE) for c in range(tm // zr)]
        for r0 in rows:
          _row_copy(zbuf, xs_hbm, zsem, zr, 0, r0).start()
        for r0 in rows:
          _row_copy(zbuf, xs_hbm, zsem, zr, 0, r0).wait()

  def body(r, _):
    for c in range(2):
      _row_copy(x_ref, xs_hbm, sem, 1, r, pos_ref[0, 0, 2 * r + c]).start(priority=c)
    return 0

  lax.fori_loop(0, tt, body, 0, unroll=8)
  for c in range(2):
    _row_copy(x_ref, xs_hbm, sem, tt).wait()


def _dispatch(x2, pos, last_tile, n_rows, tt, tm):
  T = x2.shape[0]
  pos3 = pos.reshape(T // tt, 1, 2 * tt)
  zr = min(MOE_ZERO_ROWS, tm)
  return pl.pallas_call(
      functools.partial(_dispatch_kernel, tm=tm),
      grid=(T // tt,),
      in_specs=[pl.BlockSpec((1, 1, 2 * tt), lambda i: (i, 0, 0), memory_space=pltpu.SMEM),
                pl.BlockSpec(memory_space=pltpu.SMEM),
                pl.BlockSpec((tt, D_MODEL), lambda i: (i, 0))],
      out_specs=pl.BlockSpec(memory_space=pl.ANY),
      out_shape=jax.ShapeDtypeStruct((n_rows, D_MODEL), F32),
      scratch_shapes=[pltpu.VMEM((zr, D_MODEL), F32), pltpu.SemaphoreType.DMA(()), pltpu.SemaphoreType.DMA(())],
      compiler_params=_cparams(("arbitrary",)),
      name="moe_dispatch",
  )(pos3, last_tile, x2)


def _moe_group_kernel(te_ref, nu_ref, xs_ref, g_ref, w1_ref, w3_ref, w2_ref, ys_ref, h_ref, acc_ref):
  del te_ref
  i = pl.program_id(0)
  f = pl.program_id(1)
  last = pl.num_programs(1) - 1
  used = i < nu_ref[0]

  @pl.when(used & (f == 0))
  def _():
    h_ref[...] = _rms(xs_ref[...], g_ref[...]).astype(BF16)
    acc_ref[...] = jnp.zeros_like(acc_ref)

  @pl.when(used)
  def _():
    h = h_ref[...]
    g = jnp.dot(h, w1_ref[...].astype(BF16), preferred_element_type=F32)
    u = jnp.dot(h, w3_ref[...].astype(BF16), preferred_element_type=F32)
    a = (_silu(g) * u).astype(BF16)
    acc_ref[...] += jnp.dot(a, w2_ref[...].astype(BF16), preferred_element_type=F32)

  @pl.when(used & (f == last))
  def _():
    ys_ref[...] = acc_ref[...]

  @pl.when(jnp.logical_not(used) & (f == last))
  def _():
    ys_ref[...] = jnp.zeros_like(ys_ref)


def _moe_group(xs, gain, tile_expert, n_used, w1, w3, w2, tm, tf):
  P = xs.shape[0]
  F = w1.shape[2]
  nf = F // tf
  fidx = lambda i, f, te, nu: jnp.where(i < nu[0], f, nf - 1)
  grid_spec = pltpu.PrefetchScalarGridSpec(
      num_scalar_prefetch=2,
      grid=(P // tm, nf),
      in_specs=[
          pl.BlockSpec((tm, D_MODEL), lambda i, f, te, nu: (jnp.minimum(i, nu[0] - 1), 0)),
          pl.BlockSpec((1, D_MODEL), lambda i, f, te, nu: (0, 0)),
          pl.BlockSpec((None, D_MODEL, tf), lambda i, f, te, nu: (te[i], 0, fidx(i, f, te, nu))),
          pl.BlockSpec((None, D_MODEL, tf), lambda i, f, te, nu: (te[i], 0, fidx(i, f, te, nu))),
          pl.BlockSpec((None, tf, D_MODEL), lambda i, f, te, nu: (te[i], fidx(i, f, te, nu), 0)),
      ],
      out_specs=pl.BlockSpec((tm, D_MODEL), lambda i, f, te, nu: (i, 0)),
      scratch_shapes=[pltpu.VMEM((tm, D_MODEL), BF16), pltpu.VMEM((tm, D_MODEL), F32)],
  )
  return pl.pallas_call(
      _moe_group_kernel,
      grid_spec=grid_spec,
      out_shape=jax.ShapeDtypeStruct((P, D_MODEL), F32),
      compiler_params=_cparams(("arbitrary", "arbitrary")),
      name="moe_group",
  )(tile_expert, n_used, xs, gain, w1, w3, w2)


def _combine_kernel(pos_ref, x_ref, sel_ref, ys_hbm, o_ref, ybuf, sem):
  tt = x_ref.shape[0]

  def body(r, _):
    for c in range(2):
      _row_copy(ys_hbm, ybuf.at[c], sem, 1, pos_ref[0, 0, 2 * r + c], r).start(priority=c)
    return 0

  lax.fori_loop(0, tt, body, 0, unroll=8)
  for c in range(2):
    _row_copy(ys_hbm, ybuf.at[c], sem, tt).wait()
  sel = sel_ref[...]
  o_ref[...] = x_ref[...] + sel[:, 2:3] * ybuf[0] + sel[:, 3:4] * ybuf[1]


def _combine(x2, sel, pos, ys, tt):
  T = x2.shape[0]
  pos3 = pos.reshape(T // tt, 1, 2 * tt)
  return pl.pallas_call(
      _combine_kernel,
      grid=(T // tt,),
      in_specs=[pl.BlockSpec((1, 1, 2 * tt), lambda i: (i, 0, 0), memory_space=pltpu.SMEM),
                pl.BlockSpec((tt, D_MODEL), lambda i: (i, 0)),
                pl.BlockSpec((tt, LANE), lambda i: (i, 0)),
                pl.BlockSpec(memory_space=pl.ANY)],
      out_specs=pl.BlockSpec((tt, D_MODEL), lambda i: (i, 0)),
      out_shape=jax.ShapeDtypeStruct((T, D_MODEL), F32),
      scratch_shapes=[pltpu.VMEM((2, tt, D_MODEL), F32), pltpu.SemaphoreType.DMA(())],
      compiler_params=_cparams(("arbitrary",)),
      name="moe_combine",
  )(pos3, x2, sel, ys)


def _block_diag(w):
  n, d, e = w.shape
  eye = jnp.eye(n, dtype=w.dtype)
  return (eye[:, None, :, None] * w[:, :, None, :]).reshape(n * d, n * e)


def _slot_lanes(nope, rope):
  z = lambda n: jnp.zeros(nope.shape[:-1] + (n,), nope.dtype)
  cut = HEAD_SLOT // 2 - ROPE_HALF
  x1, x2 = (z(ROPE_HALF), z(ROPE_HALF)) if rope is None else (rope[..., :ROPE_HALF], rope[..., ROPE_HALF:])
  return jnp.concatenate([x1, nope[..., :cut], x2, nope[..., cut:], z(HEAD_SLOT - MLA_QK)], axis=-1)


def _head_slots(w, width):
  r = w.shape[0]
  w3 = w.reshape(r, MLA_HEADS, width)
  rope = w3[..., MLA_NOPE:] if width == MLA_QK else None
  return _slot_lanes(w3[..., :MLA_NOPE], rope).reshape(r, MLA_HEADS * HEAD_SLOT)


def _slot_gain(g):
  g = g.astype(F32).reshape(1, MLA_QK)
  return _slot_lanes(g[:, :MLA_NOPE], g[:, MLA_NOPE:])


def _value_slots(wv):
  r = wv.shape[0]
  w4 = wv.reshape(r, MLA_HEADS // 2, 2, MLA_V)
  z = jnp.zeros((r, MLA_HEADS // 2, HEAD_SLOT - MLA_V), wv.dtype)
  out = jnp.stack([jnp.concatenate([w4[:, :, 0], z], axis=-1), jnp.concatenate([z, w4[:, :, 1]], axis=-1)], axis=2)
  one = np.zeros((1, MLA_HEADS // 2, 2, HEAD_SLOT), np.float32)
  one[:, :, 0, MLA_V] = 1.0
  one[:, :, 1, 0] = 1.0
  return out.reshape(r, MLA_HEADS * HEAD_SLOT), jnp.asarray(one.reshape(1, MLA_HEADS * HEAD_SLOT))


def _score_bound(gq, gk):
  b = MLA_QK ** 0.5 * math.log2(math.e) * 1.02 * jnp.max(jnp.abs(gq)) * jnp.max(jnp.abs(gk))
  return b.astype(F32).reshape(1)


def kernel(x, positions, norm_mix, w_in, conv_w, conv_b, lru_wa, lru_ba, lru_wx, lru_bx, lru_lam,
           out_g_a, hg_lb_logits, hg_norm_g, mla_q_norm, mla_w_uq, mla_kv_norm, mla_w_ukv,
           qk_norm_q, qk_norm_k, out_g_c, w_out, norm_ffn, ffn_w_gate_up, ffn_w_down,
           moe_router, moe_w1, moe_w3, moe_w2):
  B, S, D = x.shape
  T = B * S
  depth = w_in.shape[0]
  tm = 512
  row = lambda v: v.astype(F32).reshape(1, -1)

  p = jax.nn.softmax(hg_lb_logits.astype(F32), axis=0)
  lower_bounds = jnp.cumsum(p, axis=0) - p[0:1]

  x2 = x.reshape(T, D)
  cos_t, sin_t = _rope_tables(positions.reshape(T, 1), tm)

  for l in range(depth):
    n_main = w_in.shape[2] - MLA_ROPE
    kr_slot_w = _slot_lanes(jnp.zeros((D, MLA_NOPE), F32), w_in[l][:, n_main:])
    w_in_l = jnp.concatenate([w_in[l][:, :n_main], kr_slot_w], axis=1).astype(BF16)
    wg_lru = jnp.concatenate(
        [_block_diag(lru_wa[l, 0]), _block_diag(lru_wx[l, 0]),
         _block_diag(lru_wa[l, 1]), _block_diag(lru_wx[l, 1])], axis=1).astype(BF16)
    bg_lru = jnp.concatenate([lru_ba[l, 0], lru_bx[l, 0], lru_ba[l, 1], lru_bx[l, 1]]).reshape(1, -1)
    wq = _head_slots(mla_w_uq[l], MLA_QK).astype(BF16)
    wkv = mla_w_ukv[l].reshape(MLA_KV_RANK, MLA_HEADS, MLA_NOPE + MLA_V)
    wk = _head_slots(wkv[:, :, :MLA_NOPE].reshape(MLA_KV_RANK, -1), MLA_NOPE).astype(BF16)
    wv, vone = _value_slots(wkv[:, :, MLA_NOPE:].reshape(MLA_KV_RANK, MLA_WIDTH))
    wv = wv.astype(BF16)

    pa, pb, pc = _norm_inproj(x2, row(norm_mix[l]), w_in_l, tm)
    ya = _rglru(pa, conv_w[l], row(conv_b[l]), wg_lru, bg_lru, lru_lam[l], row(out_g_a[l]), B, S)
    yb = _hgrn2(pb, lower_bounds[l], row(jnp.tile(hg_norm_g[l], HG_HEADS)), B, S)
    q, k, v = _mla_prep(pc, cos_t, sin_t, row(mla_q_norm[l]), row(mla_kv_norm[l]), wq, wk, wv, vone,
                        _slot_gain(qk_norm_q[l]) * (MLA_QK ** -0.5 * math.log2(math.e)),
                        _slot_gain(qk_norm_k[l]), B, S, min(MLA_PREP_ROWS, S))
    yc = _attention(_score_bound(qk_norm_q[l], qk_norm_k[l]), q, k, v, B, S, min(ATTN_Q_BLOCK, S))
    x2 = _outproj(x2, ya, yb, yc, row(out_g_c[l]), w_out[l].astype(BF16), tm)

    if l % 2 == 0:
      wgu = ffn_w_gate_up[l // 2]
      pad = D_FF_PAD - D_FF
      wg = jnp.pad(wgu[:, :D_FF], ((0, 0), (0, pad))).astype(BF16)
      wu = jnp.pad(wgu[:, D_FF:], ((0, 0), (0, pad))).astype(BF16)
      wd = jnp.pad(ffn_w_down[l // 2], ((0, pad), (0, 0))).astype(BF16)
      x2 = _ffn(x2, row(norm_ffn[l]), wg, wu, wd, tm)
    else:
      wr = jnp.pad(moe_router[l // 2].astype(F32), ((0, 0), (0, LANE - N_EXPERTS)))
      gain = row(norm_ffn[l])
      tme = 1024 if T >= 8192 else 128
      sel = _router(x2, gain, wr, tm)
      pos, tile_expert, n_used, last_tile, n_tiles = _route_plan(sel, tme)
      xs = _dispatch(x2, pos, last_tile, n_tiles * tme, tm, tme)
      ys = _moe_group(xs, gain, tile_expert, n_used, moe_w1[l // 2], moe_w3[l // 2], moe_w2[l // 2],
                      tme, D_EXPERT // 7)
      x2 = _combine(x2, sel, pos, ys, tm)
  return x2.reshape(B, S, D)
```

```python
import functools
import math

import jax
import jax.numpy as jnp
import numpy as np
from jax import lax
from jax.experimental import pallas as pl
from jax.experimental.pallas import tpu as pltpu

F32 = jnp.float32
BF16 = jnp.bfloat16

D_MODEL = 1024
LRU_WIDTH = 256
LRU_BLOCKS = 4
CONV_WIDTH = 4
LRU_C = 8.0
HG_HEADS = 4
HG_DK = 64
HG_WIDTH = 256
MLA_HEADS = 8
MLA_NOPE = 64
MLA_ROPE = 32
MLA_V = 64
MLA_QK = MLA_NOPE + MLA_ROPE
MLA_Q_RANK = 256
MLA_KV_RANK = 128
MLA_WIDTH = MLA_HEADS * MLA_V
ROPE_THETA = 10000.0
D_FF = 2752
N_EXPERTS = 8
D_EXPERT = 3584
EPS = 1e-6

LANE = 128
SUBLANE = 8
HEAD_SLOT = 128
HG_CHUNK = 64
D_FF_PAD = 2816
MOE_ZERO_ROWS = 256
FFN_CHUNK = 1024
VMEM_LIMIT = 56 * 1024 * 1024

PA_W, PB_W, PC_W = 512, 1280, 512


def _cparams(sem):
  return pltpu.CompilerParams(dimension_semantics=sem, vmem_limit_bytes=VMEM_LIMIT)


def _sigmoid(x):
  return 0.5 + 0.5 * jnp.tanh(0.5 * x)


def _silu(x):
  return x * _sigmoid(x)


def _rms(x, gain, n=None):
  n = x.shape[-1] if n is None else n
  ms = jnp.sum(x * x, axis=-1, keepdims=True) * (1.0 / n)
  return x * lax.rsqrt(ms + EPS) * gain


def _norm_inproj_kernel(x_ref, g_ref, w_ref, oa_ref, ob_ref, oc_ref):
  h = _rms(x_ref[...], g_ref[...]).astype(BF16)
  p = jnp.dot(h, w_ref[...], preferred_element_type=F32)
  oa_ref[...] = p[:, :PA_W]
  ob_ref[...] = p[:, PA_W:PA_W + PB_W]
  oc_ref[...] = p[:, PA_W + PB_W:]


def _norm_inproj(x2, gain, w, tm):
  T = x2.shape[0]
  n = w.shape[1]
  return pl.pallas_call(
      _norm_inproj_kernel,
      grid=(T // tm,),
      in_specs=[
          pl.BlockSpec((tm, D_MODEL), lambda i: (i, 0)),
          pl.BlockSpec((1, D_MODEL), lambda i: (0, 0)),
          pl.BlockSpec((D_MODEL, n), lambda i: (0, 0)),
      ],
      out_specs=[
          pl.BlockSpec((tm, PA_W), lambda i: (i, 0)),
          pl.BlockSpec((tm, PB_W), lambda i: (i, 0)),
          pl.BlockSpec((tm, PC_W), lambda i: (i, 0)),
      ],
      out_shape=[
          jax.ShapeDtypeStruct((T, PA_W), F32),
          jax.ShapeDtypeStruct((T, PB_W), F32),
          jax.ShapeDtypeStruct((T, PC_W), F32),
      ],
      compiler_params=_cparams(("parallel",)),
      name="norm_inproj",
  )(x2, gain, w)


def _rglru_kernel(pa_ref, cw_ref, cb_ref, wg_ref, bg_ref, lam_ref, og_ref, o_ref,
                  af_ref, bf_ref, ab_ref, bb_ref, hf_ref, hb_ref):
  S = pa_ref.shape[0]
  W = LRU_WIDTH
  xa = pa_ref[:, 0:W]
  row = lax.broadcasted_iota(jnp.int32, (S, W), 0)

  xc = cb_ref[...] + cw_ref[2:3, :] * xa
  for k in (0, 1, 3):
    d = k - CONV_WIDTH // 2
    shifted = pltpu.roll(xa, (-d) % S, 0)
    valid = (row >= -d) if d < 0 else (row < S - d)
    xc = xc + cw_ref[k:k + 1, :] * jnp.where(valid, shifted, 0.0)

  z = jnp.dot(xc.astype(BF16), wg_ref[...], preferred_element_type=F32) + bg_ref[...]
  lam = lam_ref[...]
  sp = jnp.maximum(-lam, 0.0) + jnp.log1p(jnp.exp(-jnp.abs(lam)))
  for d, (a_ref, b_ref) in enumerate(((af_ref, bf_ref), (ab_ref, bb_ref))):
    r = _sigmoid(z[:, d * 2 * W:d * 2 * W + W])
    i = _sigmoid(z[:, d * 2 * W + W:(d + 1) * 2 * W])
    log_a = (-LRU_C) * r * sp[d:d + 1, :]
    a = jnp.exp(log_a)
    a_ref[...] = a
    y2 = 2.0 * log_a
    u = a * a
    near = jnp.where(u == 1.0, y2, (u - 1.0) * y2 / jnp.log(u))
    em1 = jnp.where(y2 > -0.5, near, u - 1.0)
    b_ref[...] = jnp.sqrt(-em1) * (i * xc)

  n = S // SUBLANE
  r8 = lax.broadcasted_iota(jnp.int32, (SUBLANE, W), 0)

  def body(c, carry):
    cf, cb = carry
    sl = pl.ds(pl.multiple_of(c * SUBLANE, SUBLANE), SUBLANE)
    a = af_ref[sl, :]
    b = bf_ref[sl, :]
    for s in (1, 2, 4):
      m = r8 >= s
      b = jnp.where(m, a * pltpu.roll(b, s, 0) + b, b)
      a = jnp.where(m, a * pltpu.roll(a, s, 0), a)
    h = b + a * cf
    hf_ref[sl, :] = h
    cf = h[SUBLANE - 1:SUBLANE, :]

    sl = pl.ds(pl.multiple_of((n - 1 - c) * SUBLANE, SUBLANE), SUBLANE)
    a = ab_ref[sl, :]
    b = bb_ref[sl, :]
    for s in (1, 2, 4):
      m = r8 < SUBLANE - s
      b = jnp.where(m, a * pltpu.roll(b, SUBLANE - s, 0) + b, b)
      a = jnp.where(m, a * pltpu.roll(a, SUBLANE - s, 0), a)
    h = b + a * cb
    hb_ref[sl, :] = h
    cb = h[0:1, :]
    return cf, cb

  zero = jnp.zeros((1, W), F32)
  lax.fori_loop(0, n, body, (zero, zero), unroll=4)

  g = pa_ref[:, W:2 * W]
  gelu = 0.5 * g * (1.0 + jnp.tanh(math.sqrt(2.0 / math.pi) * (g + 0.044715 * (g * g * g))))
  y = (hf_ref[...] + hb_ref[...]) * gelu
  o_ref[...] = _rms(y, og_ref[...]).astype(o_ref.dtype)


def _rglru(pa, conv_w, conv_b, wg, bg, lam, out_g, B, S):
  T = B * S
  W = LRU_WIDTH
  const = lambda shape: pl.BlockSpec(shape, lambda b: (0,) * len(shape))
  return pl.pallas_call(
      _rglru_kernel,
      grid=(B,),
      in_specs=[
          pl.BlockSpec((S, PA_W), lambda b: (b, 0)),
          const((CONV_WIDTH, W)), const((1, W)), const((W, 4 * W)), const((1, 4 * W)),
          const((2, W)), const((1, W)),
      ],
      out_specs=pl.BlockSpec((S, W), lambda b: (b, 0)),
      out_shape=jax.ShapeDtypeStruct((T, W), BF16),
      scratch_shapes=[pltpu.VMEM((S, W), F32) for _ in range(6)],
      compiler_params=_cparams(("parallel",)),
      name="rglru",
  )(pa, conv_w, conv_b, wg, bg, lam, out_g)


_HG_LEVELS = (1, 2, 4, 8, 16, 32)
HG_SAFE_LOG = -80.0


def _bcast_row_in_blocks(x, block, row):
  L, C = x.shape
  x3 = x.reshape(L // block, block, C)
  return jnp.broadcast_to(x3[:, row:row + 1, :], x3.shape).reshape(L, C)


def _inc_prefix_products(f, row):
  L = f.shape[0]
  out = {1: f}
  p2 = f * jnp.where((row & 1) == 1, pltpu.roll(f, 1, 0), 1.0)
  out[2] = p2
  r4 = row & 3
  p4 = p2 * jnp.where(r4 == 2, pltpu.roll(p2, 1, 0), jnp.where(r4 == 3, pltpu.roll(p2, 2, 0), 1.0))
  out[4] = p4
  p, h = p4, 4
  while h < L:
    p = p * jnp.where((row & h) != 0, _bcast_row_in_blocks(p, 2 * h, h - 1), 1.0)
    h *= 2
    out[h] = p
  return out


def _inc_suffix_products(f, row):
  L = f.shape[0]
  up = lambda x, s: pltpu.roll(x, L - s, 0)
  out = {1: f}
  p2 = f * jnp.where((row & 1) == 0, up(f, 1), 1.0)
  out[2] = p2
  r4 = row & 3
  p4 = p2 * jnp.where(r4 == 1, up(p2, 1), jnp.where(r4 == 0, up(p2, 2), 1.0))
  out[4] = p4
  p, h = p4, 4
  while h < L:
    p = p * jnp.where((row & h) == 0, _bcast_row_in_blocks(p, 2 * h, h), 1.0)
    h *= 2
    out[h] = p
  return out


def _hg_chunk(q, f, v, st_ref, masks_ref, hmask, row, reverse):
  L = q.shape[0]
  k = 1.0 - f
  pre = _inc_prefix_products(f, row)
  suf = _inc_suffix_products(f, row)
  if not reverse:
    q_dec = pre
    k_dec = {h: (jnp.where((row & (h - 1)) == h - 1, 1.0, pltpu.roll(suf[h], L - 1, 0)) if h > 1 else None)
             for h in suf}
  else:
    q_dec = suf
    k_dec = {h: (jnp.where((row & (h - 1)) == 0, 1.0, pltpu.roll(pre[h], 1, 0)) if h > 1 else None)
             for h in pre}

  def head_blockdiag(x):
    return jnp.concatenate([x] * HG_HEADS, axis=0) * hmask

  nt = (((1,), (1,)), ((), ()))
  kb = k.astype(BF16)
  p = masks_ref[0] * lax.dot_general(q.astype(BF16), head_blockdiag(kb), nt, preferred_element_type=F32)
  for li, h in enumerate(_HG_LEVELS):
    qh = (q * q_dec[h]).astype(BF16)
    kh = kb if h == 1 else (k * k_dec[h]).astype(BF16)
    p = p + masks_ref[li + 1] * lax.dot_general(qh, head_blockdiag(kh), nt, preferred_element_type=F32)

  vb = v.astype(BF16)
  o = jnp.dot(p.astype(BF16), head_blockdiag(vb), preferred_element_type=F32)
  st = st_ref[...]
  o = o + lax.dot_general((q * q_dec[L]).astype(BF16), st.astype(BF16), nt, preferred_element_type=F32)

  ke = (k * k_dec[L]).astype(BF16)
  end = q_dec[L][0:1, :] if reverse else q_dec[L][L - 1:L, :]
  tn = (((0,), (0,)), ((), ()))
  upd = lax.dot_general(vb, ke, tn, preferred_element_type=F32)
  st_ref[...] = st * end + upd * hmask.astype(F32)
  return o


def _hg_chunk_fast(q, f, v, st_ref, masks_ref, hmask, row, reverse):
  L = q.shape[0]
  a = (_inc_suffix_products(f, row) if reverse else _inc_prefix_products(f, row))[L]
  qt = (q * a).astype(BF16)
  kt = ((1.0 - f) * (1.0 / a)).astype(BF16)

  def head_blockdiag(x):
    return jnp.concatenate([x] * HG_HEADS, axis=0) * hmask

  nt = (((1,), (1,)), ((), ()))
  tn = (((0,), (0,)), ((), ()))
  p = jnp.where(masks_ref[len(_HG_LEVELS) + 1] != 0.0,
                lax.dot_general(qt, head_blockdiag(kt), nt, preferred_element_type=F32), 0.0)
  vb = v.astype(BF16)
  st = st_ref[...]
  o = jnp.dot(p.astype(BF16), head_blockdiag(vb), preferred_element_type=F32)
  o = o + lax.dot_general(qt, st.astype(BF16), nt, preferred_element_type=F32)
  end = a[0:1, :] if reverse else a[L - 1:L, :]
  upd = lax.dot_general(vb, kt, tn, preferred_element_type=F32)
  st_ref[...] = (st + upd * hmask.astype(F32)) * end
  return o


def _hgrn2_kernel(pb_ref, lb_ref, ng_ref, mf_ref, mb_ref, hm_ref, ones_ref, o_ref,
                  q_ref, ff_ref, fb_ref, of_ref, ob_ref, sf_ref, sb_ref):
  S = pb_ref.shape[0]
  W = HG_WIDTH
  L = HG_CHUNK
  n = S // L
  row = lax.broadcasted_iota(jnp.int32, (L, W), 0)
  hmask = hm_ref[...]
  sf_ref[...] = jnp.zeros_like(sf_ref)
  sb_ref[...] = jnp.zeros_like(sb_ref)

  def full_reduce(x, op):
    return op(op(x, axis=1, keepdims=True), axis=0, keepdims=True)

  q = _silu(pb_ref[:, 0:W]) * (HG_DK ** -0.5)
  q_ref[...] = q
  margin = (jnp.log(1.0 + full_reduce(jnp.abs(q), jnp.max))
            + jnp.log(1.0 + full_reduce(jnp.abs(pb_ref[:, 3 * W:4 * W]), jnp.max)))
  log_min = None
  for d, f_ref in enumerate((ff_ref, fb_ref)):
    lb = lb_ref[d:d + 1, :]
    f = lb + (1.0 - lb) * _sigmoid(pb_ref[:, (d + 1) * W:(d + 2) * W])
    f_ref[...] = f
    chunk_log = jnp.sum(jnp.log(f).reshape(n, L, W), axis=1)
    m = full_reduce(chunk_log, jnp.min)
    log_min = m if log_min is None else jnp.minimum(log_min, m)
  in_range = (log_min - margin)[0, 0] >= HG_SAFE_LOG

  def make_body(chunk_fn):
    def body(c, _):
      for f_ref, out_ref, st_ref, m_ref, rev, cc in ((ff_ref, of_ref, sf_ref, mf_ref, False, c),
                                                     (fb_ref, ob_ref, sb_ref, mb_ref, True, n - 1 - c)):
        sl = pl.ds(pl.multiple_of(cc * L, L), L)
        out_ref[sl, :] = chunk_fn(q_ref[sl, :], f_ref[sl, :], pb_ref[sl, 3 * W:4 * W], st_ref, m_ref,
                                  hmask, row, rev)
      return 0
    return body

  @pl.when(in_range)
  def _():
    lax.fori_loop(0, n, make_body(_hg_chunk_fast), 0, unroll=4)

  @pl.when(jnp.logical_not(in_range))
  def _():
    lax.fori_loop(0, n, make_body(_hg_chunk), 0)

  o = of_ref[...] + ob_ref[...]
  ms = jnp.dot((o * o).astype(BF16), ones_ref[...], preferred_element_type=F32) * (1.0 / HG_DK)
  y = o * lax.rsqrt(ms + EPS) * ng_ref[...]
  o_ref[...] = (y * _silu(pb_ref[:, 4 * W:5 * W])).astype(o_ref.dtype)


def _hg_constants():
  L = HG_CHUNK
  t = jnp.arange(L)[:, None]
  s = jnp.arange(L)[None, :]
  fwd, bwd = [t == s], [t == s]
  for h in _HG_LEVELS:
    same = (t // (2 * h)) == (s // (2 * h))
    fwd.append(same & ((t & h) != 0) & ((s & h) == 0))
    bwd.append(same & ((t & h) == 0) & ((s & h) != 0))
  fwd.append(t >= s)
  bwd.append(t <= s)
  tile = lambda m: jnp.tile(jnp.stack(m).astype(F32), (1, 1, HG_HEADS))
  hid = jnp.arange(HG_WIDTH) // HG_DK
  hmask = (hid[:, None] == hid[None, :])
  return tile(fwd), tile(bwd), hmask.astype(BF16), hmask.astype(BF16)


def _hgrn2(pb, lb, norm_g, B, S):
  T = B * S
  W = HG_WIDTH
  mf, mb, hmask, ones = _hg_constants()
  const = lambda shape: pl.BlockSpec(shape, lambda b: (0,) * len(shape))
  return pl.pallas_call(
      _hgrn2_kernel,
      grid=(B,),
      in_specs=[
          pl.BlockSpec((S, PB_W), lambda b: (b, 0)),
          const((2, W)), const((1, W)), const(mf.shape), const(mb.shape), const((W, W)), const((W, W)),
      ],
      out_specs=pl.BlockSpec((S, W), lambda b: (b, 0)),
      out_shape=jax.ShapeDtypeStruct((T, W), BF16),
      scratch_shapes=[pltpu.VMEM((S, W), F32) for _ in range(5)]
                     + [pltpu.VMEM((W, W), F32), pltpu.VMEM((W, W), F32)],
      compiler_params=_cparams(("parallel",)),
      name="hgrn2",
  )(pb, lb, norm_g, mf, mb, hmask, ones)


ROPE_HALF = MLA_ROPE // 2
ATTN_MAX_BOUND = 60.0
ATTN_KEY_BLOCK = 256
ATTN_Q_BLOCK = 2048
MLA_PREP_ROWS = 1024


def _rope_tables_kernel(pos_ref, invf_ref, sgn_ref, cos_ref, sin_ref):
  ang = pos_ref[...].astype(F32) * invf_ref[...]
  cos_ref[...] = jnp.cos(ang)
  sin_ref[...] = sgn_ref[...] * jnp.sin(ang)


def _rope_tables(pos2, tm):
  T = pos2.shape[0]
  inv_freq = (ROPE_THETA ** (-jnp.arange(0, ROPE_HALF, dtype=F32) * (2.0 / MLA_ROPE))).reshape(1, ROPE_HALF)
  ones = jnp.ones((1, ROPE_HALF), F32)
  no_nope = jnp.zeros((1, MLA_NOPE), F32)
  invf = _slot_lanes(no_nope, jnp.concatenate([inv_freq, inv_freq], axis=1))
  sgn = _slot_lanes(no_nope, jnp.concatenate([-ones, ones], axis=1))
  row = pl.BlockSpec((1, HEAD_SLOT), lambda i: (0, 0))
  return pl.pallas_call(
      _rope_tables_kernel,
      grid=(T // tm,),
      in_specs=[pl.BlockSpec((tm, 1), lambda i: (i, 0)), row, row],
      out_specs=[pl.BlockSpec((tm, HEAD_SLOT), lambda i: (i, 0))] * 2,
      out_shape=[jax.ShapeDtypeStruct((T, HEAD_SLOT), F32)] * 2,
      compiler_params=_cparams(("parallel",)),
      name="rope_tables",
  )(pos2, invf, sgn)


def _rope(t, cos_t, sin_t):
  return t * cos_t + pltpu.roll(t, HEAD_SLOT // 2, 1) * sin_t


def _mla_prep_kernel(pc_ref, cos_ref, sin_ref, qn_ref, kvn_ref, wq_ref, wk_ref, wv_ref, vone_ref, gq_ref, gk_ref,
                     q_ref, k_ref, v_ref):
  cos_t = cos_ref[...]
  sin_t = sin_ref[...]
  cq = _rms(pc_ref[:, 0:MLA_Q_RANK], qn_ref[...]).astype(BF16)
  ckv = _rms(pc_ref[:, MLA_Q_RANK:MLA_Q_RANK + MLA_KV_RANK], kvn_ref[...]).astype(BF16)
  kr_slot = pc_ref[:, MLA_Q_RANK + MLA_KV_RANK:]
  q_all = jnp.dot(cq, wq_ref[...], preferred_element_type=F32)
  k_all = jnp.dot(ckv, wk_ref[...], preferred_element_type=F32)
  v_ref[...] = (jnp.dot(ckv, wv_ref[...], preferred_element_type=F32) + vone_ref[...]).astype(v_ref.dtype)
  for h in range(MLA_HEADS):
    sl = slice(h * HEAD_SLOT, (h + 1) * HEAD_SLOT)
    qh = _rope(_rms(q_all[:, sl], gq_ref[...], MLA_QK), cos_t, sin_t)
    q_ref[h] = qh.astype(q_ref.dtype)
    kh = _rope(_rms(k_all[:, sl] + kr_slot, gk_ref[...], MLA_QK), cos_t, sin_t)
    k_ref[h] = kh.astype(k_ref.dtype)


def _mla_prep(pc, cos_t, sin_t, q_norm, kv_norm, wq, wk, wv, vone, gq, gk, B, S, ts):
  H = MLA_HEADS
  nb = S // ts
  const = lambda shape: pl.BlockSpec(shape, lambda b, s: (0,) * len(shape))
  tok = lambda w: pl.BlockSpec((ts, w), lambda b, s: (b * nb + s, 0))
  return pl.pallas_call(
      _mla_prep_kernel,
      grid=(B, nb),
      in_specs=[
          tok(PC_W), tok(HEAD_SLOT), tok(HEAD_SLOT),
          const((1, MLA_Q_RANK)), const((1, MLA_KV_RANK)),
          const(wq.shape), const(wk.shape), const(wv.shape), const(vone.shape),
          const((1, HEAD_SLOT)), const((1, HEAD_SLOT)),
      ],
      out_specs=[
          pl.BlockSpec((None, H, ts, HEAD_SLOT), lambda b, s: (b, 0, s, 0)),
          pl.BlockSpec((None, H, ts, HEAD_SLOT), lambda b, s: (b, 0, s, 0)),
          pl.BlockSpec((ts, H * HEAD_SLOT), lambda b, s: (b * nb + s, 0)),
      ],
      out_shape=[
          jax.ShapeDtypeStruct((B, H, S, HEAD_SLOT), BF16),
          jax.ShapeDtypeStruct((B, H, S, HEAD_SLOT), BF16),
          jax.ShapeDtypeStruct((B * S, H * HEAD_SLOT), BF16),
      ],
      compiler_params=_cparams(("parallel", "parallel")),
      name="mla_prep",
  )(pc, cos_t, sin_t, q_norm, kv_norm, wq, wk, wv, vone, gq, gk)


def _attention_kernel(bound_ref, q_ref, k_ref, v_ref, o_ref):
  tq = q_ref.shape[1]
  S = k_ref.shape[1]
  bound = bound_ref[0]
  nt = (((1,), (1,)), ((), ()))

  def finish(res):
    lane = lax.broadcasted_iota(jnp.int32, (tq, HEAD_SLOT), 1)
    even = res[0] * (1.0 / res[0][:, MLA_V:MLA_V + 1])
    odd = res[1] * (1.0 / res[1][:, 0:1])
    o_ref[...] = jnp.where(lane < MLA_V, even, odd).astype(o_ref.dtype)

  @pl.when(bound <= ATTN_MAX_BOUND)
  def _():
    res = []
    for h in range(2):
      acc = None
      for j in range(S // ATTN_KEY_BLOCK):
        ks = slice(j * ATTN_KEY_BLOCK, (j + 1) * ATTN_KEY_BLOCK)
        s = lax.dot_general(q_ref[h], k_ref[h, ks, :], nt, preferred_element_type=F32)
        p = jnp.exp2(s - bound).astype(BF16)
        c = jnp.dot(p, v_ref[ks, h * HEAD_SLOT:(h + 1) * HEAD_SLOT], preferred_element_type=F32)
        acc = c if acc is None else acc + c
      res.append(acc)
    finish(res)

  @pl.when(bound > ATTN_MAX_BOUND)
  def _():
    res = []
    for h in range(2):
      s = lax.dot_general(q_ref[h], k_ref[h], nt, preferred_element_type=F32)
      p = jnp.exp2(s - jnp.max(s, axis=-1, keepdims=True)).astype(BF16)
      res.append(jnp.dot(p, v_ref[:, h * HEAD_SLOT:(h + 1) * HEAD_SLOT], preferred_element_type=F32))
    finish(res)


def _attention(bound, q, k, v, B, S, tq):
  H = MLA_HEADS
  nq = S // tq
  return pl.pallas_call(
      _attention_kernel,
      grid=(B, H // 2, nq),
      in_specs=[
          pl.BlockSpec(memory_space=pltpu.SMEM),
          pl.BlockSpec((None, 2, tq, HEAD_SLOT), lambda b, hp, i: (b, hp, i, 0)),
          pl.BlockSpec((None, 2, S, HEAD_SLOT), lambda b, hp, i: (b, hp, 0, 0)),
          pl.BlockSpec((S, 2 * HEAD_SLOT), lambda b, hp, i: (b, hp)),
      ],
      out_specs=pl.BlockSpec((tq, 2 * MLA_V), lambda b, hp, i: (b * nq + i, hp)),
      out_shape=jax.ShapeDtypeStruct((B * S, MLA_WIDTH), F32),
      compiler_params=_cparams(("parallel", "parallel", "parallel")),
      name="mla_attention",
  )(bound, q, k, v)


def _mixed_residual(x_ref, ya_ref, yb_ref, yc_ref, gc_ref, w_ref):
  yc = _rms(yc_ref[...], gc_ref[...]).astype(BF16)
  a0, a1 = LRU_WIDTH, LRU_WIDTH + HG_WIDTH
  acc = jnp.dot(ya_ref[...], w_ref[0:a0, :], preferred_element_type=F32)
  acc = acc + jnp.dot(yb_ref[...], w_ref[a0:a1, :], preferred_element_type=F32)
  acc = acc + jnp.dot(yc, w_ref[a1:, :], preferred_element_type=F32)
  return x_ref[...] + acc


def _outproj_kernel(x_ref, ya_ref, yb_ref, yc_ref, gc_ref, w_ref, o_ref):
  o_ref[...] = _mixed_residual(x_ref, ya_ref, yb_ref, yc_ref, gc_ref, w_ref)


def _outproj(x2, ya, yb, yc, gc, w, tm):
  T = x2.shape[0]
  tok = lambda wd: pl.BlockSpec((tm, wd), lambda i: (i, 0))
  return pl.pallas_call(
      _outproj_kernel,
      grid=(T // tm,),
      in_specs=[tok(D_MODEL), tok(LRU_WIDTH), tok(HG_WIDTH), tok(MLA_WIDTH),
                pl.BlockSpec((1, MLA_WIDTH), lambda i: (0, 0)),
                pl.BlockSpec((D_MODEL, D_MODEL), lambda i: (0, 0))],
      out_specs=tok(D_MODEL),
      out_shape=jax.ShapeDtypeStruct((T, D_MODEL), F32),
      compiler_params=_cparams(("parallel",)),
      name="outproj",
  )(x2, ya, yb, yc, gc, w)


def _ffn_kernel(x_ref, ya_ref, yb_ref, yc_ref, gc_ref, wo_ref, g_ref, wg_ref, wu_ref, wd_ref, o_ref):
  x = _mixed_residual(x_ref, ya_ref, yb_ref, yc_ref, gc_ref, wo_ref)
  h = _rms(x, g_ref[...]).astype(BF16)
  F = wg_ref.shape[1]
  acc = x
  for c0 in range(0, F, FFN_CHUNK):
    c1 = min(c0 + FFN_CHUNK, F)
    g = jnp.dot(h, wg_ref[:, c0:c1], preferred_element_type=F32)
    u = jnp.dot(h, wu_ref[:, c0:c1], preferred_element_type=F32)
    a = (_silu(g) * u).astype(BF16)
    acc = acc + jnp.dot(a, wd_ref[c0:c1, :], preferred_element_type=F32)
  o_ref[...] = acc


def _outproj_ffn(x2, ya, yb, yc, gc, w_out, gain, wg, wu, wd, tm):
  T = x2.shape[0]
  F = wg.shape[1]
  tok = lambda w: pl.BlockSpec((tm, w), lambda i: (i, 0))
  resident = lambda shape: pl.BlockSpec(shape, lambda i: (0, 0), pipeline_mode=pl.Buffered(1))
  return pl.pallas_call(
      _ffn_kernel,
      grid=(T // tm,),
      in_specs=[
          tok(D_MODEL), tok(LRU_WIDTH), tok(HG_WIDTH), tok(MLA_WIDTH),
          pl.BlockSpec((1, MLA_WIDTH), lambda i: (0, 0)), resident((D_MODEL, D_MODEL)),
          pl.BlockSpec((1, D_MODEL), lambda i: (0, 0)),
          resident((D_MODEL, F)), resident((D_MODEL, F)), resident((F, D_MODEL)),
      ],
      out_specs=tok(D_MODEL),
      out_shape=jax.ShapeDtypeStruct((T, D_MODEL), F32),
      compiler_params=_cparams(("parallel",)),
      name="outproj_ffn",
  )(x2, ya, yb, yc, gc, w_out, gain, wg, wu, wd)


def _router_kernel(x_ref, g_ref, wr_ref, sel_ref):
  def split(v):
    hi = v.astype(BF16)
    return hi, (v - hi.astype(F32)).astype(BF16)

  h_hi, h_lo = split(_rms(x_ref[...], g_ref[...]))
  w_hi, w_lo = split(wr_ref[...])
  logits = (jnp.dot(h_hi, w_hi, preferred_element_type=F32)
            + jnp.dot(h_lo, w_hi, preferred_element_type=F32)
            + jnp.dot(h_hi, w_lo, preferred_element_type=F32))
  lane = lax.broadcasted_iota(jnp.int32, logits.shape, 1)
  neg = jnp.float32(-jnp.inf)
  lg = jnp.where(lane < N_EXPERTS, logits, neg)
  m1 = jnp.max(lg, axis=-1, keepdims=True)
  i1 = jnp.min(jnp.where(lg == m1, lane, LANE), axis=-1, keepdims=True)
  lg2 = jnp.where(lane == i1, neg, lg)
  m2 = jnp.max(lg2, axis=-1, keepdims=True)
  i2 = jnp.min(jnp.where(lg2 == m2, lane, LANE), axis=-1, keepdims=True)
  e2 = jnp.exp(m2 - m1)
  w1 = 1.0 / (1.0 + e2)
  w2 = e2 / (1.0 + e2)
  sel_ref[...] = jnp.where(lane == 0, i1.astype(F32), jnp.where(lane == 1, i2.astype(F32),
                           jnp.where(lane == 2, w1, jnp.where(lane == 3, w2, 0.0))))


def _router(x2, gain, wr, tm):
  T = x2.shape[0]
  return pl.pallas_call(
      _router_kernel,
      grid=(T // tm,),
      in_specs=[pl.BlockSpec((tm, D_MODEL), lambda i: (i, 0)),
                pl.BlockSpec((1, D_MODEL), lambda i: (0, 0)),
                pl.BlockSpec((D_MODEL, LANE), lambda i: (0, 0))],
      out_specs=pl.BlockSpec((tm, LANE), lambda i: (i, 0)),
      out_shape=jax.ShapeDtypeStruct((T, LANE), F32),
      compiler_params=_cparams(("parallel",)),
      name="moe_router",
  )(x2, gain, wr)


def _route_plan(sel, tm):
  T = sel.shape[0]
  e_flat = sel[:, 0:2].astype(jnp.int32).reshape(1, 2 * T)
  onehot = (e_flat == jnp.arange(N_EXPERTS, dtype=jnp.int32)[:, None]).astype(jnp.int32)
  csum = jnp.cumsum(onehot, axis=1)
  counts = csum[:, -1]
  tiles_e = (counts + tm - 1) // tm
  tile_end = jnp.cumsum(tiles_e)
  offs = (tile_end - tiles_e) * tm
  pos = jnp.sum(onehot * (csum - 1 + offs[:, None]), axis=0)
  n_tiles = (2 * T) // tm + N_EXPERTS
  tile_expert = jnp.sum(jnp.arange(n_tiles, dtype=jnp.int32)[:, None] >= tile_end[None, :], axis=1)
  tile_expert = jnp.minimum(tile_expert, N_EXPERTS - 1).astype(jnp.int32)
  last_tile = jnp.where(tiles_e > 0, offs + (tiles_e - 1) * tm, -1)
  spare = tile_end[-1] + jnp.arange(N_EXPERTS, dtype=jnp.int32)
  spare_tile = jnp.where(spare < n_tiles, spare * tm, -1)
  clear = jnp.concatenate([last_tile, spare_tile]).astype(jnp.int32)
  return pos.astype(jnp.int32), tile_expert, tile_end[-1:].astype(jnp.int32), clear, n_tiles


def _row_copy(src, dst, sem, rows=1, src_row=0, dst_row=0):
  return pltpu.make_async_copy(src.at[pl.ds(src_row, rows), :], dst.at[pl.ds(dst_row, rows), :], sem)


def _dispatch_kernel(pos_ref, lt_ref, x_ref, xs_hbm, zbuf, sem, zsem, *, tm):
  tt = x_ref.shape[0]
  zr = zbuf.shape[0]

  @pl.when(pl.program_id(0) == 0)
  def _():
    zbuf[...] = jnp.zeros_like(zbuf)
    for e in range(lt_ref.shape[0]):
      @pl.when(lt_ref[e] >= 0)
      def _():
        rows = [pl.multiple_of(lt_ref[e] + c * zr, SUBLANE) for c in range(tm // zr)]
        for r0 in rows:
          _row_copy(zbuf, xs_hbm, zsem, zr, 0, r0).start()
        for r0 in rows:
          _row_copy(zbuf, xs_hbm, zsem, zr, 0, r0).wait()

  def body(r, _):
    for c in range(2):
      _row_copy(x_ref, xs_hbm, sem, 1, r, pos_ref[0, 0, 2 * r + c]).start(priority=c)
    return 0

  lax.fori_loop(0, tt, body, 0, unroll=8)
  for c in range(2):
    _row_copy(x_ref, xs_hbm, sem, tt).wait()


def _dispatch(x2, pos, last_tile, n_rows, tt, tm):
  T = x2.shape[0]
  pos3 = pos.reshape(T // tt, 1, 2 * tt)
  zr = min(MOE_ZERO_ROWS, tm)
  return pl.pallas_call(
      functools.partial(_dispatch_kernel, tm=tm),
      grid=(T // tt,),
      in_specs=[pl.BlockSpec((1, 1, 2 * tt), lambda i: (i, 0, 0), memory_space=pltpu.SMEM),
                pl.BlockSpec(memory_space=pltpu.SMEM),
                pl.BlockSpec((tt, D_MODEL), lambda i: (i, 0))],
      out_specs=pl.BlockSpec(memory_space=pl.ANY),
      out_shape=jax.ShapeDtypeStruct((n_rows, D_MODEL), F32),
      scratch_shapes=[pltpu.VMEM((zr, D_MODEL), F32), pltpu.SemaphoreType.DMA(()), pltpu.SemaphoreType.DMA(())],
      compiler_params=_cparams(("arbitrary",)),
      name="moe_dispatch",
  )(pos3, last_tile, x2)


def _moe_group_kernel(te_ref, nu_ref, xs_ref, g_ref, w1_ref, w3_ref, w2_ref, ys_ref, h_ref, acc_ref):
  del te_ref
  i = pl.program_id(0)
  f = pl.program_id(1)
  last = pl.num_programs(1) - 1
  used = i < nu_ref[0]

  @pl.when(used & (f == 0))
  def _():
    h_ref[...] = _rms(xs_ref[...], g_ref[...]).astype(BF16)
    acc_ref[...] = jnp.zeros_like(acc_ref)

  @pl.when(used)
  def _():
    h = h_ref[...]
    g = jnp.dot(h, w1_ref[...].astype(BF16), preferred_element_type=F32)
    u = jnp.dot(h, w3_ref[...].astype(BF16), preferred_element_type=F32)
    a = (_silu(g) * u).astype(BF16)
    acc_ref[...] += jnp.dot(a, w2_ref[...].astype(BF16), preferred_element_type=F32)

  @pl.when(used & (f == last))
  def _():
    ys_ref[...] = acc_ref[...]

  @pl.when(jnp.logical_not(used) & (f == last))
  def _():
    ys_ref[...] = jnp.zeros_like(ys_ref)


def _moe_group(xs, gain, tile_expert, n_used, w1, w3, w2, tm, tf):
  P = xs.shape[0]
  F = w1.shape[2]
  nf = F // tf
  fidx = lambda i, f, te, nu: jnp.where(i < nu[0], f, nf - 1)
  grid_spec = pltpu.PrefetchScalarGridSpec(
      num_scalar_prefetch=2,
      grid=(P // tm, nf),
      in_specs=[
          pl.BlockSpec((tm, D_MODEL), lambda i, f, te, nu: (jnp.minimum(i, nu[0] - 1), 0)),
          pl.BlockSpec((1, D_MODEL), lambda i, f, te, nu: (0, 0)),
          pl.BlockSpec((None, D_MODEL, tf), lambda i, f, te, nu: (te[i], 0, fidx(i, f, te, nu))),
          pl.BlockSpec((None, D_MODEL, tf), lambda i, f, te, nu: (te[i], 0, fidx(i, f, te, nu))),
          pl.BlockSpec((None, tf, D_MODEL), lambda i, f, te, nu: (te[i], fidx(i, f, te, nu), 0)),
      ],
      out_specs=pl.BlockSpec((tm, D_MODEL), lambda i, f, te, nu: (i, 0)),
      scratch_shapes=[pltpu.VMEM((tm, D_MODEL), BF16), pltpu.VMEM((tm, D_MODEL), F32)],
  )
  return pl.pallas_call(
      _moe_group_kernel,
      grid_spec=grid_spec,
      out_shape=jax.ShapeDtypeStruct((P, D_MODEL), F32),
      compiler_params=_cparams(("arbitrary", "arbitrary")),
      name="moe_group",
  )(tile_expert, n_used, xs, gain, w1, w3, w2)


def _combine_kernel(pos_ref, x_ref, sel_ref, ys_hbm, o_ref, ybuf, sem):
  tt = x_ref.shape[0]

  def body(r, _):
    for c in range(2):
      _row_copy(ys_hbm, ybuf.at[c], sem, 1, pos_ref[0, 0, 2 * r + c], r).start(priority=c)
    return 0

  lax.fori_loop(0, tt, body, 0, unroll=8)
  for c in range(2):
    _row_copy(ys_hbm, ybuf.at[c], sem, tt).wait()
  sel = sel_ref[...]
  o_ref[...] = x_ref[...] + sel[:, 2:3] * ybuf[0] + sel[:, 3:4] * ybuf[1]


def _combine(x2, sel, pos, ys, tt):
  T = x2.shape[0]
  pos3 = pos.reshape(T // tt, 1, 2 * tt)
  return pl.pallas_call(
      _combine_kernel,
      grid=(T // tt,),
      in_specs=[pl.BlockSpec((1, 1, 2 * tt), lambda i: (i, 0, 0), memory_space=pltpu.SMEM),
                pl.BlockSpec((tt, D_MODEL), lambda i: (i, 0)),
                pl.BlockSpec((tt, LANE), lambda i: (i, 0)),
                pl.BlockSpec(memory_space=pl.ANY)],
      out_specs=pl.BlockSpec((tt, D_MODEL), lambda i: (i, 0)),
      out_shape=jax.ShapeDtypeStruct((T, D_MODEL), F32),
      scratch_shapes=[pltpu.VMEM((2, tt, D_MODEL), F32), pltpu.SemaphoreType.DMA(())],
      compiler_params=_cparams(("arbitrary",)),
      name="moe_combine",
  )(pos3, x2, sel, ys)


def _block_diag(w):
  n, d, e = w.shape
  eye = jnp.eye(n, dtype=w.dtype)
  return (eye[:, None, :, None] * w[:, :, None, :]).reshape(n * d, n * e)


def _slot_lanes(nope, rope):
  z = lambda n: jnp.zeros(nope.shape[:-1] + (n,), nope.dtype)
  cut = HEAD_SLOT // 2 - ROPE_HALF
  x1, x2 = (z(ROPE_HALF), z(ROPE_HALF)) if rope is None else (rope[..., :ROPE_HALF], rope[..., ROPE_HALF:])
  return jnp.concatenate([x1, nope[..., :cut], x2, nope[..., cut:], z(HEAD_SLOT - MLA_QK)], axis=-1)


def _head_slots(w, width):
  r = w.shape[0]
  w3 = w.reshape(r, MLA_HEADS, width)
  rope = w3[..., MLA_NOPE:] if width == MLA_QK else None
  return _slot_lanes(w3[..., :MLA_NOPE], rope).reshape(r, MLA_HEADS * HEAD_SLOT)


def _slot_gain(g):
  g = g.astype(F32).reshape(1, MLA_QK)
  return _slot_lanes(g[:, :MLA_NOPE], g[:, MLA_NOPE:])


def _value_slots(wv):
  r = wv.shape[0]
  w4 = wv.reshape(r, MLA_HEADS // 2, 2, MLA_V)
  z = jnp.zeros((r, MLA_HEADS // 2, HEAD_SLOT - MLA_V), wv.dtype)
  out = jnp.stack([jnp.concatenate([w4[:, :, 0], z], axis=-1), jnp.concatenate([z, w4[:, :, 1]], axis=-1)], axis=2)
  one = np.zeros((1, MLA_HEADS // 2, 2, HEAD_SLOT), np.float32)
  one[:, :, 0, MLA_V] = 1.0
  one[:, :, 1, 0] = 1.0
  return out.reshape(r, MLA_HEADS * HEAD_SLOT), jnp.asarray(one.reshape(1, MLA_HEADS * HEAD_SLOT))


def _score_bound(gq, gk):
  b = MLA_QK ** 0.5 * math.log2(math.e) * 1.02 * jnp.max(jnp.abs(gq)) * jnp.max(jnp.abs(gk))
  return b.astype(F32).reshape(1)


def kernel(x, positions, norm_mix, w_in, conv_w, conv_b, lru_wa, lru_ba, lru_wx, lru_bx, lru_lam,
           out_g_a, hg_lb_logits, hg_norm_g, mla_q_norm, mla_w_uq, mla_kv_norm, mla_w_ukv,
           qk_norm_q, qk_norm_k, out_g_c, w_out, norm_ffn, ffn_w_gate_up, ffn_w_down,
           moe_router, moe_w1, moe_w3, moe_w2):
  B, S, D = x.shape
  T = B * S
  depth = w_in.shape[0]
  tm = 512
  row = lambda v: v.astype(F32).reshape(1, -1)

  p = jax.nn.softmax(hg_lb_logits.astype(F32), axis=0)
  lower_bounds = jnp.cumsum(p, axis=0) - p[0:1]

  x2 = x.reshape(T, D)
  cos_t, sin_t = _rope_tables(positions.reshape(T, 1), tm)

  for l in range(depth):
    n_main = w_in.shape[2] - MLA_ROPE
    kr_slot_w = _slot_lanes(jnp.zeros((D, MLA_NOPE), F32), w_in[l][:, n_main:])
    w_in_l = jnp.concatenate([w_in[l][:, :n_main], kr_slot_w], axis=1).astype(BF16)
    wg_lru = jnp.concatenate(
        [_block_diag(lru_wa[l, 0]), _block_diag(lru_wx[l, 0]),
         _block_diag(lru_wa[l, 1]), _block_diag(lru_wx[l, 1])], axis=1).astype(BF16)
    bg_lru = jnp.concatenate([lru_ba[l, 0], lru_bx[l, 0], lru_ba[l, 1], lru_bx[l, 1]]).reshape(1, -1)
    wq = _head_slots(mla_w_uq[l], MLA_QK).astype(BF16)
    wkv = mla_w_ukv[l].reshape(MLA_KV_RANK, MLA_HEADS, MLA_NOPE + MLA_V)
    wk = _head_slots(wkv[:, :, :MLA_NOPE].reshape(MLA_KV_RANK, -1), MLA_NOPE).astype(BF16)
    wv, vone = _value_slots(wkv[:, :, MLA_NOPE:].reshape(MLA_KV_RANK, MLA_WIDTH))
    wv = wv.astype(BF16)

    pa, pb, pc = _norm_inproj(x2, row(norm_mix[l]), w_in_l, tm)
    ya = _rglru(pa, conv_w[l], row(conv_b[l]), wg_lru, bg_lru, lru_lam[l], row(out_g_a[l]), B, S)
    yb = _hgrn2(pb, lower_bounds[l], row(jnp.tile(hg_norm_g[l], HG_HEADS)), B, S)
    q, k, v = _mla_prep(pc, cos_t, sin_t, row(mla_q_norm[l]), row(mla_kv_norm[l]), wq, wk, wv, vone,
                        _slot_gain(qk_norm_q[l]) * (MLA_QK ** -0.5 * math.log2(math.e)),
                        _slot_gain(qk_norm_k[l]), B, S, min(MLA_PREP_ROWS, S))
    yc = _attention(_score_bound(qk_norm_q[l], qk_norm_k[l]), q, k, v, B, S, min(ATTN_Q_BLOCK, S))
    if l % 2 == 0:
      wgu = ffn_w_gate_up[l // 2]
      pad = D_FF_PAD - D_FF
      wg = jnp.pad(wgu[:, :D_FF], ((0, 0), (0, pad))).astype(BF16)
      wu = jnp.pad(wgu[:, D_FF:], ((0, 0), (0, pad))).astype(BF16)
      wd = jnp.pad(ffn_w_down[l // 2], ((0, pad), (0, 0))).astype(BF16)
      x2 = _outproj_ffn(x2, ya, yb, yc, row(out_g_c[l]), w_out[l].astype(BF16), row(norm_ffn[l]), wg, wu, wd, tm)
    else:
      x2 = _outproj(x2, ya, yb, yc, row(out_g_c[l]), w_out[l].astype(BF16), tm)
      wr = jnp.pad(moe_router[l // 2].astype(F32), ((0, 0), (0, LANE - N_EXPERTS)))
      gain = row(norm_ffn[l])
      tme = 1024 if T >= 8192 else 128
      sel = _router(x2, gain, wr, tm)
      pos, tile_expert, n_used, last_tile, n_tiles = _route_plan(sel, tme)
      xs = _dispatch(x2, pos, last_tile, n_tiles * tme, tm, tme)
      ys = _moe_group(xs, gain, tile_expert, n_used, moe_w1[l // 2], moe_w3[l // 2], moe_w2[l // 2],
                      tme, D_EXPERT // 7)
      x2 = _combine(x2, sel, pos, ys, tm)
  return x2.reshape(B, S, D)
```

```python
import functools
import math

import jax
import jax.numpy as jnp
import numpy as np
from jax import lax
from jax.experimental import pallas as pl
from jax.experimental.pallas import tpu as pltpu

F32 = jnp.float32
BF16 = jnp.bfloat16

D_MODEL = 1024
LRU_WIDTH = 256
LRU_BLOCKS = 4
CONV_WIDTH = 4
LRU_C = 8.0
HG_HEADS = 4
HG_DK = 64
HG_WIDTH = 256
MLA_HEADS = 8
MLA_NOPE = 64
MLA_ROPE = 32
MLA_V = 64
MLA_QK = MLA_NOPE + MLA_ROPE
MLA_Q_RANK = 256
MLA_KV_RANK = 128
MLA_WIDTH = MLA_HEADS * MLA_V
ROPE_THETA = 10000.0
D_FF = 2752
N_EXPERTS = 8
D_EXPERT = 3584
EPS = 1e-6

LANE = 128
SUBLANE = 8
HEAD_SLOT = 128
HG_CHUNK = 64
D_FF_PAD = 2816
MOE_ZERO_ROWS = 256
FFN_CHUNK = 1024
PROJ_ROWS = 1024
VMEM_LIMIT = 56 * 1024 * 1024

PA_W, PB_W, PC_W = 512, 1280, 512


def _cparams(sem):
  return pltpu.CompilerParams(dimension_semantics=sem, vmem_limit_bytes=VMEM_LIMIT)


def _sigmoid(x):
  return 0.5 + 0.5 * jnp.tanh(0.5 * x)


def _silu(x):
  return x * _sigmoid(x)


def _rms(x, gain, n=None):
  n = x.shape[-1] if n is None else n
  ms = jnp.sum(x * x, axis=-1, keepdims=True) * (1.0 / n)
  return x * lax.rsqrt(ms + EPS) * gain


def _norm_inproj_kernel(x_ref, g_ref, w_ref, oa_ref, ob_ref, oc_ref):
  h = _rms(x_ref[...], g_ref[...]).astype(BF16)
  p = jnp.dot(h, w_ref[...], preferred_element_type=F32)
  oa_ref[...] = p[:, :PA_W]
  ob_ref[...] = p[:, PA_W:PA_W + PB_W]
  oc_ref[...] = p[:, PA_W + PB_W:]


def _norm_inproj(x2, gain, w, tm):
  T = x2.shape[0]
  n = w.shape[1]
  return pl.pallas_call(
      _norm_inproj_kernel,
      grid=(T // tm,),
      in_specs=[
          pl.BlockSpec((tm, D_MODEL), lambda i: (i, 0)),
          pl.BlockSpec((1, D_MODEL), lambda i: (0, 0)),
          pl.BlockSpec((D_MODEL, n), lambda i: (0, 0)),
      ],
      out_specs=[
          pl.BlockSpec((tm, PA_W), lambda i: (i, 0)),
          pl.BlockSpec((tm, PB_W), lambda i: (i, 0)),
          pl.BlockSpec((tm, PC_W), lambda i: (i, 0)),
      ],
      out_shape=[
          jax.ShapeDtypeStruct((T, PA_W), F32),
          jax.ShapeDtypeStruct((T, PB_W), F32),
          jax.ShapeDtypeStruct((T, PC_W), F32),
      ],
      compiler_params=_cparams(("parallel",)),
      name="norm_inproj",
  )(x2, gain, w)


def _rglru_kernel(pa_ref, cw_ref, cb_ref, wg_ref, bg_ref, lam_ref, og_ref, o_ref,
                  af_ref, bf_ref, ab_ref, bb_ref, hf_ref, hb_ref):
  S = pa_ref.shape[0]
  W = LRU_WIDTH
  xa = pa_ref[:, 0:W]
  row = lax.broadcasted_iota(jnp.int32, (S, W), 0)

  xc = cb_ref[...] + cw_ref[2:3, :] * xa
  for k in (0, 1, 3):
    d = k - CONV_WIDTH // 2
    shifted = pltpu.roll(xa, (-d) % S, 0)
    valid = (row >= -d) if d < 0 else (row < S - d)
    xc = xc + cw_ref[k:k + 1, :] * jnp.where(valid, shifted, 0.0)

  z = jnp.dot(xc.astype(BF16), wg_ref[...], preferred_element_type=F32) + bg_ref[...]
  lam = lam_ref[...]
  sp = jnp.maximum(-lam, 0.0) + jnp.log1p(jnp.exp(-jnp.abs(lam)))
  for d, (a_ref, b_ref) in enumerate(((af_ref, bf_ref), (ab_ref, bb_ref))):
    r = _sigmoid(z[:, d * 2 * W:d * 2 * W + W])
    i = _sigmoid(z[:, d * 2 * W + W:(d + 1) * 2 * W])
    log_a = (-LRU_C) * r * sp[d:d + 1, :]
    a = jnp.exp(log_a)
    a_ref[...] = a
    y2 = 2.0 * log_a
    u = a * a
    near = jnp.where(u == 1.0, y2, (u - 1.0) * y2 / jnp.log(u))
    em1 = jnp.where(y2 > -0.5, near, u - 1.0)
    b_ref[...] = jnp.sqrt(-em1) * (i * xc)

  n = S // SUBLANE
  r8 = lax.broadcasted_iota(jnp.int32, (SUBLANE, W), 0)

  def body(c, carry):
    cf, cb = carry
    sl = pl.ds(pl.multiple_of(c * SUBLANE, SUBLANE), SUBLANE)
    a = af_ref[sl, :]
    b = bf_ref[sl, :]
    for s in (1, 2, 4):
      m = r8 >= s
      b = jnp.where(m, a * pltpu.roll(b, s, 0) + b, b)
      a = jnp.where(m, a * pltpu.roll(a, s, 0), a)
    h = b + a * cf
    hf_ref[sl, :] = h
    cf = h[SUBLANE - 1:SUBLANE, :]

    sl = pl.ds(pl.multiple_of((n - 1 - c) * SUBLANE, SUBLANE), SUBLANE)
    a = ab_ref[sl, :]
    b = bb_ref[sl, :]
    for s in (1, 2, 4):
      m = r8 < SUBLANE - s
      b = jnp.where(m, a * pltpu.roll(b, SUBLANE - s, 0) + b, b)
      a = jnp.where(m, a * pltpu.roll(a, SUBLANE - s, 0), a)
    h = b + a * cb
    hb_ref[sl, :] = h
    cb = h[0:1, :]
    return cf, cb

  zero = jnp.zeros((1, W), F32)
  lax.fori_loop(0, n, body, (zero, zero), unroll=8)

  g = pa_ref[:, W:2 * W]
  gelu = 0.5 * g * (1.0 + jnp.tanh(math.sqrt(2.0 / math.pi) * (g + 0.044715 * (g * g * g))))
  y = (hf_ref[...] + hb_ref[...]) * gelu
  o_ref[...] = _rms(y, og_ref[...]).astype(o_ref.dtype)


def _rglru(pa, conv_w, conv_b, wg, bg, lam, out_g, B, S):
  T = B * S
  W = LRU_WIDTH
  const = lambda shape: pl.BlockSpec(shape, lambda b: (0,) * len(shape))
  return pl.pallas_call(
      _rglru_kernel,
      grid=(B,),
      in_specs=[
          pl.BlockSpec((S, PA_W), lambda b: (b, 0)),
          const((CONV_WIDTH, W)), const((1, W)), const((W, 4 * W)), const((1, 4 * W)),
          const((2, W)), const((1, W)),
      ],
      out_specs=pl.BlockSpec((S, W), lambda b: (b, 0)),
      out_shape=jax.ShapeDtypeStruct((T, W), BF16),
      scratch_shapes=[pltpu.VMEM((S, W), F32) for _ in range(6)],
      compiler_params=_cparams(("parallel",)),
      name="rglru",
  )(pa, conv_w, conv_b, wg, bg, lam, out_g)


_HG_LEVELS = (1, 2, 4, 8, 16, 32)
HG_SAFE_LOG = -80.0


def _bcast_row_in_blocks(x, block, row):
  L, C = x.shape
  x3 = x.reshape(L // block, block, C)
  return jnp.broadcast_to(x3[:, row:row + 1, :], x3.shape).reshape(L, C)


def _inc_prefix_products(f, row):
  L = f.shape[0]
  out = {1: f}
  p2 = f * jnp.where((row & 1) == 1, pltpu.roll(f, 1, 0), 1.0)
  out[2] = p2
  r4 = row & 3
  p4 = p2 * jnp.where(r4 == 2, pltpu.roll(p2, 1, 0), jnp.where(r4 == 3, pltpu.roll(p2, 2, 0), 1.0))
  out[4] = p4
  p, h = p4, 4
  while h < L:
    p = p * jnp.where((row & h) != 0, _bcast_row_in_blocks(p, 2 * h, h - 1), 1.0)
    h *= 2
    out[h] = p
  return out


def _inc_suffix_products(f, row):
  L = f.shape[0]
  up = lambda x, s: pltpu.roll(x, L - s, 0)
  out = {1: f}
  p2 = f * jnp.where((row & 1) == 0, up(f, 1), 1.0)
  out[2] = p2
  r4 = row & 3
  p4 = p2 * jnp.where(r4 == 1, up(p2, 1), jnp.where(r4 == 0, up(p2, 2), 1.0))
  out[4] = p4
  p, h = p4, 4
  while h < L:
    p = p * jnp.where((row & h) == 0, _bcast_row_in_blocks(p, 2 * h, h), 1.0)
    h *= 2
    out[h] = p
  return out


def _hg_chunk(q, f, v, st_ref, masks_ref, hmask, row, reverse):
  L = q.shape[0]
  k = 1.0 - f
  pre = _inc_prefix_products(f, row)
  suf = _inc_suffix_products(f, row)
  if not reverse:
    q_dec = pre
    k_dec = {h: (jnp.where((row & (h - 1)) == h - 1, 1.0, pltpu.roll(suf[h], L - 1, 0)) if h > 1 else None)
             for h in suf}
  else:
    q_dec = suf
    k_dec = {h: (jnp.where((row & (h - 1)) == 0, 1.0, pltpu.roll(pre[h], 1, 0)) if h > 1 else None)
             for h in pre}

  def head_blockdiag(x):
    return jnp.concatenate([x] * HG_HEADS, axis=0) * hmask

  nt = (((1,), (1,)), ((), ()))
  kb = k.astype(BF16)
  p = masks_ref[0] * lax.dot_general(q.astype(BF16), head_blockdiag(kb), nt, preferred_element_type=F32)
  for li, h in enumerate(_HG_LEVELS):
    qh = (q * q_dec[h]).astype(BF16)
    kh = kb if h == 1 else (k * k_dec[h]).astype(BF16)
    p = p + masks_ref[li + 1] * lax.dot_general(qh, head_blockdiag(kh), nt, preferred_element_type=F32)

  vb = v.astype(BF16)
  o = jnp.dot(p.astype(BF16), head_blockdiag(vb), preferred_element_type=F32)
  st = st_ref[...]
  o = o + lax.dot_general((q * q_dec[L]).astype(BF16), st.astype(BF16), nt, preferred_element_type=F32)

  ke = (k * k_dec[L]).astype(BF16)
  end = q_dec[L][0:1, :] if reverse else q_dec[L][L - 1:L, :]
  tn = (((0,), (0,)), ((), ()))
  upd = lax.dot_general(vb, ke, tn, preferred_element_type=F32)
  st_ref[...] = st * end + upd * hmask.astype(F32)
  return o


def _hg_chunk_fast(q, f, v, st_ref, masks_ref, hmask, row, reverse):
  L = q.shape[0]
  a = (_inc_suffix_products(f, row) if reverse else _inc_prefix_products(f, row))[L]
  qt = (q * a).astype(BF16)
  kt = ((1.0 - f) * (1.0 / a)).astype(BF16)

  def head_blockdiag(x):
    return jnp.concatenate([x] * HG_HEADS, axis=0) * hmask

  nt = (((1,), (1,)), ((), ()))
  tn = (((0,), (0,)), ((), ()))
  p = jnp.where(masks_ref[len(_HG_LEVELS) + 1] != 0.0,
                lax.dot_general(qt, head_blockdiag(kt), nt, preferred_element_type=F32), 0.0)
  vb = v.astype(BF16)
  st = st_ref[...]
  o = jnp.dot(p.astype(BF16), head_blockdiag(vb), preferred_element_type=F32)
  o = o + lax.dot_general(qt, st.astype(BF16), nt, preferred_element_type=F32)
  end = a[0:1, :] if reverse else a[L - 1:L, :]
  upd = lax.dot_general(vb, kt, tn, preferred_element_type=F32)
  st_ref[...] = (st + upd * hmask.astype(F32)) * end
  return o


def _hgrn2_kernel(pb_ref, lb_ref, ng_ref, mf_ref, mb_ref, hm_ref, ones_ref, o_ref,
                  q_ref, ff_ref, fb_ref, of_ref, ob_ref, sf_ref, sb_ref):
  S = pb_ref.shape[0]
  W = HG_WIDTH
  L = HG_CHUNK
  n = S // L
  row = lax.broadcasted_iota(jnp.int32, (L, W), 0)
  hmask = hm_ref[...]
  sf_ref[...] = jnp.zeros_like(sf_ref)
  sb_ref[...] = jnp.zeros_like(sb_ref)

  def full_reduce(x, op):
    return op(op(x, axis=1, keepdims=True), axis=0, keepdims=True)

  q = _silu(pb_ref[:, 0:W]) * (HG_DK ** -0.5)
  q_ref[...] = q
  margin = (jnp.log(1.0 + full_reduce(jnp.abs(q), jnp.max))
            + jnp.log(1.0 + full_reduce(jnp.abs(pb_ref[:, 3 * W:4 * W]), jnp.max)))
  log_min = None
  for d, f_ref in enumerate((ff_ref, fb_ref)):
    lb = lb_ref[d:d + 1, :]
    f = lb + (1.0 - lb) * _sigmoid(pb_ref[:, (d + 1) * W:(d + 2) * W])
    f_ref[...] = f
    chunk_log = jnp.sum(jnp.log(f).reshape(n, L, W), axis=1)
    m = full_reduce(chunk_log, jnp.min)
    log_min = m if log_min is None else jnp.minimum(log_min, m)
  in_range = (log_min - margin)[0, 0] >= HG_SAFE_LOG

  def make_body(chunk_fn):
    def body(c, _):
      for f_ref, out_ref, st_ref, m_ref, rev, cc in ((ff_ref, of_ref, sf_ref, mf_ref, False, c),
                                                     (fb_ref, ob_ref, sb_ref, mb_ref, True, n - 1 - c)):
        sl = pl.ds(pl.multiple_of(cc * L, L), L)
        out_ref[sl, :] = chunk_fn(q_ref[sl, :], f_ref[sl, :], pb_ref[sl, 3 * W:4 * W], st_ref, m_ref,
                                  hmask, row, rev)
      return 0
    return body

  @pl.when(in_range)
  def _():
    lax.fori_loop(0, n, make_body(_hg_chunk_fast), 0, unroll=8)

  @pl.when(jnp.logical_not(in_range))
  def _():
    lax.fori_loop(0, n, make_body(_hg_chunk), 0)

  o = of_ref[...] + ob_ref[...]
  ms = jnp.dot((o * o).astype(BF16), ones_ref[...], preferred_element_type=F32) * (1.0 / HG_DK)
  y = o * lax.rsqrt(ms + EPS) * ng_ref[...]
  o_ref[...] = (y * _silu(pb_ref[:, 4 * W:5 * W])).astype(o_ref.dtype)


def _hg_constants():
  L = HG_CHUNK
  t = jnp.arange(L)[:, None]
  s = jnp.arange(L)[None, :]
  fwd, bwd = [t == s], [t == s]
  for h in _HG_LEVELS:
    same = (t // (2 * h)) == (s // (2 * h))
    fwd.append(same & ((t & h) != 0) & ((s & h) == 0))
    bwd.append(same & ((t & h) == 0) & ((s & h) != 0))
  fwd.append(t >= s)
  bwd.append(t <= s)
  tile = lambda m: jnp.tile(jnp.stack(m).astype(F32), (1, 1, HG_HEADS))
  hid = jnp.arange(HG_WIDTH) // HG_DK
  hmask = (hid[:, None] == hid[None, :])
  return tile(fwd), tile(bwd), hmask.astype(BF16), hmask.astype(BF16)


def _hgrn2(pb, lb, norm_g, B, S):
  T = B * S
  W = HG_WIDTH
  mf, mb, hmask, ones = _hg_constants()
  const = lambda shape: pl.BlockSpec(shape, lambda b: (0,) * len(shape))
  return pl.pallas_call(
      _hgrn2_kernel,
      grid=(B,),
      in_specs=[
          pl.BlockSpec((S, PB_W), lambda b: (b, 0)),
          const((2, W)), const((1, W)), const(mf.shape), const(mb.shape), const((W, W)), const((W, W)),
      ],
      out_specs=pl.BlockSpec((S, W), lambda b: (b, 0)),
      out_shape=jax.ShapeDtypeStruct((T, W), BF16),
      scratch_shapes=[pltpu.VMEM((S, W), F32) for _ in range(5)]
                     + [pltpu.VMEM((W, W), F32), pltpu.VMEM((W, W), F32)],
      compiler_params=_cparams(("parallel",)),
      name="hgrn2",
  )(pb, lb, norm_g, mf, mb, hmask, ones)


ROPE_HALF = MLA_ROPE // 2
ATTN_MAX_BOUND = 60.0
ATTN_KEY_BLOCK = 256
ATTN_Q_BLOCK = 2048
MLA_PREP_ROWS = 1024


def _rope_tables_kernel(pos_ref, invf_ref, sgn_ref, cos_ref, sin_ref):
  ang = pos_ref[...].astype(F32) * invf_ref[...]
  cos_ref[...] = jnp.cos(ang)
  sin_ref[...] = sgn_ref[...] * jnp.sin(ang)


def _rope_tables(pos2, tm):
  T = pos2.shape[0]
  inv_freq = (ROPE_THETA ** (-jnp.arange(0, ROPE_HALF, dtype=F32) * (2.0 / MLA_ROPE))).reshape(1, ROPE_HALF)
  ones = jnp.ones((1, ROPE_HALF), F32)
  no_nope = jnp.zeros((1, MLA_NOPE), F32)
  invf = _slot_lanes(no_nope, jnp.concatenate([inv_freq, inv_freq], axis=1))
  sgn = _slot_lanes(no_nope, jnp.concatenate([-ones, ones], axis=1))
  row = pl.BlockSpec((1, HEAD_SLOT), lambda i: (0, 0))
  return pl.pallas_call(
      _rope_tables_kernel,
      grid=(T // tm,),
      in_specs=[pl.BlockSpec((tm, 1), lambda i: (i, 0)), row, row],
      out_specs=[pl.BlockSpec((tm, HEAD_SLOT), lambda i: (i, 0))] * 2,
      out_shape=[jax.ShapeDtypeStruct((T, HEAD_SLOT), F32)] * 2,
      compiler_params=_cparams(("parallel",)),
      name="rope_tables",
  )(pos2, invf, sgn)


def _rope(t, cos_t, sin_t):
  return t * cos_t + pltpu.roll(t, HEAD_SLOT // 2, 1) * sin_t


def _mla_prep_kernel(pc_ref, cos_ref, sin_ref, qn_ref, kvn_ref, wq_ref, wk_ref, wv_ref, vone_ref, gq_ref, gk_ref,
                     q_ref, k_ref, v_ref):
  cos_t = cos_ref[...]
  sin_t = sin_ref[...]
  cq = _rms(pc_ref[:, 0:MLA_Q_RANK], qn_ref[...]).astype(BF16)
  ckv = _rms(pc_ref[:, MLA_Q_RANK:MLA_Q_RANK + MLA_KV_RANK], kvn_ref[...]).astype(BF16)
  kr_slot = pc_ref[:, MLA_Q_RANK + MLA_KV_RANK:]
  q_all = jnp.dot(cq, wq_ref[...], preferred_element_type=F32)
  k_all = jnp.dot(ckv, wk_ref[...], preferred_element_type=F32)
  v_ref[...] = (jnp.dot(ckv, wv_ref[...], preferred_element_type=F32) + vone_ref[...]).astype(v_ref.dtype)
  for h in range(MLA_HEADS):
    sl = slice(h * HEAD_SLOT, (h + 1) * HEAD_SLOT)
    qh = _rope(_rms(q_all[:, sl], gq_ref[...], MLA_QK), cos_t, sin_t)
    q_ref[h] = qh.astype(q_ref.dtype)
    kh = _rope(_rms(k_all[:, sl] + kr_slot, gk_ref[...], MLA_QK), cos_t, sin_t)
    k_ref[h] = kh.astype(k_ref.dtype)


def _mla_prep(pc, cos_t, sin_t, q_norm, kv_norm, wq, wk, wv, vone, gq, gk, B, S, ts):
  H = MLA_HEADS
  nb = S // ts
  const = lambda shape: pl.BlockSpec(shape, lambda b, s: (0,) * len(shape))
  tok = lambda w: pl.BlockSpec((ts, w), lambda b, s: (b * nb + s, 0))
  return pl.pallas_call(
      _mla_prep_kernel,
      grid=(B, nb),
      in_specs=[
          tok(PC_W), tok(HEAD_SLOT), tok(HEAD_SLOT),
          const((1, MLA_Q_RANK)), const((1, MLA_KV_RANK)),
          const(wq.shape), const(wk.shape), const(wv.shape), const(vone.shape),
          const((1, HEAD_SLOT)), const((1, HEAD_SLOT)),
      ],
      out_specs=[
          pl.BlockSpec((None, H, ts, HEAD_SLOT), lambda b, s: (b, 0, s, 0)),
          pl.BlockSpec((None, H, ts, HEAD_SLOT), lambda b, s: (b, 0, s, 0)),
          pl.BlockSpec((ts, H * HEAD_SLOT), lambda b, s: (b * nb + s, 0)),
      ],
      out_shape=[
          jax.ShapeDtypeStruct((B, H, S, HEAD_SLOT), BF16),
          jax.ShapeDtypeStruct((B, H, S, HEAD_SLOT), BF16),
          jax.ShapeDtypeStruct((B * S, H * HEAD_SLOT), BF16),
      ],
      compiler_params=_cparams(("parallel", "parallel")),
      name="mla_prep",
  )(pc, cos_t, sin_t, q_norm, kv_norm, wq, wk, wv, vone, gq, gk)


def _attention_kernel(bound_ref, q_ref, k_ref, v_ref, o_ref):
  tq = q_ref.shape[1]
  S = k_ref.shape[1]
  bound = bound_ref[0]
  nt = (((1,), (1,)), ((), ()))

  def finish(res):
    lane = lax.broadcasted_iota(jnp.int32, (tq, HEAD_SLOT), 1)
    even = res[0] * (1.0 / res[0][:, MLA_V:MLA_V + 1])
    odd = res[1] * (1.0 / res[1][:, 0:1])
    o_ref[...] = jnp.where(lane < MLA_V, even, odd).astype(o_ref.dtype)

  @pl.when(bound <= ATTN_MAX_BOUND)
  def _():
    res = []
    for h in range(2):
      acc = None
      for j in range(S // ATTN_KEY_BLOCK):
        ks = slice(j * ATTN_KEY_BLOCK, (j + 1) * ATTN_KEY_BLOCK)
        s = lax.dot_general(q_ref[h], k_ref[h, ks, :], nt, preferred_element_type=F32)
        p = jnp.exp2(s - bound).astype(BF16)
        c = jnp.dot(p, v_ref[ks, h * HEAD_SLOT:(h + 1) * HEAD_SLOT], preferred_element_type=F32)
        acc = c if acc is None else acc + c
      res.append(acc)
    finish(res)

  @pl.when(bound > ATTN_MAX_BOUND)
  def _():
    res = []
    for h in range(2):
      s = lax.dot_general(q_ref[h], k_ref[h], nt, preferred_element_type=F32)
      p = jnp.exp2(s - jnp.max(s, axis=-1, keepdims=True)).astype(BF16)
      res.append(jnp.dot(p, v_ref[:, h * HEAD_SLOT:(h + 1) * HEAD_SLOT], preferred_element_type=F32))
    finish(res)


def _attention(bound, q, k, v, B, S, tq):
  H = MLA_HEADS
  nq = S // tq
  return pl.pallas_call(
      _attention_kernel,
      grid=(B, H // 2, nq),
      in_specs=[
          pl.BlockSpec(memory_space=pltpu.SMEM),
          pl.BlockSpec((None, 2, tq, HEAD_SLOT), lambda b, hp, i: (b, hp, i, 0)),
          pl.BlockSpec((None, 2, S, HEAD_SLOT), lambda b, hp, i: (b, hp, 0, 0)),
          pl.BlockSpec((S, 2 * HEAD_SLOT), lambda b, hp, i: (b, hp)),
      ],
      out_specs=pl.BlockSpec((tq, 2 * MLA_V), lambda b, hp, i: (b * nq + i, hp)),
      out_shape=jax.ShapeDtypeStruct((B * S, MLA_WIDTH), F32),
      compiler_params=_cparams(("parallel", "parallel", "parallel")),
      name="mla_attention",
  )(bound, q, k, v)


def _mixed_residual(x_ref, ya_ref, yb_ref, yc_ref, gc_ref, w_ref):
  yc = _rms(yc_ref[...], gc_ref[...]).astype(BF16)
  a0, a1 = LRU_WIDTH, LRU_WIDTH + HG_WIDTH
  acc = jnp.dot(ya_ref[...], w_ref[0:a0, :], preferred_element_type=F32)
  acc = acc + jnp.dot(yb_ref[...], w_ref[a0:a1, :], preferred_element_type=F32)
  acc = acc + jnp.dot(yc, w_ref[a1:, :], preferred_element_type=F32)
  return x_ref[...] + acc


def _outproj_kernel(x_ref, ya_ref, yb_ref, yc_ref, gc_ref, w_ref, o_ref):
  o_ref[...] = _mixed_residual(x_ref, ya_ref, yb_ref, yc_ref, gc_ref, w_ref)


def _outproj(x2, ya, yb, yc, gc, w, tm):
  T = x2.shape[0]
  tok = lambda wd: pl.BlockSpec((tm, wd), lambda i: (i, 0))
  return pl.pallas_call(
      _outproj_kernel,
      grid=(T // tm,),
      in_specs=[tok(D_MODEL), tok(LRU_WIDTH), tok(HG_WIDTH), tok(MLA_WIDTH),
                pl.BlockSpec((1, MLA_WIDTH), lambda i: (0, 0)),
                pl.BlockSpec((D_MODEL, D_MODEL), lambda i: (0, 0))],
      out_specs=tok(D_MODEL),
      out_shape=jax.ShapeDtypeStruct((T, D_MODEL), F32),
      compiler_params=_cparams(("parallel",)),
      name="outproj",
  )(x2, ya, yb, yc, gc, w)


def _ffn_kernel(x_ref, ya_ref, yb_ref, yc_ref, gc_ref, wo_ref, g_ref, wg_ref, wu_ref, wd_ref, o_ref):
  x = _mixed_residual(x_ref, ya_ref, yb_ref, yc_ref, gc_ref, wo_ref)
  h = _rms(x, g_ref[...]).astype(BF16)
  F = wg_ref.shape[1]
  acc = x
  for c0 in range(0, F, FFN_CHUNK):
    c1 = min(c0 + FFN_CHUNK, F)
    g = jnp.dot(h, wg_ref[:, c0:c1], preferred_element_type=F32)
    u = jnp.dot(h, wu_ref[:, c0:c1], preferred_element_type=F32)
    a = (_silu(g) * u).astype(BF16)
    acc = acc + jnp.dot(a, wd_ref[c0:c1, :], preferred_element_type=F32)
  o_ref[...] = acc


def _outproj_ffn(x2, ya, yb, yc, gc, w_out, gain, wg, wu, wd, tm):
  T = x2.shape[0]
  F = wg.shape[1]
  tok = lambda w: pl.BlockSpec((tm, w), lambda i: (i, 0))
  resident = lambda shape: pl.BlockSpec(shape, lambda i: (0, 0), pipeline_mode=pl.Buffered(1))
  return pl.pallas_call(
      _ffn_kernel,
      grid=(T // tm,),
      in_specs=[
          tok(D_MODEL), tok(LRU_WIDTH), tok(HG_WIDTH), tok(MLA_WIDTH),
          pl.BlockSpec((1, MLA_WIDTH), lambda i: (0, 0)), resident((D_MODEL, D_MODEL)),
          pl.BlockSpec((1, D_MODEL), lambda i: (0, 0)),
          resident((D_MODEL, F)), resident((D_MODEL, F)), resident((F, D_MODEL)),
      ],
      out_specs=tok(D_MODEL),
      out_shape=jax.ShapeDtypeStruct((T, D_MODEL), F32),
      compiler_params=_cparams(("parallel",)),
      name="outproj_ffn",
  )(x2, ya, yb, yc, gc, w_out, gain, wg, wu, wd)


def _router_kernel(x_ref, g_ref, wr_ref, sel_ref):
  def split(v):
    hi = v.astype(BF16)
    return hi, (v - hi.astype(F32)).astype(BF16)

  h_hi, h_lo = split(_rms(x_ref[...], g_ref[...]))
  w_hi, w_lo = split(wr_ref[...])
  logits = (jnp.dot(h_hi, w_hi, preferred_element_type=F32)
            + jnp.dot(h_lo, w_hi, preferred_element_type=F32)
            + jnp.dot(h_hi, w_lo, preferred_element_type=F32))
  lane = lax.broadcasted_iota(jnp.int32, logits.shape, 1)
  neg = jnp.float32(-jnp.inf)
  lg = jnp.where(lane < N_EXPERTS, logits, neg)
  m1 = jnp.max(lg, axis=-1, keepdims=True)
  i1 = jnp.min(jnp.where(lg == m1, lane, LANE), axis=-1, keepdims=True)
  lg2 = jnp.where(lane == i1, neg, lg)
  m2 = jnp.max(lg2, axis=-1, keepdims=True)
  i2 = jnp.min(jnp.where(lg2 == m2, lane, LANE), axis=-1, keepdims=True)
  e2 = jnp.exp(m2 - m1)
  w1 = 1.0 / (1.0 + e2)
  w2 = e2 / (1.0 + e2)
  sel_ref[...] = jnp.where(lane == 0, i1.astype(F32), jnp.where(lane == 1, i2.astype(F32),
                           jnp.where(lane == 2, w1, jnp.where(lane == 3, w2, 0.0))))


def _router(x2, gain, wr, tm):
  T = x2.shape[0]
  return pl.pallas_call(
      _router_kernel,
      grid=(T // tm,),
      in_specs=[pl.BlockSpec((tm, D_MODEL), lambda i: (i, 0)),
                pl.BlockSpec((1, D_MODEL), lambda i: (0, 0)),
                pl.BlockSpec((D_MODEL, LANE), lambda i: (0, 0))],
      out_specs=pl.BlockSpec((tm, LANE), lambda i: (i, 0)),
      out_shape=jax.ShapeDtypeStruct((T, LANE), F32),
      compiler_params=_cparams(("parallel",)),
      name="moe_router",
  )(x2, gain, wr)


def _route_plan(sel, tm):
  T = sel.shape[0]
  e_flat = sel[:, 0:2].astype(jnp.int32).reshape(1, 2 * T)
  onehot = (e_flat == jnp.arange(N_EXPERTS, dtype=jnp.int32)[:, None]).astype(jnp.int32)
  csum = jnp.cumsum(onehot, axis=1)
  counts = csum[:, -1]
  tiles_e = (counts + tm - 1) // tm
  tile_end = jnp.cumsum(tiles_e)
  offs = (tile_end - tiles_e) * tm
  pos = jnp.sum(onehot * (csum - 1 + offs[:, None]), axis=0)
  n_tiles = (2 * T) // tm + N_EXPERTS
  tile_expert = jnp.sum(jnp.arange(n_tiles, dtype=jnp.int32)[:, None] >= tile_end[None, :], axis=1)
  tile_expert = jnp.minimum(tile_expert, N_EXPERTS - 1).astype(jnp.int32)
  last_tile = jnp.where(tiles_e > 0, offs + (tiles_e - 1) * tm, -1)
  spare = tile_end[-1] + jnp.arange(N_EXPERTS, dtype=jnp.int32)
  spare_tile = jnp.where(spare < n_tiles, spare * tm, -1)
  clear = jnp.concatenate([last_tile, spare_tile]).astype(jnp.int32)
  return pos.astype(jnp.int32), tile_expert, tile_end[-1:].astype(jnp.int32), clear, n_tiles


def _row_copy(src, dst, sem, rows=1, src_row=0, dst_row=0):
  return pltpu.make_async_copy(src.at[pl.ds(src_row, rows), :], dst.at[pl.ds(dst_row, rows), :], sem)


def _dispatch_kernel(pos_ref, lt_ref, x_ref, xs_hbm, zbuf, sem, zsem, *, tm):
  tt = x_ref.shape[0]
  zr = zbuf.shape[0]

  @pl.when(pl.program_id(0) == 0)
  def _():
    zbuf[...] = jnp.zeros_like(zbuf)
    for e in range(lt_ref.shape[0]):
      @pl.when(lt_ref[e] >= 0)
      def _():
        rows = [pl.multiple_of(lt_ref[e] + c * zr, SUBLANE) for c in range(tm // zr)]
        for r0 in rows:
          _row_copy(zbuf, xs_hbm, zsem, zr, 0, r0).start()
        for r0 in rows:
          _row_copy(zbuf, xs_hbm, zsem, zr, 0, r0).wait()

  def body(r, _):
    for c in range(2):
      _row_copy(x_ref, xs_hbm, sem, 1, r, pos_ref[0, 0, 2 * r + c]).start(priority=c)
    return 0

  lax.fori_loop(0, tt, body, 0, unroll=8)
  for c in range(2):
    _row_copy(x_ref, xs_hbm, sem, tt).wait()


def _dispatch(x2, pos, last_tile, n_rows, tt, tm):
  T = x2.shape[0]
  pos3 = pos.reshape(T // tt, 1, 2 * tt)
  zr = min(MOE_ZERO_ROWS, tm)
  return pl.pallas_call(
      functools.partial(_dispatch_kernel, tm=tm),
      grid=(T // tt,),
      in_specs=[pl.BlockSpec((1, 1, 2 * tt), lambda i: (i, 0, 0), memory_space=pltpu.SMEM),
                pl.BlockSpec(memory_space=pltpu.SMEM),
                pl.BlockSpec((tt, D_MODEL), lambda i: (i, 0))],
      out_specs=pl.BlockSpec(memory_space=pl.ANY),
      out_shape=jax.ShapeDtypeStruct((n_rows, D_MODEL), F32),
      scratch_shapes=[pltpu.VMEM((zr, D_MODEL), F32), pltpu.SemaphoreType.DMA(()), pltpu.SemaphoreType.DMA(())],
      compiler_params=_cparams(("arbitrary",)),
      name="moe_dispatch",
  )(pos3, last_tile, x2)


def _moe_group_kernel(te_ref, nu_ref, xs_ref, g_ref, w1_ref, w3_ref, w2_ref, ys_ref, h_ref, acc_ref):
  del te_ref
  i = pl.program_id(0)
  f = pl.program_id(1)
  last = pl.num_programs(1) - 1
  used = i < nu_ref[0]

  @pl.when(used & (f == 0))
  def _():
    h_ref[...] = _rms(xs_ref[...], g_ref[...]).astype(BF16)
    acc_ref[...] = jnp.zeros_like(acc_ref)

  @pl.when(used)
  def _():
    h = h_ref[...]
    g = jnp.dot(h, w1_ref[...].astype(BF16), preferred_element_type=F32)
    u = jnp.dot(h, w3_ref[...].astype(BF16), preferred_element_type=F32)
    a = (_silu(g) * u).astype(BF16)
    acc_ref[...] += jnp.dot(a, w2_ref[...].astype(BF16), preferred_element_type=F32)

  @pl.when(used & (f == last))
  def _():
    ys_ref[...] = acc_ref[...]

  @pl.when(jnp.logical_not(used) & (f == last))
  def _():
    ys_ref[...] = jnp.zeros_like(ys_ref)


def _moe_group(xs, gain, tile_expert, n_used, w1, w3, w2, tm, tf):
  P = xs.shape[0]
  F = w1.shape[2]
  nf = F // tf
  fidx = lambda i, f, te, nu: jnp.where(i < nu[0], f, nf - 1)
  grid_spec = pltpu.PrefetchScalarGridSpec(
      num_scalar_prefetch=2,
      grid=(P // tm, nf),
      in_specs=[
          pl.BlockSpec((tm, D_MODEL), lambda i, f, te, nu: (jnp.minimum(i, nu[0] - 1), 0)),
          pl.BlockSpec((1, D_MODEL), lambda i, f, te, nu: (0, 0)),
          pl.BlockSpec((None, D_MODEL, tf), lambda i, f, te, nu: (te[i], 0, fidx(i, f, te, nu))),
          pl.BlockSpec((None, D_MODEL, tf), lambda i, f, te, nu: (te[i], 0, fidx(i, f, te, nu))),
          pl.BlockSpec((None, tf, D_MODEL), lambda i, f, te, nu: (te[i], fidx(i, f, te, nu), 0)),
      ],
      out_specs=pl.BlockSpec((tm, D_MODEL), lambda i, f, te, nu: (i, 0)),
      scratch_shapes=[pltpu.VMEM((tm, D_MODEL), BF16), pltpu.VMEM((tm, D_MODEL), F32)],
  )
  return pl.pallas_call(
      _moe_group_kernel,
      grid_spec=grid_spec,
      out_shape=jax.ShapeDtypeStruct((P, D_MODEL), F32),
      compiler_params=_cparams(("arbitrary", "arbitrary")),
      name="moe_group",
  )(tile_expert, n_used, xs, gain, w1, w3, w2)


def _combine_kernel(pos_ref, x_ref, sel_ref, ys_hbm, o_ref, ybuf, sem):
  tt = x_ref.shape[0]

  def body(r, _):
    for c in range(2):
      _row_copy(ys_hbm, ybuf.at[c], sem, 1, pos_ref[0, 0, 2 * r + c], r).start(priority=c)
    return 0

  lax.fori_loop(0, tt, body, 0, unroll=8)
  for c in range(2):
    _row_copy(ys_hbm, ybuf.at[c], sem, tt).wait()
  sel = sel_ref[...]
  o_ref[...] = x_ref[...] + sel[:, 2:3] * ybuf[0] + sel[:, 3:4] * ybuf[1]


def _combine(x2, sel, pos, ys, tt):
  T = x2.shape[0]
  pos3 = pos.reshape(T // tt, 1, 2 * tt)
  return pl.pallas_call(
      _combine_kernel,
      grid=(T // tt,),
      in_specs=[pl.BlockSpec((1, 1, 2 * tt), lambda i: (i, 0, 0), memory_space=pltpu.SMEM),
                pl.BlockSpec((tt, D_MODEL), lambda i: (i, 0)),
                pl.BlockSpec((tt, LANE), lambda i: (i, 0)),
                pl.BlockSpec(memory_space=pl.ANY)],
      out_specs=pl.BlockSpec((tt, D_MODEL), lambda i: (i, 0)),
      out_shape=jax.ShapeDtypeStruct((T, D_MODEL), F32),
      scratch_shapes=[pltpu.VMEM((2, tt, D_MODEL), F32), pltpu.SemaphoreType.DMA(())],
      compiler_params=_cparams(("arbitrary",)),
      name="moe_combine",
  )(pos3, x2, sel, ys)


def _block_diag(w):
  n, d, e = w.shape
  eye = jnp.eye(n, dtype=w.dtype)
  return (eye[:, None, :, None] * w[:, :, None, :]).reshape(n * d, n * e)


def _slot_lanes(nope, rope):
  z = lambda n: jnp.zeros(nope.shape[:-1] + (n,), nope.dtype)
  cut = HEAD_SLOT // 2 - ROPE_HALF
  x1, x2 = (z(ROPE_HALF), z(ROPE_HALF)) if rope is None else (rope[..., :ROPE_HALF], rope[..., ROPE_HALF:])
  return jnp.concatenate([x1, nope[..., :cut], x2, nope[..., cut:], z(HEAD_SLOT - MLA_QK)], axis=-1)


def _head_slots(w, width):
  r = w.shape[0]
  w3 = w.reshape(r, MLA_HEADS, width)
  rope = w3[..., MLA_NOPE:] if width == MLA_QK else None
  return _slot_lanes(w3[..., :MLA_NOPE], rope).reshape(r, MLA_HEADS * HEAD_SLOT)


def _slot_gain(g):
  g = g.astype(F32).reshape(1, MLA_QK)
  return _slot_lanes(g[:, :MLA_NOPE], g[:, MLA_NOPE:])


def _value_slots(wv):
  r = wv.shape[0]
  w4 = wv.reshape(r, MLA_HEADS // 2, 2, MLA_V)
  z = jnp.zeros((r, MLA_HEADS // 2, HEAD_SLOT - MLA_V), wv.dtype)
  out = jnp.stack([jnp.concatenate([w4[:, :, 0], z], axis=-1), jnp.concatenate([z, w4[:, :, 1]], axis=-1)], axis=2)
  one = np.zeros((1, MLA_HEADS // 2, 2, HEAD_SLOT), np.float32)
  one[:, :, 0, MLA_V] = 1.0
  one[:, :, 1, 0] = 1.0
  return out.reshape(r, MLA_HEADS * HEAD_SLOT), jnp.asarray(one.reshape(1, MLA_HEADS * HEAD_SLOT))


def _score_bound(gq, gk):
  b = MLA_QK ** 0.5 * math.log2(math.e) * 1.02 * jnp.max(jnp.abs(gq)) * jnp.max(jnp.abs(gk))
  return b.astype(F32).reshape(1)


def kernel(x, positions, norm_mix, w_in, conv_w, conv_b, lru_wa, lru_ba, lru_wx, lru_bx, lru_lam,
           out_g_a, hg_lb_logits, hg_norm_g, mla_q_norm, mla_w_uq, mla_kv_norm, mla_w_ukv,
           qk_norm_q, qk_norm_k, out_g_c, w_out, norm_ffn, ffn_w_gate_up, ffn_w_down,
           moe_router, moe_w1, moe_w3, moe_w2):
  B, S, D = x.shape
  T = B * S
  depth = w_in.shape[0]
  tm = 512
  row = lambda v: v.astype(F32).reshape(1, -1)

  p = jax.nn.softmax(hg_lb_logits.astype(F32), axis=0)
  lower_bounds = jnp.cumsum(p, axis=0) - p[0:1]

  x2 = x.reshape(T, D)
  cos_t, sin_t = _rope_tables(positions.reshape(T, 1), tm)

  for l in range(depth):
    n_main = w_in.shape[2] - MLA_ROPE
    kr_slot_w = _slot_lanes(jnp.zeros((D, MLA_NOPE), F32), w_in[l][:, n_main:])
    w_in_l = jnp.concatenate([w_in[l][:, :n_main], kr_slot_w], axis=1).astype(BF16)
    wg_lru = jnp.concatenate(
        [_block_diag(lru_wa[l, 0]), _block_diag(lru_wx[l, 0]),
         _block_diag(lru_wa[l, 1]), _block_diag(lru_wx[l, 1])], axis=1).astype(BF16)
    bg_lru = jnp.concatenate([lru_ba[l, 0], lru_bx[l, 0], lru_ba[l, 1], lru_bx[l, 1]]).reshape(1, -1)
    wq = _head_slots(mla_w_uq[l], MLA_QK).astype(BF16)
    wkv = mla_w_ukv[l].reshape(MLA_KV_RANK, MLA_HEADS, MLA_NOPE + MLA_V)
    wk = _head_slots(wkv[:, :, :MLA_NOPE].reshape(MLA_KV_RANK, -1), MLA_NOPE).astype(BF16)
    wv, vone = _value_slots(wkv[:, :, MLA_NOPE:].reshape(MLA_KV_RANK, MLA_WIDTH))
    wv = wv.astype(BF16)

    pa, pb, pc = _norm_inproj(x2, row(norm_mix[l]), w_in_l, min(PROJ_ROWS, T))
    ya = _rglru(pa, conv_w[l], row(conv_b[l]), wg_lru, bg_lru, lru_lam[l], row(out_g_a[l]), B, S)
    yb = _hgrn2(pb, lower_bounds[l], row(jnp.tile(hg_norm_g[l], HG_HEADS)), B, S)
    q, k, v = _mla_prep(pc, cos_t, sin_t, row(mla_q_norm[l]), row(mla_kv_norm[l]), wq, wk, wv, vone,
                        _slot_gain(qk_norm_q[l]) * (MLA_QK ** -0.5 * math.log2(math.e)),
                        _slot_gain(qk_norm_k[l]), B, S, min(MLA_PREP_ROWS, S))
    yc = _attention(_score_bound(qk_norm_q[l], qk_norm_k[l]), q, k, v, B, S, min(ATTN_Q_BLOCK, S))
    if l % 2 == 0:
      wgu = ffn_w_gate_up[l // 2]
      pad = D_FF_PAD - D_FF
      wg = jnp.pad(wgu[:, :D_FF], ((0, 0), (0, pad))).astype(BF16)
      wu = jnp.pad(wgu[:, D_FF:], ((0, 0), (0, pad))).astype(BF16)
      wd = jnp.pad(ffn_w_down[l // 2], ((0, pad), (0, 0))).astype(BF16)
      x2 = _outproj_ffn(x2, ya, yb, yc, row(out_g_c[l]), w_out[l].astype(BF16), row(norm_ffn[l]), wg, wu, wd, tm)
    else:
      x2 = _outproj(x2, ya, yb, yc, row(out_g_c[l]), w_out[l].astype(BF16), min(PROJ_ROWS, T))
      wr = jnp.pad(moe_router[l // 2].astype(F32), ((0, 0), (0, LANE - N_EXPERTS)))
      gain = row(norm_ffn[l])
      tme = 1024 if T >= 8192 else 128
      sel = _router(x2, gain, wr, tm)
      pos, tile_expert, n_used, last_tile, n_tiles = _route_plan(sel, tme)
      xs = _dispatch(x2, pos, last_tile, n_tiles * tme, tm, tme)
      ys = _moe_group(xs, gain, tile_expert, n_used, moe_w1[l // 2], moe_w3[l // 2], moe_w2[l // 2],
                      tme, D_EXPERT // 7)
      x2 = _combine(x2, sel, pos, ys, tm)
  return x2.reshape(B, S, D)
```

```python
import functools
import math

import jax
import jax.numpy as jnp
import numpy as np
from jax import lax
from jax.experimental import pallas as pl
from jax.experimental.pallas import tpu as pltpu

F32 = jnp.float32
BF16 = jnp.bfloat16

D_MODEL = 1024
LRU_WIDTH = 256
LRU_BLOCKS = 4
CONV_WIDTH = 4
LRU_C = 8.0
HG_HEADS = 4
HG_DK = 64
HG_WIDTH = 256
MLA_HEADS = 8
MLA_NOPE = 64
MLA_ROPE = 32
MLA_V = 64
MLA_QK = MLA_NOPE + MLA_ROPE
MLA_Q_RANK = 256
MLA_KV_RANK = 128
MLA_WIDTH = MLA_HEADS * MLA_V
ROPE_THETA = 10000.0
D_FF = 2752
N_EXPERTS = 8
D_EXPERT = 3584
EPS = 1e-6

LANE = 128
SUBLANE = 8
HEAD_SLOT = 128
HG_CHUNK = 64
D_FF_PAD = 2816
MOE_ZERO_ROWS = 256
FFN_CHUNK = 1024
PROJ_ROWS = 1024
VMEM_LIMIT = 56 * 1024 * 1024

PA_W, PB_W, PC_W = 512, 1280, 512


def _cparams(sem):
  return pltpu.CompilerParams(dimension_semantics=sem, vmem_limit_bytes=VMEM_LIMIT)


def _sigmoid(x):
  return 0.5 + 0.5 * jnp.tanh(0.5 * x)


def _silu(x):
  return x * _sigmoid(x)


def _rms(x, gain, n=None):
  n = x.shape[-1] if n is None else n
  ms = jnp.sum(x * x, axis=-1, keepdims=True) * (1.0 / n)
  return x * lax.rsqrt(ms + EPS) * gain


def _norm_inproj_kernel(x_ref, g_ref, w_ref, oa_ref, ob_ref, oc_ref):
  h = _rms(x_ref[...], g_ref[...]).astype(BF16)
  p = jnp.dot(h, w_ref[...], preferred_element_type=F32)
  oa_ref[...] = p[:, :PA_W]
  ob_ref[...] = p[:, PA_W:PA_W + PB_W]
  oc_ref[...] = p[:, PA_W + PB_W:]


def _norm_inproj(x2, gain, w, tm):
  T = x2.shape[0]
  n = w.shape[1]
  return pl.pallas_call(
      _norm_inproj_kernel,
      grid=(T // tm,),
      in_specs=[
          pl.BlockSpec((tm, D_MODEL), lambda i: (i, 0)),
          pl.BlockSpec((1, D_MODEL), lambda i: (0, 0)),
          pl.BlockSpec((D_MODEL, n), lambda i: (0, 0)),
      ],
      out_specs=[
          pl.BlockSpec((tm, PA_W), lambda i: (i, 0)),
          pl.BlockSpec((tm, PB_W), lambda i: (i, 0)),
          pl.BlockSpec((tm, PC_W), lambda i: (i, 0)),
      ],
      out_shape=[
          jax.ShapeDtypeStruct((T, PA_W), F32),
          jax.ShapeDtypeStruct((T, PB_W), F32),
          jax.ShapeDtypeStruct((T, PC_W), F32),
      ],
      compiler_params=_cparams(("parallel",)),
      name="norm_inproj",
  )(x2, gain, w)


def _rglru_kernel(pa_ref, cw_ref, cb_ref, wg_ref, bg_ref, lam_ref, og_ref, o_ref,
                  af_ref, bf_ref, ab_ref, bb_ref, hf_ref, hb_ref):
  S = pa_ref.shape[0]
  W = LRU_WIDTH
  xa = pa_ref[:, 0:W]
  row = lax.broadcasted_iota(jnp.int32, (S, W), 0)

  xc = cb_ref[...] + cw_ref[2:3, :] * xa
  for k in (0, 1, 3):
    d = k - CONV_WIDTH // 2
    shifted = pltpu.roll(xa, (-d) % S, 0)
    valid = (row >= -d) if d < 0 else (row < S - d)
    xc = xc + cw_ref[k:k + 1, :] * jnp.where(valid, shifted, 0.0)

  z = jnp.dot(xc.astype(BF16), wg_ref[...], preferred_element_type=F32) + bg_ref[...]
  lam = lam_ref[...]
  sp = jnp.maximum(-lam, 0.0) + jnp.log1p(jnp.exp(-jnp.abs(lam)))
  for d, (a_ref, b_ref) in enumerate(((af_ref, bf_ref), (ab_ref, bb_ref))):
    r = _sigmoid(z[:, d * 2 * W:d * 2 * W + W])
    i = _sigmoid(z[:, d * 2 * W + W:(d + 1) * 2 * W])
    log_a = (-LRU_C) * r * sp[d:d + 1, :]
    a = jnp.exp(log_a)
    a_ref[...] = a
    y2 = 2.0 * log_a
    u = a * a
    near = jnp.where(u == 1.0, y2, (u - 1.0) * y2 / jnp.log(u))
    em1 = jnp.where(y2 > -0.5, near, u - 1.0)
    b_ref[...] = jnp.sqrt(-em1) * (i * xc)

  n = S // SUBLANE
  r8 = lax.broadcasted_iota(jnp.int32, (SUBLANE, W), 0)

  def body(c, carry):
    cf, cb = carry
    sl = pl.ds(pl.multiple_of(c * SUBLANE, SUBLANE), SUBLANE)
    a = af_ref[sl, :]
    b = bf_ref[sl, :]
    for s in (1, 2, 4):
      m = r8 >= s
      b = jnp.where(m, a * pltpu.roll(b, s, 0) + b, b)
      a = jnp.where(m, a * pltpu.roll(a, s, 0), a)
    h = b + a * cf
    hf_ref[sl, :] = h
    cf = h[SUBLANE - 1:SUBLANE, :]

    sl = pl.ds(pl.multiple_of((n - 1 - c) * SUBLANE, SUBLANE), SUBLANE)
    a = ab_ref[sl, :]
    b = bb_ref[sl, :]
    for s in (1, 2, 4):
      m = r8 < SUBLANE - s
      b = jnp.where(m, a * pltpu.roll(b, SUBLANE - s, 0) + b, b)
      a = jnp.where(m, a * pltpu.roll(a, SUBLANE - s, 0), a)
    h = b + a * cb
    hb_ref[sl, :] = h
    cb = h[0:1, :]
    return cf, cb

  zero = jnp.zeros((1, W), F32)
  lax.fori_loop(0, n, body, (zero, zero), unroll=8)

  g = pa_ref[:, W:2 * W]
  gelu = 0.5 * g * (1.0 + jnp.tanh(math.sqrt(2.0 / math.pi) * (g + 0.044715 * (g * g * g))))
  y = (hf_ref[...] + hb_ref[...]) * gelu
  o_ref[...] = _rms(y, og_ref[...]).astype(o_ref.dtype)


def _rglru(pa, conv_w, conv_b, wg, bg, lam, out_g, B, S):
  T = B * S
  W = LRU_WIDTH
  const = lambda shape: pl.BlockSpec(shape, lambda b: (0,) * len(shape))
  return pl.pallas_call(
      _rglru_kernel,
      grid=(B,),
      in_specs=[
          pl.BlockSpec((S, PA_W), lambda b: (b, 0)),
          const((CONV_WIDTH, W)), const((1, W)), const((W, 4 * W)), const((1, 4 * W)),
          const((2, W)), const((1, W)),
      ],
      out_specs=pl.BlockSpec((S, W), lambda b: (b, 0)),
      out_shape=jax.ShapeDtypeStruct((T, W), BF16),
      scratch_shapes=[pltpu.VMEM((S, W), F32) for _ in range(6)],
      compiler_params=_cparams(("parallel",)),
      name="rglru",
  )(pa, conv_w, conv_b, wg, bg, lam, out_g)


_HG_LEVELS = (1, 2, 4, 8, 16, 32)
HG_SAFE_LOG = -80.0


def _bcast_row_in_blocks(x, block, row):
  L, C = x.shape
  x3 = x.reshape(L // block, block, C)
  return jnp.broadcast_to(x3[:, row:row + 1, :], x3.shape).reshape(L, C)


def _inc_prefix_products(f, row):
  L = f.shape[0]
  out = {1: f}
  p2 = f * jnp.where((row & 1) == 1, pltpu.roll(f, 1, 0), 1.0)
  out[2] = p2
  r4 = row & 3
  p4 = p2 * jnp.where(r4 == 2, pltpu.roll(p2, 1, 0), jnp.where(r4 == 3, pltpu.roll(p2, 2, 0), 1.0))
  out[4] = p4
  p, h = p4, 4
  while h < L:
    p = p * jnp.where((row & h) != 0, _bcast_row_in_blocks(p, 2 * h, h - 1), 1.0)
    h *= 2
    out[h] = p
  return out


def _inc_suffix_products(f, row):
  L = f.shape[0]
  up = lambda x, s: pltpu.roll(x, L - s, 0)
  out = {1: f}
  p2 = f * jnp.where((row & 1) == 0, up(f, 1), 1.0)
  out[2] = p2
  r4 = row & 3
  p4 = p2 * jnp.where(r4 == 1, up(p2, 1), jnp.where(r4 == 0, up(p2, 2), 1.0))
  out[4] = p4
  p, h = p4, 4
  while h < L:
    p = p * jnp.where((row & h) == 0, _bcast_row_in_blocks(p, 2 * h, h), 1.0)
    h *= 2
    out[h] = p
  return out


def _hg_chunk(q, f, v, st_ref, masks_ref, hmask, row, reverse):
  L = q.shape[0]
  k = 1.0 - f
  pre = _inc_prefix_products(f, row)
  suf = _inc_suffix_products(f, row)
  if not reverse:
    q_dec = pre
    k_dec = {h: (jnp.where((row & (h - 1)) == h - 1, 1.0, pltpu.roll(suf[h], L - 1, 0)) if h > 1 else None)
             for h in suf}
  else:
    q_dec = suf
    k_dec = {h: (jnp.where((row & (h - 1)) == 0, 1.0, pltpu.roll(pre[h], 1, 0)) if h > 1 else None)
             for h in pre}

  def head_blockdiag(x):
    return jnp.concatenate([x] * HG_HEADS, axis=0) * hmask

  nt = (((1,), (1,)), ((), ()))
  kb = k.astype(BF16)
  p = masks_ref[0] * lax.dot_general(q.astype(BF16), head_blockdiag(kb), nt, preferred_element_type=F32)
  for li, h in enumerate(_HG_LEVELS):
    qh = (q * q_dec[h]).astype(BF16)
    kh = kb if h == 1 else (k * k_dec[h]).astype(BF16)
    p = p + masks_ref[li + 1] * lax.dot_general(qh, head_blockdiag(kh), nt, preferred_element_type=F32)

  vb = v.astype(BF16)
  o = jnp.dot(p.astype(BF16), head_blockdiag(vb), preferred_element_type=F32)
  st = st_ref[...]
  o = o + lax.dot_general((q * q_dec[L]).astype(BF16), st.astype(BF16), nt, preferred_element_type=F32)

  ke = (k * k_dec[L]).astype(BF16)
  end = q_dec[L][0:1, :] if reverse else q_dec[L][L - 1:L, :]
  tn = (((0,), (0,)), ((), ()))
  upd = lax.dot_general(vb, ke, tn, preferred_element_type=F32)
  st_ref[...] = st * end + upd * hmask.astype(F32)
  return o


def _hg_chunk_fast(q, f, v, st_ref, masks_ref, hmask, row, reverse):
  L = q.shape[0]
  a = (_inc_suffix_products(f, row) if reverse else _inc_prefix_products(f, row))[L]
  qt = (q * a).astype(BF16)
  kt = ((1.0 - f) * (1.0 / a)).astype(BF16)

  def head_blockdiag(x):
    return jnp.concatenate([x] * HG_HEADS, axis=0) * hmask

  nt = (((1,), (1,)), ((), ()))
  tn = (((0,), (0,)), ((), ()))
  p = jnp.where(masks_ref[len(_HG_LEVELS) + 1] != 0.0,
                lax.dot_general(qt, head_blockdiag(kt), nt, preferred_element_type=F32), 0.0)
  vb = v.astype(BF16)
  st = st_ref[...]
  o = jnp.dot(p.astype(BF16), head_blockdiag(vb), preferred_element_type=F32)
  o = o + lax.dot_general(qt, st.astype(BF16), nt, preferred_element_type=F32)
  end = a[0:1, :] if reverse else a[L - 1:L, :]
  upd = lax.dot_general(vb, kt, tn, preferred_element_type=F32)
  st_ref[...] = (st + upd * hmask.astype(F32)) * end
  return o


def _hgrn2_kernel(pb_ref, lb_ref, ng_ref, mf_ref, mb_ref, hm_ref, ones_ref, o_ref,
                  q_ref, ff_ref, fb_ref, of_ref, ob_ref, sf_ref, sb_ref):
  S = pb_ref.shape[0]
  W = HG_WIDTH
  L = HG_CHUNK
  n = S // L
  row = lax.broadcasted_iota(jnp.int32, (L, W), 0)
  hmask = hm_ref[...]
  sf_ref[...] = jnp.zeros_like(sf_ref)
  sb_ref[...] = jnp.zeros_like(sb_ref)

  def full_reduce(x, op):
    return op(op(x, axis=1, keepdims=True), axis=0, keepdims=True)

  q = _silu(pb_ref[:, 0:W]) * (HG_DK ** -0.5)
  q_ref[...] = q
  margin = (jnp.log(1.0 + full_reduce(jnp.abs(q), jnp.max))
            + jnp.log(1.0 + full_reduce(jnp.abs(pb_ref[:, 3 * W:4 * W]), jnp.max)))
  log_min = None
  for d, f_ref in enumerate((ff_ref, fb_ref)):
    lb = lb_ref[d:d + 1, :]
    f = lb + (1.0 - lb) * _sigmoid(pb_ref[:, (d + 1) * W:(d + 2) * W])
    f_ref[...] = f
    chunk_log = jnp.sum(jnp.log(f).reshape(n, L, W), axis=1)
    m = full_reduce(chunk_log, jnp.min)
    log_min = m if log_min is None else jnp.minimum(log_min, m)
  in_range = (log_min - margin)[0, 0] >= HG_SAFE_LOG

  def make_body(chunk_fn):
    def body(c, _):
      for f_ref, out_ref, st_ref, m_ref, rev, cc in ((ff_ref, of_ref, sf_ref, mf_ref, False, c),
                                                     (fb_ref, ob_ref, sb_ref, mb_ref, True, n - 1 - c)):
        sl = pl.ds(pl.multiple_of(cc * L, L), L)
        out_ref[sl, :] = chunk_fn(q_ref[sl, :], f_ref[sl, :], pb_ref[sl, 3 * W:4 * W], st_ref, m_ref,
                                  hmask, row, rev)
      return 0
    return body

  @pl.when(in_range)
  def _():
    lax.fori_loop(0, n, make_body(_hg_chunk_fast), 0, unroll=8)

  @pl.when(jnp.logical_not(in_range))
  def _():
    lax.fori_loop(0, n, make_body(_hg_chunk), 0)

  o = of_ref[...] + ob_ref[...]
  ms = jnp.dot((o * o).astype(BF16), ones_ref[...], preferred_element_type=F32) * (1.0 / HG_DK)
  y = o * lax.rsqrt(ms + EPS) * ng_ref[...]
  o_ref[...] = (y * _silu(pb_ref[:, 4 * W:5 * W])).astype(o_ref.dtype)


def _hg_constants():
  L = HG_CHUNK
  t = jnp.arange(L)[:, None]
  s = jnp.arange(L)[None, :]
  fwd, bwd = [t == s], [t == s]
  for h in _HG_LEVELS:
    same = (t // (2 * h)) == (s // (2 * h))
    fwd.append(same & ((t & h) != 0) & ((s & h) == 0))
    bwd.append(same & ((t & h) == 0) & ((s & h) != 0))
  fwd.append(t >= s)
  bwd.append(t <= s)
  tile = lambda m: jnp.tile(jnp.stack(m).astype(F32), (1, 1, HG_HEADS))
  hid = jnp.arange(HG_WIDTH) // HG_DK
  hmask = (hid[:, None] == hid[None, :])
  return tile(fwd), tile(bwd), hmask.astype(BF16), hmask.astype(BF16)


def _hgrn2(pb, lb, norm_g, B, S):
  T = B * S
  W = HG_WIDTH
  mf, mb, hmask, ones = _hg_constants()
  const = lambda shape: pl.BlockSpec(shape, lambda b: (0,) * len(shape))
  return pl.pallas_call(
      _hgrn2_kernel,
      grid=(B,),
      in_specs=[
          pl.BlockSpec((S, PB_W), lambda b: (b, 0)),
          const((2, W)), const((1, W)), const(mf.shape), const(mb.shape), const((W, W)), const((W, W)),
      ],
      out_specs=pl.BlockSpec((S, W), lambda b: (b, 0)),
      out_shape=jax.ShapeDtypeStruct((T, W), BF16),
      scratch_shapes=[pltpu.VMEM((S, W), F32) for _ in range(5)]
                     + [pltpu.VMEM((W, W), F32), pltpu.VMEM((W, W), F32)],
      compiler_params=_cparams(("parallel",)),
      name="hgrn2",
  )(pb, lb, norm_g, mf, mb, hmask, ones)


ROPE_HALF = MLA_ROPE // 2
ATTN_MAX_BOUND = 60.0
ATTN_KEY_BLOCK = 256
ATTN_Q_BLOCK = 2048
MLA_PREP_ROWS = 1024


def _rope_tables_kernel(pos_ref, invf_ref, sgn_ref, cos_ref, sin_ref):
  ang = pos_ref[...].astype(F32) * invf_ref[...]
  cos_ref[...] = jnp.cos(ang)
  sin_ref[...] = sgn_ref[...] * jnp.sin(ang)


def _rope_tables(pos2, tm):
  T = pos2.shape[0]
  inv_freq = (ROPE_THETA ** (-jnp.arange(0, ROPE_HALF, dtype=F32) * (2.0 / MLA_ROPE))).reshape(1, ROPE_HALF)
  ones = jnp.ones((1, ROPE_HALF), F32)
  no_nope = jnp.zeros((1, MLA_NOPE), F32)
  invf = _slot_lanes(no_nope, jnp.concatenate([inv_freq, inv_freq], axis=1))
  sgn = _slot_lanes(no_nope, jnp.concatenate([-ones, ones], axis=1))
  row = pl.BlockSpec((1, HEAD_SLOT), lambda i: (0, 0))
  return pl.pallas_call(
      _rope_tables_kernel,
      grid=(T // tm,),
      in_specs=[pl.BlockSpec((tm, 1), lambda i: (i, 0)), row, row],
      out_specs=[pl.BlockSpec((tm, HEAD_SLOT), lambda i: (i, 0))] * 2,
      out_shape=[jax.ShapeDtypeStruct((T, HEAD_SLOT), F32)] * 2,
      compiler_params=_cparams(("parallel",)),
      name="rope_tables",
  )(pos2, invf, sgn)


def _rope(t, cos_t, sin_t):
  return t * cos_t + pltpu.roll(t, HEAD_SLOT // 2, 1) * sin_t


def _mla_prep_kernel(pc_ref, cos_ref, sin_ref, qn_ref, kvn_ref, wq_ref, wk_ref, wv_ref, vone_ref, gq_ref, gk_ref,
                     q_ref, k_ref, v_ref):
  cos_t = cos_ref[...]
  sin_t = sin_ref[...]
  cq = _rms(pc_ref[:, 0:MLA_Q_RANK], qn_ref[...]).astype(BF16)
  ckv = _rms(pc_ref[:, MLA_Q_RANK:MLA_Q_RANK + MLA_KV_RANK], kvn_ref[...]).astype(BF16)
  kr_slot = pc_ref[:, MLA_Q_RANK + MLA_KV_RANK:]
  q_all = jnp.dot(cq, wq_ref[...], preferred_element_type=F32)
  k_all = jnp.dot(ckv, wk_ref[...], preferred_element_type=F32)
  v_ref[...] = (jnp.dot(ckv, wv_ref[...], preferred_element_type=F32) + vone_ref[...]).astype(v_ref.dtype)
  ones = jnp.ones((HEAD_SLOT, HEAD_SLOT), BF16)

  def rms_mxu(x, gain):
    sq = x * x
    hi = sq.astype(BF16)
    lo = (sq - hi.astype(F32)).astype(BF16)
    ms = (jnp.dot(hi, ones, preferred_element_type=F32) + jnp.dot(lo, ones, preferred_element_type=F32)) * (1.0 / MLA_QK)
    return x * lax.rsqrt(ms + EPS) * gain

  for h in range(MLA_HEADS):
    sl = slice(h * HEAD_SLOT, (h + 1) * HEAD_SLOT)
    qh = _rope(rms_mxu(q_all[:, sl], gq_ref[...]), cos_t, sin_t)
    q_ref[h] = qh.astype(q_ref.dtype)
    kh = _rope(rms_mxu(k_all[:, sl] + kr_slot, gk_ref[...]), cos_t, sin_t)
    k_ref[h] = kh.astype(k_ref.dtype)


def _mla_prep(pc, cos_t, sin_t, q_norm, kv_norm, wq, wk, wv, vone, gq, gk, B, S, ts):
  H = MLA_HEADS
  nb = S // ts
  const = lambda shape: pl.BlockSpec(shape, lambda b, s: (0,) * len(shape))
  tok = lambda w: pl.BlockSpec((ts, w), lambda b, s: (b * nb + s, 0))
  return pl.pallas_call(
      _mla_prep_kernel,
      grid=(B, nb),
      in_specs=[
          tok(PC_W), tok(HEAD_SLOT), tok(HEAD_SLOT),
          const((1, MLA_Q_RANK)), const((1, MLA_KV_RANK)),
          const(wq.shape), const(wk.shape), const(wv.shape), const(vone.shape),
          const((1, HEAD_SLOT)), const((1, HEAD_SLOT)),
      ],
      out_specs=[
          pl.BlockSpec((None, H, ts, HEAD_SLOT), lambda b, s: (b, 0, s, 0)),
          pl.BlockSpec((None, H, ts, HEAD_SLOT), lambda b, s: (b, 0, s, 0)),
          pl.BlockSpec((ts, H * HEAD_SLOT), lambda b, s: (b * nb + s, 0)),
      ],
      out_shape=[
          jax.ShapeDtypeStruct((B, H, S, HEAD_SLOT), BF16),
          jax.ShapeDtypeStruct((B, H, S, HEAD_SLOT), BF16),
          jax.ShapeDtypeStruct((B * S, H * HEAD_SLOT), BF16),
      ],
      compiler_params=_cparams(("parallel", "parallel")),
      name="mla_prep",
  )(pc, cos_t, sin_t, q_norm, kv_norm, wq, wk, wv, vone, gq, gk)


def _attention_kernel(bound_ref, q_ref, k_ref, v_ref, o_ref):
  tq = q_ref.shape[1]
  S = k_ref.shape[1]
  bound = bound_ref[0]
  nt = (((1,), (1,)), ((), ()))

  def finish(res):
    lane = lax.broadcasted_iota(jnp.int32, (tq, HEAD_SLOT), 1)
    even = res[0] * (1.0 / res[0][:, MLA_V:MLA_V + 1])
    odd = res[1] * (1.0 / res[1][:, 0:1])
    o_ref[...] = jnp.where(lane < MLA_V, even, odd).astype(o_ref.dtype)

  @pl.when(bound <= ATTN_MAX_BOUND)
  def _():
    res = []
    for h in range(2):
      acc = None
      for j in range(S // ATTN_KEY_BLOCK):
        ks = slice(j * ATTN_KEY_BLOCK, (j + 1) * ATTN_KEY_BLOCK)
        s = lax.dot_general(q_ref[h], k_ref[h, ks, :], nt, preferred_element_type=F32)
        p = jnp.exp2(s - bound).astype(BF16)
        c = jnp.dot(p, v_ref[ks, h * HEAD_SLOT:(h + 1) * HEAD_SLOT], preferred_element_type=F32)
        acc = c if acc is None else acc + c
      res.append(acc)
    finish(res)

  @pl.when(bound > ATTN_MAX_BOUND)
  def _():
    res = []
    for h in range(2):
      s = lax.dot_general(q_ref[h], k_ref[h], nt, preferred_element_type=F32)
      p = jnp.exp2(s - jnp.max(s, axis=-1, keepdims=True)).astype(BF16)
      res.append(jnp.dot(p, v_ref[:, h * HEAD_SLOT:(h + 1) * HEAD_SLOT], preferred_element_type=F32))
    finish(res)


def _attention(bound, q, k, v, B, S, tq):
  H = MLA_HEADS
  nq = S // tq
  return pl.pallas_call(
      _attention_kernel,
      grid=(B, H // 2, nq),
      in_specs=[
          pl.BlockSpec(memory_space=pltpu.SMEM),
          pl.BlockSpec((None, 2, tq, HEAD_SLOT), lambda b, hp, i: (b, hp, i, 0)),
          pl.BlockSpec((None, 2, S, HEAD_SLOT), lambda b, hp, i: (b, hp, 0, 0)),
          pl.BlockSpec((S, 2 * HEAD_SLOT), lambda b, hp, i: (b, hp)),
      ],
      out_specs=pl.BlockSpec((tq, 2 * MLA_V), lambda b, hp, i: (b * nq + i, hp)),
      out_shape=jax.ShapeDtypeStruct((B * S, MLA_WIDTH), F32),
      compiler_params=_cparams(("parallel", "parallel", "parallel")),
      name="mla_attention",
  )(bound, q, k, v)


def _mixed_residual(x_ref, ya_ref, yb_ref, yc_ref, gc_ref, w_ref):
  yc = _rms(yc_ref[...], gc_ref[...]).astype(BF16)
  a0, a1 = LRU_WIDTH, LRU_WIDTH + HG_WIDTH
  acc = jnp.dot(ya_ref[...], w_ref[0:a0, :], preferred_element_type=F32)
  acc = acc + jnp.dot(yb_ref[...], w_ref[a0:a1, :], preferred_element_type=F32)
  acc = acc + jnp.dot(yc, w_ref[a1:, :], preferred_element_type=F32)
  return x_ref[...] + acc


def _outproj_kernel(x_ref, ya_ref, yb_ref, yc_ref, gc_ref, w_ref, o_ref):
  o_ref[...] = _mixed_residual(x_ref, ya_ref, yb_ref, yc_ref, gc_ref, w_ref)


def _outproj(x2, ya, yb, yc, gc, w, tm):
  T = x2.shape[0]
  tok = lambda wd: pl.BlockSpec((tm, wd), lambda i: (i, 0))
  return pl.pallas_call(
      _outproj_kernel,
      grid=(T // tm,),
      in_specs=[tok(D_MODEL), tok(LRU_WIDTH), tok(HG_WIDTH), tok(MLA_WIDTH),
                pl.BlockSpec((1, MLA_WIDTH), lambda i: (0, 0)),
                pl.BlockSpec((D_MODEL, D_MODEL), lambda i: (0, 0))],
      out_specs=tok(D_MODEL),
      out_shape=jax.ShapeDtypeStruct((T, D_MODEL), F32),
      compiler_params=_cparams(("parallel",)),
      name="outproj",
  )(x2, ya, yb, yc, gc, w)


def _ffn_kernel(x_ref, ya_ref, yb_ref, yc_ref, gc_ref, wo_ref, g_ref, wg_ref, wu_ref, wd_ref, o_ref):
  x = _mixed_residual(x_ref, ya_ref, yb_ref, yc_ref, gc_ref, wo_ref)
  h = _rms(x, g_ref[...]).astype(BF16)
  F = wg_ref.shape[1]
  acc = x
  for c0 in range(0, F, FFN_CHUNK):
    c1 = min(c0 + FFN_CHUNK, F)
    g = jnp.dot(h, wg_ref[:, c0:c1], preferred_element_type=F32)
    u = jnp.dot(h, wu_ref[:, c0:c1], preferred_element_type=F32)
    a = (_silu(g) * u).astype(BF16)
    acc = acc + jnp.dot(a, wd_ref[c0:c1, :], preferred_element_type=F32)
  o_ref[...] = acc


def _outproj_ffn(x2, ya, yb, yc, gc, w_out, gain, wg, wu, wd, tm):
  T = x2.shape[0]
  F = wg.shape[1]
  tok = lambda w: pl.BlockSpec((tm, w), lambda i: (i, 0))
  resident = lambda shape: pl.BlockSpec(shape, lambda i: (0, 0), pipeline_mode=pl.Buffered(1))
  return pl.pallas_call(
      _ffn_kernel,
      grid=(T // tm,),
      in_specs=[
          tok(D_MODEL), tok(LRU_WIDTH), tok(HG_WIDTH), tok(MLA_WIDTH),
          pl.BlockSpec((1, MLA_WIDTH), lambda i: (0, 0)), resident((D_MODEL, D_MODEL)),
          pl.BlockSpec((1, D_MODEL), lambda i: (0, 0)),
          resident((D_MODEL, F)), resident((D_MODEL, F)), resident((F, D_MODEL)),
      ],
      out_specs=tok(D_MODEL),
      out_shape=jax.ShapeDtypeStruct((T, D_MODEL), F32),
      compiler_params=_cparams(("parallel",)),
      name="outproj_ffn",
  )(x2, ya, yb, yc, gc, w_out, gain, wg, wu, wd)


def _router_kernel(x_ref, g_ref, wr_ref, sel_ref):
  def split(v):
    hi = v.astype(BF16)
    return hi, (v - hi.astype(F32)).astype(BF16)

  h_hi, h_lo = split(_rms(x_ref[...], g_ref[...]))
  w_hi, w_lo = split(wr_ref[...])
  logits = (jnp.dot(h_hi, w_hi, preferred_element_type=F32)
            + jnp.dot(h_lo, w_hi, preferred_element_type=F32)
            + jnp.dot(h_hi, w_lo, preferred_element_type=F32))
  lane = lax.broadcasted_iota(jnp.int32, logits.shape, 1)
  neg = jnp.float32(-jnp.inf)
  lg = jnp.where(lane < N_EXPERTS, logits, neg)
  m1 = jnp.max(lg, axis=-1, keepdims=True)
  i1 = jnp.min(jnp.where(lg == m1, lane, LANE), axis=-1, keepdims=True)
  lg2 = jnp.where(lane == i1, neg, lg)
  m2 = jnp.max(lg2, axis=-1, keepdims=True)
  i2 = jnp.min(jnp.where(lg2 == m2, lane, LANE), axis=-1, keepdims=True)
  e2 = jnp.exp(m2 - m1)
  w1 = 1.0 / (1.0 + e2)
  w2 = e2 / (1.0 + e2)
  sel_ref[...] = jnp.where(lane == 0, i1.astype(F32), jnp.where(lane == 1, i2.astype(F32),
                           jnp.where(lane == 2, w1, jnp.where(lane == 3, w2, 0.0))))


def _router(x2, gain, wr, tm):
  T = x2.shape[0]
  return pl.pallas_call(
      _router_kernel,
      grid=(T // tm,),
      in_specs=[pl.BlockSpec((tm, D_MODEL), lambda i: (i, 0)),
                pl.BlockSpec((1, D_MODEL), lambda i: (0, 0)),
                pl.BlockSpec((D_MODEL, LANE), lambda i: (0, 0))],
      out_specs=pl.BlockSpec((tm, LANE), lambda i: (i, 0)),
      out_shape=jax.ShapeDtypeStruct((T, LANE), F32),
      compiler_params=_cparams(("parallel",)),
      name="moe_router",
  )(x2, gain, wr)


def _route_plan(sel, tm):
  T = sel.shape[0]
  e_flat = sel[:, 0:2].astype(jnp.int32).reshape(1, 2 * T)
  onehot = (e_flat == jnp.arange(N_EXPERTS, dtype=jnp.int32)[:, None]).astype(jnp.int32)
  csum = jnp.cumsum(onehot, axis=1)
  counts = csum[:, -1]
  tiles_e = (counts + tm - 1) // tm
  tile_end = jnp.cumsum(tiles_e)
  offs = (tile_end - tiles_e) * tm
  pos = jnp.sum(onehot * (csum - 1 + offs[:, None]), axis=0)
  n_tiles = (2 * T) // tm + N_EXPERTS
  tile_expert = jnp.sum(jnp.arange(n_tiles, dtype=jnp.int32)[:, None] >= tile_end[None, :], axis=1)
  tile_expert = jnp.minimum(tile_expert, N_EXPERTS - 1).astype(jnp.int32)
  last_tile = jnp.where(tiles_e > 0, offs + (tiles_e - 1) * tm, -1)
  spare = tile_end[-1] + jnp.arange(N_EXPERTS, dtype=jnp.int32)
  spare_tile = jnp.where(spare < n_tiles, spare * tm, -1)
  clear = jnp.concatenate([last_tile, spare_tile]).astype(jnp.int32)
  return pos.astype(jnp.int32), tile_expert, tile_end[-1:].astype(jnp.int32), clear, n_tiles


def _row_copy(src, dst, sem, rows=1, src_row=0, dst_row=0):
  return pltpu.make_async_copy(src.at[pl.ds(src_row, rows), :], dst.at[pl.ds(dst_row, rows), :], sem)


def _dispatch_kernel(pos_ref, lt_ref, x_ref, xs_hbm, zbuf, sem, zsem, *, tm):
  tt = x_ref.shape[0]
  zr = zbuf.shape[0]

  @pl.when(pl.program_id(0) == 0)
  def _():
    zbuf[...] = jnp.zeros_like(zbuf)
    for e in range(lt_ref.shape[0]):
      @pl.when(lt_ref[e] >= 0)
      def _():
        rows = [pl.multiple_of(lt_ref[e] + c * zr, SUBLANE) for c in range(tm // zr)]
        for r0 in rows:
          _row_copy(zbuf, xs_hbm, zsem, zr, 0, r0).start()
        for r0 in rows:
          _row_copy(zbuf, xs_hbm, zsem, zr, 0, r0).wait()

  def body(r, _):
    for c in range(2):
      _row_copy(x_ref, xs_hbm, sem, 1, r, pos_ref[0, 0, 2 * r + c]).start(priority=c)
    return 0

  lax.fori_loop(0, tt, body, 0, unroll=8)
  for c in range(2):
    _row_copy(x_ref, xs_hbm, sem, tt).wait()


def _dispatch(x2, pos, last_tile, n_rows, tt, tm):
  T = x2.shape[0]
  pos3 = pos.reshape(T // tt, 1, 2 * tt)
  zr = min(MOE_ZERO_ROWS, tm)
  return pl.pallas_call(
      functools.partial(_dispatch_kernel, tm=tm),
      grid=(T // tt,),
      in_specs=[pl.BlockSpec((1, 1, 2 * tt), lambda i: (i, 0, 0), memory_space=pltpu.SMEM),
                pl.BlockSpec(memory_space=pltpu.SMEM),
                pl.BlockSpec((tt, D_MODEL), lambda i: (i, 0))],
      out_specs=pl.BlockSpec(memory_space=pl.ANY),
      out_shape=jax.ShapeDtypeStruct((n_rows, D_MODEL), F32),
      scratch_shapes=[pltpu.VMEM((zr, D_MODEL), F32), pltpu.SemaphoreType.DMA(()), pltpu.SemaphoreType.DMA(())],
      compiler_params=_cparams(("arbitrary",)),
      name="moe_dispatch",
  )(pos3, last_tile, x2)


def _moe_group_kernel(te_ref, nu_ref, xs_ref, g_ref, w1_ref, w3_ref, w2_ref, ys_ref, h_ref, acc_ref):
  del te_ref
  i = pl.program_id(0)
  f = pl.program_id(1)
  last = pl.num_programs(1) - 1
  used = i < nu_ref[0]

  @pl.when(used & (f == 0))
  def _():
    h_ref[...] = _rms(xs_ref[...], g_ref[...]).astype(BF16)
    acc_ref[...] = jnp.zeros_like(acc_ref)

  @pl.when(used)
  def _():
    h = h_ref[...]
    g = jnp.dot(h, w1_ref[...].astype(BF16), preferred_element_type=F32)
    u = jnp.dot(h, w3_ref[...].astype(BF16), preferred_element_type=F32)
    a = (_silu(g) * u).astype(BF16)
    acc_ref[...] += jnp.dot(a, w2_ref[...].astype(BF16), preferred_element_type=F32)

  @pl.when(used & (f == last))
  def _():
    ys_ref[...] = acc_ref[...]

  @pl.when(jnp.logical_not(used) & (f == last))
  def _():
    ys_ref[...] = jnp.zeros_like(ys_ref)


def _moe_group(xs, gain, tile_expert, n_used, w1, w3, w2, tm, tf):
  P = xs.shape[0]
  F = w1.shape[2]
  nf = F // tf
  fidx = lambda i, f, te, nu: jnp.where(i < nu[0], f, nf - 1)
  grid_spec = pltpu.PrefetchScalarGridSpec(
      num_scalar_prefetch=2,
      grid=(P // tm, nf),
      in_specs=[
          pl.BlockSpec((tm, D_MODEL), lambda i, f, te, nu: (jnp.minimum(i, nu[0] - 1), 0)),
          pl.BlockSpec((1, D_MODEL), lambda i, f, te, nu: (0, 0)),
          pl.BlockSpec((None, D_MODEL, tf), lambda i, f, te, nu: (te[i], 0, fidx(i, f, te, nu))),
          pl.BlockSpec((None, D_MODEL, tf), lambda i, f, te, nu: (te[i], 0, fidx(i, f, te, nu))),
          pl.BlockSpec((None, tf, D_MODEL), lambda i, f, te, nu: (te[i], fidx(i, f, te, nu), 0)),
      ],
      out_specs=pl.BlockSpec((tm, D_MODEL), lambda i, f, te, nu: (i, 0)),
      scratch_shapes=[pltpu.VMEM((tm, D_MODEL), BF16), pltpu.VMEM((tm, D_MODEL), F32)],
  )
  return pl.pallas_call(
      _moe_group_kernel,
      grid_spec=grid_spec,
      out_shape=jax.ShapeDtypeStruct((P, D_MODEL), F32),
      compiler_params=_cparams(("arbitrary", "arbitrary")),
      name="moe_group",
  )(tile_expert, n_used, xs, gain, w1, w3, w2)


def _combine_kernel(pos_ref, x_ref, sel_ref, ys_hbm, o_ref, ybuf, sem):
  tt = x_ref.shape[0]

  def body(r, _):
    for c in range(2):
      _row_copy(ys_hbm, ybuf.at[c], sem, 1, pos_ref[0, 0, 2 * r + c], r).start(priority=c)
    return 0

  lax.fori_loop(0, tt, body, 0, unroll=8)
  for c in range(2):
    _row_copy(ys_hbm, ybuf.at[c], sem, tt).wait()
  sel = sel_ref[...]
  o_ref[...] = x_ref[...] + sel[:, 2:3] * ybuf[0] + sel[:, 3:4] * ybuf[1]


def _combine(x2, sel, pos, ys, tt):
  T = x2.shape[0]
  pos3 = pos.reshape(T // tt, 1, 2 * tt)
  return pl.pallas_call(
      _combine_kernel,
      grid=(T // tt,),
      in_specs=[pl.BlockSpec((1, 1, 2 * tt), lambda i: (i, 0, 0), memory_space=pltpu.SMEM),
                pl.BlockSpec((tt, D_MODEL), lambda i: (i, 0)),
                pl.BlockSpec((tt, LANE), lambda i: (i, 0)),
                pl.BlockSpec(memory_space=pl.ANY)],
      out_specs=pl.BlockSpec((tt, D_MODEL), lambda i: (i, 0)),
      out_shape=jax.ShapeDtypeStruct((T, D_MODEL), F32),
      scratch_shapes=[pltpu.VMEM((2, tt, D_MODEL), F32), pltpu.SemaphoreType.DMA(())],
      compiler_params=_cparams(("arbitrary",)),
      name="moe_combine",
  )(pos3, x2, sel, ys)


def _block_diag(w):
  n, d, e = w.shape
  eye = jnp.eye(n, dtype=w.dtype)
  return (eye[:, None, :, None] * w[:, :, None, :]).reshape(n * d, n * e)


def _slot_lanes(nope, rope):
  z = lambda n: jnp.zeros(nope.shape[:-1] + (n,), nope.dtype)
  cut = HEAD_SLOT // 2 - ROPE_HALF
  x1, x2 = (z(ROPE_HALF), z(ROPE_HALF)) if rope is None else (rope[..., :ROPE_HALF], rope[..., ROPE_HALF:])
  return jnp.concatenate([x1, nope[..., :cut], x2, nope[..., cut:], z(HEAD_SLOT - MLA_QK)], axis=-1)


def _head_slots(w, width):
  r = w.shape[0]
  w3 = w.reshape(r, MLA_HEADS, width)
  rope = w3[..., MLA_NOPE:] if width == MLA_QK else None
  return _slot_lanes(w3[..., :MLA_NOPE], rope).reshape(r, MLA_HEADS * HEAD_SLOT)


def _slot_gain(g):
  g = g.astype(F32).reshape(1, MLA_QK)
  return _slot_lanes(g[:, :MLA_NOPE], g[:, MLA_NOPE:])


def _value_slots(wv):
  r = wv.shape[0]
  w4 = wv.reshape(r, MLA_HEADS // 2, 2, MLA_V)
  z = jnp.zeros((r, MLA_HEADS // 2, HEAD_SLOT - MLA_V), wv.dtype)
  out = jnp.stack([jnp.concatenate([w4[:, :, 0], z], axis=-1), jnp.concatenate([z, w4[:, :, 1]], axis=-1)], axis=2)
  one = np.zeros((1, MLA_HEADS // 2, 2, HEAD_SLOT), np.float32)
  one[:, :, 0, MLA_V] = 1.0
  one[:, :, 1, 0] = 1.0
  return out.reshape(r, MLA_HEADS * HEAD_SLOT), jnp.asarray(one.reshape(1, MLA_HEADS * HEAD_SLOT))


def _score_bound(gq, gk):
  b = MLA_QK ** 0.5 * math.log2(math.e) * 1.02 * jnp.max(jnp.abs(gq)) * jnp.max(jnp.abs(gk))
  return b.astype(F32).reshape(1)


def kernel(x, positions, norm_mix, w_in, conv_w, conv_b, lru_wa, lru_ba, lru_wx, lru_bx, lru_lam,
           out_g_a, hg_lb_logits, hg_norm_g, mla_q_norm, mla_w_uq, mla_kv_norm, mla_w_ukv,
           qk_norm_q, qk_norm_k, out_g_c, w_out, norm_ffn, ffn_w_gate_up, ffn_w_down,
           moe_router, moe_w1, moe_w3, moe_w2):
  B, S, D = x.shape
  T = B * S
  depth = w_in.shape[0]
  tm = 512
  row = lambda v: v.astype(F32).reshape(1, -1)

  p = jax.nn.softmax(hg_lb_logits.astype(F32), axis=0)
  lower_bounds = jnp.cumsum(p, axis=0) - p[0:1]

  x2 = x.reshape(T, D)
  cos_t, sin_t = _rope_tables(positions.reshape(T, 1), tm)

  for l in range(depth):
    n_main = w_in.shape[2] - MLA_ROPE
    kr_slot_w = _slot_lanes(jnp.zeros((D, MLA_NOPE), F32), w_in[l][:, n_main:])
    w_in_l = jnp.concatenate([w_in[l][:, :n_main], kr_slot_w], axis=1).astype(BF16)
    wg_lru = jnp.concatenate(
        [_block_diag(lru_wa[l, 0]), _block_diag(lru_wx[l, 0]),
         _block_diag(lru_wa[l, 1]), _block_diag(lru_wx[l, 1])], axis=1).astype(BF16)
    bg_lru = jnp.concatenate([lru_ba[l, 0], lru_bx[l, 0], lru_ba[l, 1], lru_bx[l, 1]]).reshape(1, -1)
    wq = _head_slots(mla_w_uq[l], MLA_QK).astype(BF16)
    wkv = mla_w_ukv[l].reshape(MLA_KV_RANK, MLA_HEADS, MLA_NOPE + MLA_V)
    wk = _head_slots(wkv[:, :, :MLA_NOPE].reshape(MLA_KV_RANK, -1), MLA_NOPE).astype(BF16)
    wv, vone = _value_slots(wkv[:, :, MLA_NOPE:].reshape(MLA_KV_RANK, MLA_WIDTH))
    wv = wv.astype(BF16)

    pa, pb, pc = _norm_inproj(x2, row(norm_mix[l]), w_in_l, min(PROJ_ROWS, T))
    ya = _rglru(pa, conv_w[l], row(conv_b[l]), wg_lru, bg_lru, lru_lam[l], row(out_g_a[l]), B, S)
    yb = _hgrn2(pb, lower_bounds[l], row(jnp.tile(hg_norm_g[l], HG_HEADS)), B, S)
    q, k, v = _mla_prep(pc, cos_t, sin_t, row(mla_q_norm[l]), row(mla_kv_norm[l]), wq, wk, wv, vone,
                        _slot_gain(qk_norm_q[l]) * (MLA_QK ** -0.5 * math.log2(math.e)),
                        _slot_gain(qk_norm_k[l]), B, S, min(MLA_PREP_ROWS, S))
    yc = _attention(_score_bound(qk_norm_q[l], qk_norm_k[l]), q, k, v, B, S, min(ATTN_Q_BLOCK, S))
    if l % 2 == 0:
      wgu = ffn_w_gate_up[l // 2]
      pad = D_FF_PAD - D_FF
      wg = jnp.pad(wgu[:, :D_FF], ((0, 0), (0, pad))).astype(BF16)
      wu = jnp.pad(wgu[:, D_FF:], ((0, 0), (0, pad))).astype(BF16)
      wd = jnp.pad(ffn_w_down[l // 2], ((0, pad), (0, 0))).astype(BF16)
      x2 = _outproj_ffn(x2, ya, yb, yc, row(out_g_c[l]), w_out[l].astype(BF16), row(norm_ffn[l]), wg, wu, wd, tm)
    else:
      x2 = _outproj(x2, ya, yb, yc, row(out_g_c[l]), w_out[l].astype(BF16), min(PROJ_ROWS, T))
      wr = jnp.pad(moe_router[l // 2].astype(F32), ((0, 0), (0, LANE - N_EXPERTS)))
      gain = row(norm_ffn[l])
      tme = 1024 if T >= 8192 else 128
      sel = _router(x2, gain, wr, tm)
      pos, tile_expert, n_used, last_tile, n_tiles = _route_plan(sel, tme)
      xs = _dispatch(x2, pos, last_tile, n_tiles * tme, tm, tme)
      ys = _moe_group(xs, gain, tile_expert, n_used, moe_w1[l // 2], moe_w3[l // 2], moe_w2[l // 2],
                      tme, D_EXPERT // 7)
      x2 = _combine(x2, sel, pos, ys, tm)
  return x2.reshape(B, S, D)
```
